```python
import jax, jax.numpy as jnp
from jax import lax
import numpy as np

D_MODEL = 1024
BATCH = 8
SEQ = 2048
DEPTH = 2
DEC_BATCH = 128
DEC_SEQ = 4
PAST_LEN = 16384
PAGE_SIZE = 128

N_HEADS = 8
HEAD_K = 128
HEAD_V = 128
QK_W = N_HEADS * HEAD_K
V_W = N_HEADS * HEAD_V
CONV_W = 4
CONV_CH = 2 * QK_W + V_W
CHUNK = 64
POOL_GROUPS = 4
POOL_WINDOWS = (2, 4, 8, 16)
POOL_W = D_MODEL
POOL_GC = POOL_W // POOL_GROUPS
POOL_HIST = max(POOL_WINDOWS) - 1
PLE_DIM = 256
IN_SIZES = (QK_W, QK_W, V_W, V_W, N_HEADS, N_HEADS, POOL_W, POOL_W, D_MODEL, D_MODEL)
IN_W = sum(IN_SIZES)
IN_SPLITS = tuple(int(s) for s in np.cumsum(IN_SIZES)[:-1])
EPS = 1e-6

kernel_name = 'hybrid_gdn_pool_decoder_step'


def rmsnorm(x, w):
    xf = x.astype(jnp.float32)
    y = xf * lax.rsqrt(jnp.mean(xf * xf, axis=-1, keepdims=True) + EPS)
    return (y * w.astype(jnp.float32)).astype(x.dtype)


def l2norm(x):
    xf = x.astype(jnp.float32)
    return xf * lax.rsqrt(jnp.sum(xf * xf, axis=-1, keepdims=True) + EPS)


def causal_conv(xc, hist, w):
    full = jnp.concatenate([hist.astype(xc.dtype), xc], axis=1)
    T = xc.shape[1]
    out = full[:, 0:T] * w[0]
    for j in range(1, CONV_W):
        out = out + full[:, j:j + T] * w[j]
    return jax.nn.silu(out), full[:, -(CONV_W - 1):]


def gated_delta_chunked(q, k, v, g, beta, s0, chunk):
    B, T, H, K = q.shape
    V = v.shape[-1]
    n = T // chunk

    def blk(t):
        t = t.reshape((B, n, chunk, H) + t.shape[3:])
        return jnp.moveaxis(t, 3, 1)

    qc, kc, vc, gc, bc = blk(q), blk(k), blk(v), blk(g), blk(beta)
    G = jnp.cumsum(gc, axis=-1)
    idx = jnp.arange(chunk)
    lower = idx[:, None] >= idx[None, :]
    strict = idx[:, None] > idx[None, :]
    decay = jnp.exp(jnp.where(lower, G[..., :, None] - G[..., None, :], -jnp.inf))
    kk = jnp.einsum('bhnik,bhnjk->bhnij', kc, kc)
    a_mat = jnp.where(strict, kk * decay * bc[..., :, None], 0.0)
    eye = jnp.eye(chunk, dtype=jnp.float32)
    rhs = jnp.concatenate([vc * bc[..., None], kc * (bc * jnp.exp(G))[..., None]], axis=-1)
    sol = lax.linalg.triangular_solve(eye + a_mat, rhs, left_side=True, lower=True)
    w_val, w_key = sol[..., :V], sol[..., V:]
    qk = jnp.einsum('bhnik,bhnjk->bhnij', qc, kc) * decay
    q_dec = qc * jnp.exp(G)[..., None]
    k_tail = kc * jnp.exp(G[..., -1:] - G)[..., None]
    g_last = jnp.exp(G[..., -1])

    def step(s, xs):
        wv, wk, qk_i, qd, kt, gl = xs
        u = wv - jnp.einsum('bhck,bhkv->bhcv', wk, s)
        o = jnp.einsum('bhck,bhkv->bhcv', qd, s) + jnp.einsum('bhij,bhjv->bhiv', qk_i, u)
        s = s * gl[..., None, None] + jnp.einsum('bhck,bhcv->bhkv', kt, u)
        return s, o

    xs = tuple(jnp.moveaxis(t, 2, 0) for t in (w_val, w_key, qk, q_dec, k_tail, g_last))
    s_fin, o = lax.scan(step, s0, xs)
    o = jnp.transpose(o, (1, 0, 3, 2, 4)).reshape(B, T, H, V)
    return o, s_fin


def pool_mix(u, hist, pos0, w_grp, scale):
    B, T, _ = u.shape
    full = jnp.concatenate([hist.astype(u.dtype), u], axis=1)
    cs = jnp.cumsum(full.astype(jnp.float32), axis=1)
    cs = jnp.concatenate([jnp.zeros_like(cs[:, :1]), cs], axis=1)
    cs = cs.reshape(B, POOL_HIST + T + 1, POOL_GROUPS, POOL_GC)
    win = jnp.array(POOL_WINDOWS, dtype=jnp.int32)
    t = jnp.arange(T, dtype=jnp.int32)
    lo = (POOL_HIST + 1 + t)[:, None] - win[None, :]
    cs_hi = cs[:, POOL_HIST + 1:]
    cs_lo = cs[:, lo, jnp.arange(POOL_GROUPS)[None, :]]
    count = jnp.minimum((pos0 + 1 + t)[:, None], win[None, :]).astype(jnp.float32)
    ug = u.astype(jnp.float32).reshape(B, T, POOL_GROUPS, POOL_GC)
    y = (cs_hi - cs_lo) / count[None, :, :, None] - ug
    y = jnp.einsum('btgc,gcd->btgd', y, w_grp.astype(jnp.float32)).reshape(B, T, POOL_W)
    y = y * scale.astype(jnp.float32)
    return y.astype(u.dtype), full[:, -POOL_HIST:]


def mixer_layer(x, p_i, conv_hist, s0, pool_hist, pos0, norm_mix, w_in, conv_w, a_log, dt_bias, gdn_norm,
                w_proj_a, pool_w, pool_scale, w_proj_b, w_out, norm_ple, w_ple_gate, w_ple_proj):
    B, T, _ = x.shape
    h = rmsnorm(x, norm_mix)
    proj = h @ w_in
    q_r, k_r, v_r, z, a_r, b_r, u, gp, ga, gb = jnp.split(proj, IN_SPLITS, axis=-1)
    qkv, conv_new = causal_conv(jnp.concatenate([q_r, k_r, v_r], axis=-1), conv_hist, conv_w)
    q, k, v = jnp.split(qkv, (QK_W, 2 * QK_W), axis=-1)
    q = l2norm(q.reshape(B, T, N_HEADS, HEAD_K)) * (HEAD_K ** -0.5)
    k = l2norm(k.reshape(B, T, N_HEADS, HEAD_K))
    v = v.reshape(B, T, N_HEADS, HEAD_V).astype(jnp.float32)
    g = -jnp.exp(a_log.astype(jnp.float32)) * jax.nn.softplus(a_r.astype(jnp.float32) + dt_bias.astype(jnp.float32))
    beta = jax.nn.sigmoid(b_r.astype(jnp.float32))
    chunk = CHUNK if T % CHUNK == 0 else T
    o, s_new = gated_delta_chunked(q, k, v, g, beta, s0.astype(jnp.float32), chunk)
    o = rmsnorm(o, gdn_norm).astype(x.dtype) * jax.nn.silu(z.reshape(B, T, N_HEADS, HEAD_V))
    y_a = o.reshape(B, T, V_W) @ w_proj_a
    y_pool, pool_new = pool_mix(u, pool_hist, pos0, pool_w, pool_scale)
    y_b = (y_pool * jax.nn.silu(gp)) @ w_proj_b
    m = jax.nn.sigmoid(ga) * y_a + jax.nn.sigmoid(gb) * y_b
    x = x + m @ w_out
    x = x + jax.nn.sigmoid(rmsnorm(x, norm_ple) @ w_ple_gate) * (p_i @ w_ple_proj)
    return x, conv_new, s_new.astype(x.dtype), pool_new


def run_group(x, p, conv_state, delta_state, pool_state, pos0, norm_mix, w_in, conv_w, a_log, dt_bias, gdn_norm,
              w_proj_a, pool_w, pool_scale, w_proj_b, w_out, norm_ple, w_ple_gate, w_ple_proj, final_norm):
    B = x.shape[0]
    convs, deltas, pools = [], [], []
    for i in range(DEPTH):
        if conv_state is None:
            ch = jnp.zeros((B, CONV_W - 1, CONV_CH), x.dtype)
            s0 = jnp.zeros((B, N_HEADS, HEAD_K, HEAD_V), jnp.float32)
            ph = jnp.zeros((B, POOL_HIST, POOL_W), x.dtype)
        else:
            ch, s0, ph = conv_state[i], delta_state[i], pool_state[i]
        x, c_new, s_new, p_new = mixer_layer(
            x, p[i], ch, s0, ph, pos0, norm_mix[i], w_in[i], conv_w[i], a_log[i], dt_bias[i], gdn_norm[i],
            w_proj_a[i], pool_w[i], pool_scale[i], w_proj_b[i], w_out[i], norm_ple[i], w_ple_gate[i], w_ple_proj[i])
        convs.append(c_new)
        deltas.append(s_new)
        pools.append(p_new)
    y = rmsnorm(x, final_norm)
    return y, jnp.stack(convs), jnp.stack(deltas), jnp.stack(pools)


def setup_inputs(seed: int = 0) -> dict:
    key = jax.random.key(seed)
    ks = jax.random.split(key, 24)
    f32 = jnp.float32
    nrm = lambda k, shape, s: (jax.random.normal(k, shape, f32) * s)
    dt = jnp.exp(jax.random.uniform(ks[9], (DEPTH, N_HEADS), f32, np.log(1e-3), np.log(1e-1)))
    return {
        'x_prompt': nrm(ks[0], (BATCH, SEQ, D_MODEL), 1.0),
        'x_sample': nrm(ks[1], (DEC_BATCH, DEC_SEQ, D_MODEL), 1.0),
        'p_prompt': nrm(ks[2], (DEPTH, BATCH, SEQ, PLE_DIM), 1.0),
        'p_sample': nrm(ks[3], (DEPTH, DEC_BATCH, DEC_SEQ, PLE_DIM), 1.0),
        'state_conv': nrm(ks[4], (DEPTH, DEC_BATCH, CONV_W - 1, CONV_CH), 1.0),
        'state_delta': nrm(ks[5], (DEPTH, DEC_BATCH, N_HEADS, HEAD_K, HEAD_V), HEAD_K ** -0.5),
        'state_pool': nrm(ks[6], (DEPTH, DEC_BATCH, POOL_HIST, POOL_W), 1.0),
        'norm_mix': 1.0 + nrm(ks[7], (DEPTH, D_MODEL), 0.02),
        'w_in': nrm(ks[8], (DEPTH, D_MODEL, IN_W), D_MODEL ** -0.5),
        'conv_w': nrm(ks[10], (DEPTH, CONV_W, CONV_CH), 0.5),
        'a_log': jnp.log(jax.random.uniform(ks[11], (DEPTH, N_HEADS), f32, 1.0, 16.0)),
        'dt_bias': dt + jnp.log(-jnp.expm1(-dt)),
        'gdn_norm': 1.0 + nrm(ks[12], (DEPTH, HEAD_V), 0.02),
        'w_proj_a': nrm(ks[13], (DEPTH, V_W, D_MODEL), V_W ** -0.5),
        'pool_w': nrm(ks[14], (DEPTH, POOL_GROUPS, POOL_GC, POOL_GC), POOL_GC ** -0.5),
        'pool_scale': 1.0 + nrm(ks[15], (DEPTH, POOL_W), 0.1),
        'w_proj_b': nrm(ks[16], (DEPTH, POOL_W, D_MODEL), POOL_W ** -0.5),
        'w_out': nrm(ks[17], (DEPTH, D_MODEL, D_MODEL), D_MODEL ** -0.5),
        'norm_ple': 1.0 + nrm(ks[18], (DEPTH, D_MODEL), 0.02),
        'w_ple_gate': nrm(ks[19], (DEPTH, D_MODEL, D_MODEL), D_MODEL ** -0.5),
        'w_ple_proj': nrm(ks[20], (DEPTH, PLE_DIM, D_MODEL), PLE_DIM ** -0.5),
        'final_norm': 1.0 + nrm(ks[21], (D_MODEL,), 0.02),
    }


def reference(x_prompt, x_sample, p_prompt, p_sample, state_conv, state_delta, state_pool, norm_mix, w_in, conv_w,
              a_log, dt_bias, gdn_norm, w_proj_a, pool_w, pool_scale, w_proj_b, w_out, norm_ple, w_ple_gate,
              w_ple_proj, final_norm):
    y_prompt, conv_p, delta_p, pool_p = run_group(
        x_prompt, p_prompt, None, None, None, 0, norm_mix, w_in, conv_w, a_log, dt_bias, gdn_norm, w_proj_a,
        pool_w, pool_scale, w_proj_b, w_out, norm_ple, w_ple_gate, w_ple_proj, final_norm)
    y_sample, conv_s, delta_s, pool_s = run_group(
        x_sample, p_sample, state_conv, state_delta, state_pool, PAST_LEN, norm_mix, w_in, conv_w, a_log, dt_bias,
        gdn_norm, w_proj_a, pool_w, pool_scale, w_proj_b, w_out, norm_ple, w_ple_gate, w_ple_proj, final_norm)
    return (y_prompt, y_sample, conv_p, delta_p, pool_p, conv_s, delta_s, pool_s)
```

```python
import functools

import jax
import jax.numpy as jnp
from jax import lax
from jax.experimental import pallas as pl
from jax.experimental.pallas import tpu as pltpu

F32 = jnp.float32
BF16 = jnp.bfloat16

D_MODEL = 1024
N_HEADS = 8
HEAD_DIM = 128
CONV_W = 4
CONV_CH = 3 * D_MODEL
POOL_WINDOWS = (2, 4, 8, 16)
POOL_HIST = 15
POOL_GC = 256
EPS = 1e-6
PAST_LEN = 16384
CHUNK = 64
BLOCK = 2 * CHUNK
VMEM_LIMIT = 56 * 1024 * 1024

_C_QKV, _C_Z, _C_U, _C_GP, _C_GA, _C_GB, _C_A, _C_B = 0, 3072, 4096, 5120, 6144, 7168, 8192, 8320
_W_COLS = 8448


def _dot(a, b, nt=False):
    dims = (((1,), (1,)), ((), ())) if nt else (((1,), (0,)), ((), ()))
    return lax.dot_general(a, b, dims, preferred_element_type=F32)


def _mm(a, b, nt=False):
    return _dot(a.astype(BF16), b.astype(BF16), nt)


def _split3(x):
    x0 = x.astype(BF16)
    r = x - x0.astype(F32)
    x1 = r.astype(BF16)
    x2 = (r - x1.astype(F32)).astype(BF16)
    return x0, x1, x2


def _mm_exact_rhs(a_bf, b):
    b0, b1, b2 = _split3(b)
    return _dot(a_bf, b0) + _dot(a_bf, b1) + _dot(a_bf, b2)


def _mm_exact_lhs(a, b_bf):
    a0, a1, a2 = _split3(a)
    return _dot(a0, b_bf) + _dot(a1, b_bf) + _dot(a2, b_bf)


def _silu(x):
    return x * jax.nn.sigmoid(x)


def _rmsnorm(x, w):
    return x * lax.rsqrt(jnp.mean(x * x, axis=-1, keepdims=True) + EPS) * w


def _full(shape):
    n = len(shape)
    return pl.BlockSpec(shape, lambda *_: (0,) * n, pipeline_mode=pl.Buffered(1))


def _mixin_core(h, w_ref, cw_ref, alog_ref, dtb_ref, pw_ref, ps_ref, outs, pos, conv_tap, pool_tap, store):
    q_ref, k_ref, v_ref, g_ref, b_ref, zs_ref, ypg_ref, sga_ref, sgb_ref = outs
    xcs = []
    for seg, out_ref in enumerate((q_ref, k_ref, v_ref)):
        c0 = seg * D_MODEL
        xc = _dot(h, w_ref[:, _C_QKV + c0:_C_QKV + c0 + D_MODEL])
        xcs.append(xc)
        acc = xc * cw_ref[CONV_W - 1:CONV_W, c0:c0 + D_MODEL]
        for j in range(CONV_W - 1):
            acc = acc + conv_tap(j, c0, xc) * cw_ref[j:j + 1, c0:c0 + D_MODEL]
        y = _silu(acc)
        if seg < 2:
            for hh in range(N_HEADS):
                ls = slice(hh * HEAD_DIM, (hh + 1) * HEAD_DIM)
                yh = y[:, ls]
                inv = lax.rsqrt(jnp.sum(yh * yh, axis=-1, keepdims=True) + EPS)
                if seg == 0:
                    inv = inv * (HEAD_DIM ** -0.5)
                store(out_ref, yh * inv, ls)
        else:
            store(out_ref, y, None)

    a_r = _dot(h, w_ref[:, _C_A:_C_A + 128])
    b_r = _dot(h, w_ref[:, _C_B:_C_B + 128])
    xs = a_r + dtb_ref[...]
    softplus = jnp.maximum(xs, 0.0) + jnp.log1p(jnp.exp(-jnp.abs(xs)))
    store(g_ref, -jnp.exp(alog_ref[...]) * softplus, None)
    store(b_ref, jax.nn.sigmoid(b_r), None)

    store(zs_ref, _silu(_dot(h, w_ref[:, _C_Z:_C_Z + D_MODEL])), None)
    store(sga_ref, jax.nn.sigmoid(_dot(h, w_ref[:, _C_GA:_C_GA + D_MODEL])), None)
    store(sgb_ref, jax.nn.sigmoid(_dot(h, w_ref[:, _C_GB:_C_GB + D_MODEL])), None)

    u = _dot(h, w_ref[:, _C_U:_C_U + D_MODEL])
    sgp = _silu(_dot(h, w_ref[:, _C_GP:_C_GP + D_MODEL]))
    for gi, win in enumerate(POOL_WINDOWS):
        l0 = gi * POOL_GC
        ls = slice(l0, l0 + POOL_GC)
        acc = u[:, ls]
        for s in range(1, win):
            acc = acc + pool_tap(s, l0, u)
        count = jnp.minimum(pos + 1, win).astype(F32)
        y = acc / count - u[:, ls]
        yp = _mm(y, pw_ref[gi]) * ps_ref[:, ls]
        store(ypg_ref, yp * sgp[:, ls], ls)
    return xcs, u


def _mixin_prompt_body(x_ref, nm_ref, w_ref, cw_ref, alog_ref, dtb_ref, pw_ref, ps_ref,
                       q_ref, k_ref, v_ref, g_ref, b_ref, zs_ref, ypg_ref, sga_ref, sgb_ref, cnew_ref, pnew_ref,
                       cext, pext, *, rows):
    hc, hp = 8, 16
    t = pl.program_id(1)

    @pl.when(t == 0)
    def _():
        cext[0:hc, :] = jnp.zeros((hc, CONV_CH), F32)
        pext[0:hp, :] = jnp.zeros((hp, D_MODEL), F32)

    h = _rmsnorm(x_ref[...], nm_ref[...]).astype(BF16)

    def conv_tap(j, c0, xc):
        if j == 0:
            cext[hc:hc + rows, c0:c0 + D_MODEL] = xc
        off = hc - (CONV_W - 1 - j)
        return cext[off:off + rows, c0:c0 + D_MODEL]

    def pool_tap(s, l0, u):
        if s == 1 and l0 == 0:
            pext[hp:hp + rows, :] = u
        return pext[hp - s:hp - s + rows, l0:l0 + POOL_GC]

    def store(ref, val, ls):
        if ls is None:
            ref[...] = val
        else:
            ref[:, ls] = val

    pos = t * rows + lax.broadcasted_iota(jnp.int32, (rows, 1), 0)
    outs = (q_ref, k_ref, v_ref, g_ref, b_ref, zs_ref, ypg_ref, sga_ref, sgb_ref)
    _mixin_core(h, w_ref, cw_ref, alog_ref, dtb_ref, pw_ref, ps_ref, outs, pos, conv_tap, pool_tap, store)
    cnew_ref[...] = cext[hc + rows - (CONV_W - 1):hc + rows, :]
    pnew_ref[...] = pext[hp + rows - POOL_HIST:hp + rows, :]
    cext[0:hc, :] = cext[rows:rows + hc, :]
    pext[0:hp, :] = pext[rows:rows + hp, :]


def _mixin_prompt(x, nm, w_big, cw, alog, dtb, pw, ps, *, batch, seq, tile):
    m = x.shape[0]
    nt = seq // tile
    row_spec = lambda c: pl.BlockSpec((tile, c), lambda b, t: (b * nt + t, 0))
    in_specs = [row_spec(D_MODEL), _full((1, D_MODEL)), _full((D_MODEL, _W_COLS)), _full((CONV_W, CONV_CH)),
                _full((1, 128)), _full((1, 128)), _full((4, POOL_GC, POOL_GC)), _full((1, D_MODEL))]
    cnew_spec = pl.BlockSpec((None, CONV_W - 1, CONV_CH), lambda b, t: (b, 0, 0))
    pnew_spec = pl.BlockSpec((None, POOL_HIST, D_MODEL), lambda b, t: (b, 0, 0))
    big = jax.ShapeDtypeStruct((m, D_MODEL), F32)
    small = jax.ShapeDtypeStruct((m, 128), F32)
    out_shape = [big, big, big, small, small, big, big, big, big,
                 jax.ShapeDtypeStruct((batch, CONV_W - 1, CONV_CH), F32),
                 jax.ShapeDtypeStruct((batch, POOL_HIST, D_MODEL), F32)]
    out_specs = [row_spec(D_MODEL)] * 3 + [row_spec(128)] * 2 + [row_spec(D_MODEL)] * 4 + [cnew_spec, pnew_spec]
    return pl.pallas_call(
        functools.partial(_mixin_prompt_body, rows=tile),
        grid=(batch, nt), in_specs=in_specs, out_specs=out_specs, out_shape=out_shape,
        scratch_shapes=[pltpu.VMEM((8 + tile, CONV_CH), F32), pltpu.VMEM((16 + tile, D_MODEL), F32)],
        compiler_params=pltpu.CompilerParams(dimension_semantics=("arbitrary", "arbitrary"),
                                             vmem_limit_bytes=VMEM_LIMIT),
        name="mixin_prompt",
    )(x, nm, w_big, cw, alog, dtb, pw, ps)


def _mixin_sample_body(x_ref, nm_ref, w_ref, cw_ref, alog_ref, dtb_ref, pw_ref, ps_ref, chist_ref, phist_ref,
                       q_ref, k_ref, v_ref, g_ref, b_ref, zs_ref, ypg_ref, sga_ref, sgb_ref, cnew_ref, pnew_ref,
                       *, steps, bb, pos0):
    rows = steps * bb
    h = _rmsnorm(x_ref[...].reshape(rows, D_MODEL), nm_ref[...]).astype(BF16)

    def delayed(new, hist_ref, nhist, d, ls_new, ls_hist):
        parts = []
        for t in range(steps):
            src = t - d
            parts.append(new[src * bb:(src + 1) * bb, ls_new] if src >= 0 else hist_ref[nhist + src, :, ls_hist])
        return jnp.concatenate(parts, axis=0)

    def conv_tap(j, c0, xc):
        return delayed(xc, chist_ref, CONV_W - 1, CONV_W - 1 - j, slice(None), slice(c0, c0 + D_MODEL))

    def pool_tap(s, l0, u):
        ls = slice(l0, l0 + POOL_GC)
        return delayed(u, phist_ref, POOL_HIST, s, ls, ls)

    def store(ref, val, ls):
        val = val.reshape(steps, bb, val.shape[-1])
        if ls is None:
            ref[...] = val
        else:
            ref[:, :, ls] = val

    pos = pos0 + lax.broadcasted_iota(jnp.int32, (rows, 1), 0) // bb
    outs = (q_ref, k_ref, v_ref, g_ref, b_ref, zs_ref, ypg_ref, sga_ref, sgb_ref)
    xcs, u = _mixin_core(h, w_ref, cw_ref, alog_ref, dtb_ref, pw_ref, ps_ref, outs, pos, conv_tap, pool_tap, store)
    for i in range(CONV_W - 1):
        src = steps + i - (CONV_W - 1)
        for seg in range(3):
            ls = slice(seg * D_MODEL, (seg + 1) * D_MODEL)
            cnew_ref[i, :, ls] = (xcs[seg][src * bb:(src + 1) * bb, :] if src >= 0
                                  else chist_ref[CONV_W - 1 + src, :, ls])
    for i in range(POOL_HIST):
        src = steps + i - POOL_HIST
        pnew_ref[i] = u[src * bb:(src + 1) * bb, :] if src >= 0 else phist_ref[POOL_HIST + src]


def _mixin_sample(x, nm, w_big, cw, alog, dtb, pw, ps, chist, phist, *, bb, pos0):
    steps, nb, _ = x.shape
    slab_spec = lambda n, c: pl.BlockSpec((n, bb, c), lambda i: (0, i, 0))
    in_specs = [slab_spec(steps, D_MODEL), _full((1, D_MODEL)), _full((D_MODEL, _W_COLS)), _full((CONV_W, CONV_CH)),
                _full((1, 128)), _full((1, 128)), _full((4, POOL_GC, POOL_GC)), _full((1, D_MODEL)),
                slab_spec(CONV_W - 1, CONV_CH), slab_spec(POOL_HIST, D_MODEL)]
    big = jax.ShapeDtypeStruct((steps, nb, D_MODEL), F32)
    small = jax.ShapeDtypeStruct((steps, nb, 128), F32)
    out_shape = [big, big, big, small, small, big, big, big, big,
                 jax.ShapeDtypeStruct(chist.shape, F32), jax.ShapeDtypeStruct(phist.shape, F32)]
    out_specs = ([slab_spec(steps, D_MODEL)] * 3 + [slab_spec(steps, 128)] * 2 + [slab_spec(steps, D_MODEL)] * 4
                 + [slab_spec(CONV_W - 1, CONV_CH), slab_spec(POOL_HIST, D_MODEL)])
    return pl.pallas_call(
        functools.partial(_mixin_sample_body, steps=steps, bb=bb, pos0=pos0),
        grid=(nb // bb,), in_specs=in_specs, out_specs=out_specs, out_shape=out_shape,
        compiler_params=pltpu.CompilerParams(dimension_semantics=("arbitrary",), vmem_limit_bytes=VMEM_LIMIT),
        name="mixin_sample",
    )(x, nm, w_big, cw, alog, dtb, pw, ps, chist, phist)


def _delta_body(q_ref, k_ref, v_ref, g_ref, b_ref, o_ref, sout_ref, s_ref, *, tile):
    t = pl.program_id(1)

    @pl.when(t == 0)
    def _():
        s_ref[...] = jnp.zeros(s_ref.shape, F32)

    r = lax.broadcasted_iota(jnp.int32, (BLOCK, BLOCK), 0)
    c = lax.broadcasted_iota(jnp.int32, (BLOCK, BLOCK), 1)
    same = (r // CHUNK) == (c // CHUNK)
    low = same & (r >= c)
    strict = same & (r > c)
    eye = jnp.where(r == c, 1.0, 0.0).astype(F32)
    l_blk = jnp.where(low, 1.0, 0.0).astype(BF16)
    ones0 = jnp.where(c < CHUNK, 1.0, 0.0).astype(BF16)
    ones1 = jnp.where(c >= CHUNK, 1.0, 0.0).astype(BF16)
    first = lax.broadcasted_iota(jnp.int32, (BLOCK, 1), 0) < CHUNK
    zeros_half = jnp.zeros((CHUNK, HEAD_DIM), F32)

    def block(i, carry):
        r0 = pl.multiple_of(i * BLOCK, BLOCK)
        gp = g_ref[pl.ds(r0, BLOCK), :]
        bt = b_ref[pl.ds(r0, BLOCK), :]
        g_cum = _mm_exact_rhs(l_blk, gp)
        tot0 = _mm_exact_rhs(ones0, gp)
        tot1 = _mm_exact_rhs(ones1, gp)
        g_t = g_cum.T
        e_g = jnp.exp(g_cum)
        tot_sel = jnp.where(first, tot0, tot1)
        e_tail = jnp.exp(tot_sel - g_cum)
        e_tot0 = jnp.exp(tot0)
        e_tot1 = jnp.exp(tot1)
        for hh in range(N_HEADS):
            ls = slice(hh * HEAD_DIM, (hh + 1) * HEAD_DIM)
            qh = q_ref[pl.ds(r0, BLOCK), ls]
            kh = k_ref[pl.ds(r0, BLOCK), ls]
            vh = v_ref[pl.ds(r0, BLOCK), ls]
            bc = bt[:, hh:hh + 1]
            decay = jnp.exp(jnp.where(low, g_cum[:, hh:hh + 1] - g_t[hh:hh + 1, :], -jnp.inf))
            kk = _mm(kh, kh, nt=True)
            p = -jnp.where(strict, kk * decay * bc, 0.0)
            tinv = eye + p
            for _ in range(5):
                p = _mm(p, p)
                tinv = tinv + _mm(tinv, p)
            rhs = jnp.concatenate([vh * bc, kh * (bc * e_g[:, hh:hh + 1])], axis=1)
            sol = _mm(tinv, rhs)
            wv = sol[:, :HEAD_DIM]
            wk = sol[:, HEAD_DIM:]
            qk = _mm(qh, kh, nt=True) * decay
            qd = qh * e_g[:, hh:hh + 1]
            kt_t = (kh * e_tail[:, hh:hh + 1]).T.astype(BF16)
            s0 = s_ref[hh]
            r_a = _mm(jnp.concatenate([wk[:CHUNK], qd[:CHUNK]], axis=0), s0)
            u_a = jnp.concatenate([wv[:CHUNK] - r_a[:CHUNK], zeros_half], axis=0).astype(BF16)
            o_a = r_a[CHUNK:] + _dot(qk[:CHUNK].astype(BF16), u_a)
            s1 = s0 * e_tot0[:, hh:hh + 1] + _dot(kt_t, u_a)
            r_b = _mm(jnp.concatenate([wk[CHUNK:], qd[CHUNK:]], axis=0), s1)
            u_b = jnp.concatenate([zeros_half, wv[CHUNK:] - r_b[:CHUNK]], axis=0).astype(BF16)
            o_b = r_b[CHUNK:] + _dot(qk[CHUNK:].astype(BF16), u_b)
            s_ref[hh] = s1 * e_tot1[:, hh:hh + 1] + _dot(kt_t, u_b)
            o_ref[pl.ds(r0, BLOCK), ls] = jnp.concatenate([o_a, o_b], axis=0)
        return carry

    lax.fori_loop(0, tile // BLOCK, block, 0)
    sout_ref[...] = s_ref[...]


def _delta_prompt(q, k, v, g, beta, *, batch, seq, tile):
    nt = seq // tile
    row_spec = lambda c: pl.BlockSpec((tile, c), lambda b, t: (b * nt + t, 0))
    s_shape = (N_HEADS, HEAD_DIM, HEAD_DIM)
    return pl.pallas_call(
        functools.partial(_delta_body, tile=tile),
        grid=(batch, nt),
        in_specs=[row_spec(D_MODEL)] * 3 + [row_spec(128)] * 2,
        out_specs=[row_spec(D_MODEL), pl.BlockSpec((None,) + s_shape, lambda b, t: (b, 0, 0, 0))],
        out_shape=[jax.ShapeDtypeStruct(q.shape, F32), jax.ShapeDtypeStruct((batch,) + s_shape, F32)],
        scratch_shapes=[pltpu.VMEM(s_shape, F32)],
        compiler_params=pltpu.CompilerParams(dimension_semantics=("arbitrary", "arbitrary"),
                                             vmem_limit_bytes=VMEM_LIMIT),
        name="delta_prompt",
    )(q, k, v, g, beta)


def _head_indicators():
    d = lax.broadcasted_iota(jnp.int32, (D_MODEL, 128), 0) // HEAD_DIM
    hcol = lax.broadcasted_iota(jnp.int32, (D_MODEL, 128), 1)
    e_sum = jnp.where(d == hcol, 1.0, 0.0).astype(BF16)
    hrow = lax.broadcasted_iota(jnp.int32, (128, D_MODEL), 0)
    d2 = lax.broadcasted_iota(jnp.int32, (128, D_MODEL), 1) // HEAD_DIM
    e_bc = jnp.where(hrow == d2, 1.0, 0.0).astype(BF16)
    return e_sum, e_bc


def _sample_prep_body(q_ref, k_ref, v_ref, g_ref, b_ref, wkqd_ref, wv_ref, kt_ref, qkd_ref, glx_ref,
                      *, steps, nb):
    e_sum, e_bc = _head_indicators()
    expand = lambda x: _mm_exact_lhs(x, e_bc)
    hsum = lambda y: _mm_exact_lhs(y, e_sum)
    sl = lambda ref, i: ref[i * nb:(i + 1) * nb, :]
    q = [sl(q_ref, i) for i in range(steps)]
    k = [sl(k_ref, i) for i in range(steps)]
    v = [sl(v_ref, i) for i in range(steps)]
    beta = [sl(b_ref, i) for i in range(steps)]
    g_cum = []
    for i in range(steps):
        gi = sl(g_ref, i)
        g_cum.append(gi if i == 0 else g_cum[-1] + gi)
    wv, wk = [], []
    for i in range(steps):
        acc_v = v[i] * expand(beta[i])
        acc_k = k[i] * expand(beta[i] * jnp.exp(g_cum[i]))
        for j in range(i):
            a_ij = expand(hsum(k[i] * k[j]) * jnp.exp(g_cum[i] - g_cum[j]) * beta[i])
            acc_v = acc_v - a_ij * wv[j]
            acc_k = acc_k - a_ij * wk[j]
        wv.append(acc_v)
        wk.append(acc_k)
    zeros = jnp.zeros((nb, D_MODEL), F32)
    for i in range(steps):
        wkqd_ref[i * nb:(i + 1) * nb, :] = wk[i]
        wkqd_ref[(steps + i) * nb:(steps + i + 1) * nb, :] = q[i] * expand(jnp.exp(g_cum[i]))
        wv_ref[i * nb:(i + 1) * nb, :] = wv[i]
        wv_ref[(steps + i) * nb:(steps + i + 1) * nb, :] = zeros
        kt_ref[i * nb:(i + 1) * nb, :] = k[i] * expand(jnp.exp(g_cum[steps - 1] - g_cum[i]))
        kt_ref[(steps + i) * nb:(steps + i + 1) * nb, :] = zeros
        for j in range(steps):
            idx = i * steps + j
            if j <= i:
                qkd_ref[idx * nb:(idx + 1) * nb, :] = hsum(q[i] * k[j]) * jnp.exp(g_cum[i] - g_cum[j])
            else:
                qkd_ref[idx * nb:(idx + 1) * nb, :] = jnp.zeros((nb, 128), F32)
    glx_ref[...] = expand(jnp.exp(g_cum[steps - 1]))


def _sample_prep(q, k, v, g, beta, *, steps, nb):
    m = steps * nb
    out_shape = [jax.ShapeDtypeStruct((2 * m, D_MODEL), F32)] * 3 + [
        jax.ShapeDtypeStruct((steps * steps * nb, 128), F32), jax.ShapeDtypeStruct((nb, D_MODEL), F32)]
    return pl.pallas_call(
        functools.partial(_sample_prep_body, steps=steps, nb=nb),
        grid=(1,),
        in_specs=[_full((m, D_MODEL))] * 3 + [_full((m, 128))] * 2,
        out_specs=[_full(s.shape) for s in out_shape],
        out_shape=out_shape,
        compiler_params=pltpu.CompilerParams(dimension_semantics=("arbitrary",), vmem_limit_bytes=VMEM_LIMIT),
        name="sample_prep",
    )(q, k, v, g, beta)


def _sample_state_body(s_ref, wkqd_ref, wv_ref, kt_ref, glx_ref, r_ref, snew_ref, *, bb):
    zeros_pad = jnp.zeros((HEAD_DIM - 8, HEAD_DIM), F32)

    def per_seq(b, carry):
        for hh in range(N_HEADS):
            ls = slice(hh * HEAD_DIM, (hh + 1) * HEAD_DIM)
            s0 = s_ref[b, hh]
            lhs = jnp.concatenate([wkqd_ref[b, :, ls], zeros_pad], axis=0)
            res = _mm(lhs, s0)[:8]
            r_ref[b, :, ls] = res
            u = wv_ref[b, :, ls] - res
            kt_t = jnp.concatenate([kt_ref[b, :, ls], zeros_pad], axis=0).T
            upd = _mm(kt_t, jnp.concatenate([u, zeros_pad], axis=0))
            snew_ref[b, hh] = s0 * glx_ref[b, :, ls] + upd
        return carry

    lax.fori_loop(0, bb, per_seq, 0)


def _sample_state(s0, wkqd, wv, kt, glx, *, bb):
    nb = s0.shape[0]
    s_spec = pl.BlockSpec((bb, N_HEADS, HEAD_DIM, HEAD_DIM), lambda i: (i, 0, 0, 0))
    slot_spec = pl.BlockSpec((bb, 8, D_MODEL), lambda i: (i, 0, 0))
    return pl.pallas_call(
        functools.partial(_sample_state_body, bb=bb),
        grid=(nb // bb,),
        in_specs=[s_spec, slot_spec, slot_spec, slot_spec, pl.BlockSpec((bb, 1, D_MODEL), lambda i: (i, 0, 0))],
        out_specs=[slot_spec, s_spec],
        out_shape=[jax.ShapeDtypeStruct((nb, 8, D_MODEL), F32), jax.ShapeDtypeStruct(s0.shape, F32)],
        compiler_params=pltpu.CompilerParams(dimension_semantics=("arbitrary",), vmem_limit_bytes=VMEM_LIMIT),
        name="sample_state",
    )(s0, wkqd, wv, kt, glx.reshape(nb, 1, D_MODEL))


def _sample_out_body(r_ref, wv_ref, qkd_ref, o_ref, *, steps, nb):
    _, e_bc = _head_indicators()
    sl = lambda ref, i: ref[i * nb:(i + 1) * nb, :]
    u = [sl(wv_ref, j) - sl(r_ref, j) for j in range(steps)]
    for i in range(steps):
        acc = sl(r_ref, steps + i)
        for j in range(i + 1):
            acc = acc + _mm_exact_lhs(sl(qkd_ref, i * steps + j), e_bc) * u[j]
        o_ref[i * nb:(i + 1) * nb, :] = acc


def _sample_out(r, wv, qkd, *, steps, nb):
    m = steps * nb
    return pl.pallas_call(
        functools.partial(_sample_out_body, steps=steps, nb=nb),
        grid=(1,),
        in_specs=[_full(r.shape), _full(wv.shape), _full(qkd.shape)],
        out_specs=_full((m, D_MODEL)),
        out_shape=jax.ShapeDtypeStruct((m, D_MODEL), F32),
        compiler_params=pltpu.CompilerParams(dimension_semantics=("arbitrary",), vmem_limit_bytes=VMEM_LIMIT),
        name="sample_out",
    )(r, wv, qkd)


def _merge_body(o_ref, zs_ref, ypg_ref, sga_ref, sgb_ref, x_ref, p_ref, gn_ref, wpa_ref, wpb_ref, wout_ref,
                npl_ref, wpg_ref, wpp_ref, fn_ref, y_ref, *, final):
    gn = gn_ref[...]
    for hh in range(N_HEADS):
        ls = slice(hh * HEAD_DIM, (hh + 1) * HEAD_DIM)
        oh = o_ref[:, ls]
        on = oh * lax.rsqrt(jnp.mean(oh * oh, axis=-1, keepdims=True) + EPS) * gn
        part = _dot((on * zs_ref[:, ls]).astype(BF16), wpa_ref[ls, :])
        y_a = part if hh == 0 else y_a + part
    y_b = _mm(ypg_ref[...], wpb_ref[...])
    m = sga_ref[...] * y_a + sgb_ref[...] * y_b
    x1 = x_ref[...] + _mm(m, wout_ref[...])
    gate = jax.nn.sigmoid(_mm(_rmsnorm(x1, npl_ref[...]), wpg_ref[...]))
    x2 = x1 + gate * _mm(p_ref[...], wpp_ref[...])
    if final:
        x2 = _rmsnorm(x2, fn_ref[...])
    y_ref[...] = x2


def _merge(o, zs, ypg, sga, sgb, x, p, gn, wpa, wpb, wout, npl, wpg, wpp, fn, *, tile, final):
    m = x.shape[0]
    ple = p.shape[1]
    row_spec = lambda c: pl.BlockSpec((tile, c), lambda i: (i, 0))
    sq = _full((D_MODEL, D_MODEL))
    return pl.pallas_call(
        functools.partial(_merge_body, final=final),
        grid=(m // tile,),
        in_specs=[row_spec(D_MODEL)] * 6 + [row_spec(ple), _full((1, HEAD_DIM)), sq, sq, sq, _full((1, D_MODEL)),
                                            sq, _full((ple, D_MODEL)), _full((1, D_MODEL))],
        out_specs=row_spec(D_MODEL),
        out_shape=jax.ShapeDtypeStruct((m, D_MODEL), F32),
        compiler_params=pltpu.CompilerParams(dimension_semantics=("arbitrary",), vmem_limit_bytes=VMEM_LIMIT),
        name="merge",
    )(o, zs, ypg, sga, sgb, x, p, gn, wpa, wpb, wout, npl, wpg, wpp, fn)


def _pad_cols(w, n):
    return jnp.pad(w, ((0, 0), (0, n - w.shape[1])))


def _prep_layer(i, norm_mix, w_in, conv_w, a_log, dt_bias, gdn_norm, w_proj_a, pool_w, pool_scale, w_proj_b,
                w_out, norm_ple, w_ple_gate, w_ple_proj):
    w = w_in[i]
    qkvz = w[:, 0:4096]
    a_w = _pad_cols(w[:, 4096:4104], 128)
    b_w = _pad_cols(w[:, 4104:4112], 128)
    rest = w[:, 4112:8208]
    w_big = jnp.concatenate([qkvz, rest, a_w, b_w], axis=1).astype(BF16)
    row = lambda a: a.reshape(1, -1)
    return dict(
        nm=row(norm_mix[i]), w_big=w_big, cw=conv_w[i],
        alog=_pad_cols(row(a_log[i]), 128), dtb=_pad_cols(row(dt_bias[i]), 128),
        pw=pool_w[i].astype(BF16), ps=row(pool_scale[i]),
        gn=row(gdn_norm[i]), wpa=w_proj_a[i].astype(BF16), wpb=w_proj_b[i].astype(BF16),
        wout=w_out[i].astype(BF16), npl=row(norm_ple[i]), wpg=w_ple_gate[i].astype(BF16),
        wpp=w_ple_proj[i].astype(BF16))


def kernel(x_prompt, x_sample, p_prompt, p_sample, state_conv, state_delta, state_pool, norm_mix, w_in, conv_w,
           a_log, dt_bias, gdn_norm, w_proj_a, pool_w, pool_scale, w_proj_b, w_out, norm_ple, w_ple_gate,
           w_ple_proj, final_norm):
    depth = w_in.shape[0]
    batch, seq, _ = x_prompt.shape
    nb, steps, _ = x_sample.shape
    fn = final_norm.reshape(1, -1)
    layers = [_prep_layer(i, norm_mix, w_in, conv_w, a_log, dt_bias, gdn_norm, w_proj_a, pool_w, pool_scale,
                          w_proj_b, w_out, norm_ple, w_ple_gate, w_ple_proj) for i in range(depth)]

    xp = x_prompt.reshape(batch * seq, D_MODEL)
    conv_p, delta_p, pool_p = [], [], []
    for i, L in enumerate(layers):
        q, k, v, g, beta, zs, ypg, sga, sgb, cnew, pnew = _mixin_prompt(
            xp, L["nm"], L["w_big"], L["cw"], L["alog"], L["dtb"], L["pw"], L["ps"],
            batch=batch, seq=seq, tile=256)
        o, s_new = _delta_prompt(q, k, v, g, beta, batch=batch, seq=seq, tile=512)
        xp = _merge(o, zs, ypg, sga, sgb, xp, p_prompt[i].reshape(batch * seq, -1), L["gn"], L["wpa"], L["wpb"],
                    L["wout"], L["npl"], L["wpg"], L["wpp"], fn, tile=512, final=(i == depth - 1))
        conv_p.append(cnew)
        delta_p.append(s_new)
        pool_p.append(pnew)
    y_prompt = xp.reshape(batch, seq, D_MODEL)

    swap = lambda a: jnp.swapaxes(a, 0, 1)
    flat = lambda a: a.reshape(-1, a.shape[-1])
    bmajor = lambda a: swap(a.reshape(-1, nb, a.shape[-1]))
    xs = swap(x_sample)
    conv_s, delta_s, pool_s = [], [], []
    for i, L in enumerate(layers):
        q, k, v, g, beta, zs, ypg, sga, sgb, cnew, pnew = _mixin_sample(
            xs, L["nm"], L["w_big"], L["cw"], L["alog"], L["dtb"], L["pw"], L["ps"],
            swap(state_conv[i]), swap(state_pool[i]), bb=32, pos0=PAST_LEN)
        wkqd, wv, kt, qkd, glx = _sample_prep(flat(q), flat(k), flat(v), flat(g), flat(beta), steps=steps, nb=nb)
        r, s_new = _sample_state(state_delta[i], bmajor(wkqd), bmajor(wv), bmajor(kt), glx, bb=8)
        o = _sample_out(flat(swap(r)), wv, qkd, steps=steps, nb=nb)
        xs = _merge(o, flat(zs), flat(ypg), flat(sga), flat(sgb), flat(xs), flat(swap(p_sample[i])), L["gn"],
                    L["wpa"], L["wpb"], L["wout"], L["npl"], L["wpg"], L["wpp"], fn, tile=steps * nb,
                    final=(i == depth - 1)).reshape(steps, nb, D_MODEL)
        conv_s.append(swap(cnew))
        delta_s.append(s_new)
        pool_s.append(swap(pnew))
    y_sample = swap(xs)

    return (y_prompt, y_sample, jnp.stack(conv_p), jnp.stack(delta_p), jnp.stack(pool_p),
            jnp.stack(conv_s), jnp.stack(delta_s), jnp.stack(pool_s))
```

```python
import functools

import jax
import jax.numpy as jnp
from jax import lax
from jax.experimental import pallas as pl
from jax.experimental.pallas import tpu as pltpu

F32 = jnp.float32
BF16 = jnp.bfloat16

D_MODEL = 1024
N_HEADS = 8
HEAD_DIM = 128
CONV_W = 4
CONV_CH = 3 * D_MODEL
POOL_WINDOWS = (2, 4, 8, 16)
POOL_HIST = 15
POOL_GC = 256
EPS = 1e-6
PAST_LEN = 16384
CHUNK = 64
BLOCK = 2 * CHUNK
VMEM_LIMIT = 56 * 1024 * 1024

_C_QKV, _C_Z, _C_U, _C_GP, _C_GA, _C_GB, _C_A, _C_B = 0, 3072, 4096, 5120, 6144, 7168, 8192, 8320
_W_COLS = 8448


def _dot(a, b, nt=False):
    dims = (((1,), (1,)), ((), ())) if nt else (((1,), (0,)), ((), ()))
    return lax.dot_general(a, b, dims, preferred_element_type=F32)


def _mm(a, b, nt=False):
    return _dot(a.astype(BF16), b.astype(BF16), nt)


def _split3(x):
    x0 = x.astype(BF16)
    r = x - x0.astype(F32)
    x1 = r.astype(BF16)
    x2 = (r - x1.astype(F32)).astype(BF16)
    return x0, x1, x2


def _mm_exact_rhs(a_bf, b):
    b0, b1, b2 = _split3(b)
    return _dot(a_bf, b0) + _dot(a_bf, b1) + _dot(a_bf, b2)


def _mm_exact_lhs(a, b_bf):
    a0, a1, a2 = _split3(a)
    return _dot(a0, b_bf) + _dot(a1, b_bf) + _dot(a2, b_bf)


def _silu(x):
    return x * jax.nn.sigmoid(x)


def _rmsnorm(x, w):
    return x * lax.rsqrt(jnp.mean(x * x, axis=-1, keepdims=True) + EPS) * w


def _full(shape):
    n = len(shape)
    return pl.BlockSpec(shape, lambda *_: (0,) * n, pipeline_mode=pl.Buffered(1))


def _mixin_core(h, w_ref, cw_ref, alog_ref, dtb_ref, pw_ref, ps_ref, outs, pos, conv_tap, pool_tap, store):
    q_ref, k_ref, v_ref, g_ref, b_ref, zs_ref, ypg_ref, sga_ref, sgb_ref = outs
    xcs = []
    for seg, out_ref in enumerate((q_ref, k_ref, v_ref)):
        c0 = seg * D_MODEL
        xc = _dot(h, w_ref[:, _C_QKV + c0:_C_QKV + c0 + D_MODEL])
        xcs.append(xc)
        acc = xc * cw_ref[CONV_W - 1:CONV_W, c0:c0 + D_MODEL]
        for j in range(CONV_W - 1):
            acc = acc + conv_tap(j, c0, xc) * cw_ref[j:j + 1, c0:c0 + D_MODEL]
        y = _silu(acc)
        if seg < 2:
            for hh in range(N_HEADS):
                ls = slice(hh * HEAD_DIM, (hh + 1) * HEAD_DIM)
                yh = y[:, ls]
                inv = lax.rsqrt(jnp.sum(yh * yh, axis=-1, keepdims=True) + EPS)
                if seg == 0:
                    inv = inv * (HEAD_DIM ** -0.5)
                store(out_ref, yh * inv, ls)
        else:
            store(out_ref, y, None)

    a_r = _dot(h, w_ref[:, _C_A:_C_A + 128])
    b_r = _dot(h, w_ref[:, _C_B:_C_B + 128])
    xs = a_r + dtb_ref[...]
    softplus = jnp.maximum(xs, 0.0) + jnp.log1p(jnp.exp(-jnp.abs(xs)))
    store(g_ref, -jnp.exp(alog_ref[...]) * softplus, None)
    store(b_ref, jax.nn.sigmoid(b_r), None)

    store(zs_ref, _silu(_dot(h, w_ref[:, _C_Z:_C_Z + D_MODEL])), None)
    store(sga_ref, jax.nn.sigmoid(_dot(h, w_ref[:, _C_GA:_C_GA + D_MODEL])), None)
    store(sgb_ref, jax.nn.sigmoid(_dot(h, w_ref[:, _C_GB:_C_GB + D_MODEL])), None)

    u = _dot(h, w_ref[:, _C_U:_C_U + D_MODEL])
    sgp = _silu(_dot(h, w_ref[:, _C_GP:_C_GP + D_MODEL]))
    for gi, win in enumerate(POOL_WINDOWS):
        l0 = gi * POOL_GC
        ls = slice(l0, l0 + POOL_GC)
        acc = u[:, ls]
        for s in range(1, win):
            acc = acc + pool_tap(s, l0, u)
        count = jnp.minimum(pos + 1, win).astype(F32)
        y = acc / count - u[:, ls]
        yp = _mm(y, pw_ref[gi]) * ps_ref[:, ls]
        store(ypg_ref, yp * sgp[:, ls], ls)
    return xcs, u


def _mixin_prompt_body(x_ref, nm_ref, w_ref, cw_ref, alog_ref, dtb_ref, pw_ref, ps_ref,
                       q_ref, k_ref, v_ref, g_ref, b_ref, zs_ref, ypg_ref, sga_ref, sgb_ref, cnew_ref, pnew_ref,
                       cext, pext, *, rows):
    hc, hp = 8, 16
    t = pl.program_id(1)

    @pl.when(t == 0)
    def _():
        cext[0:hc, :] = jnp.zeros((hc, CONV_CH), F32)
        pext[0:hp, :] = jnp.zeros((hp, D_MODEL), F32)

    h = _rmsnorm(x_ref[...], nm_ref[...]).astype(BF16)

    def conv_tap(j, c0, xc):
        if j == 0:
            cext[hc:hc + rows, c0:c0 + D_MODEL] = xc
        off = hc - (CONV_W - 1 - j)
        return cext[off:off + rows, c0:c0 + D_MODEL]

    def pool_tap(s, l0, u):
        if s == 1 and l0 == 0:
            pext[hp:hp + rows, :] = u
        return pext[hp - s:hp - s + rows, l0:l0 + POOL_GC]

    def store(ref, val, ls):
        if ls is None:
            ref[...] = val
        else:
            ref[:, ls] = val

    pos = t * rows + lax.broadcasted_iota(jnp.int32, (rows, 1), 0)
    outs = (q_ref, k_ref, v_ref, g_ref, b_ref, zs_ref, ypg_ref, sga_ref, sgb_ref)
    _mixin_core(h, w_ref, cw_ref, alog_ref, dtb_ref, pw_ref, ps_ref, outs, pos, conv_tap, pool_tap, store)
    cnew_ref[...] = cext[hc + rows - (CONV_W - 1):hc + rows, :]
    pnew_ref[...] = pext[hp + rows - POOL_HIST:hp + rows, :]
    cext[0:hc, :] = cext[rows:rows + hc, :]
    pext[0:hp, :] = pext[rows:rows + hp, :]


def _mixin_prompt(x, nm, w_big, cw, alog, dtb, pw, ps, *, batch, seq, tile):
    m = x.shape[0]
    nt = seq // tile
    row_spec = lambda c: pl.BlockSpec((tile, c), lambda b, t: (b * nt + t, 0))
    in_specs = [row_spec(D_MODEL), _full((1, D_MODEL)), _full((D_MODEL, _W_COLS)), _full((CONV_W, CONV_CH)),
                _full((1, 128)), _full((1, 128)), _full((4, POOL_GC, POOL_GC)), _full((1, D_MODEL))]
    cnew_spec = pl.BlockSpec((None, CONV_W - 1, CONV_CH), lambda b, t: (b, 0, 0))
    pnew_spec = pl.BlockSpec((None, POOL_HIST, D_MODEL), lambda b, t: (b, 0, 0))
    big = jax.ShapeDtypeStruct((m, D_MODEL), F32)
    small = jax.ShapeDtypeStruct((m, 128), F32)
    out_shape = [big, big, big, small, small, big, big, big, big,
                 jax.ShapeDtypeStruct((batch, CONV_W - 1, CONV_CH), F32),
                 jax.ShapeDtypeStruct((batch, POOL_HIST, D_MODEL), F32)]
    out_specs = [row_spec(D_MODEL)] * 3 + [row_spec(128)] * 2 + [row_spec(D_MODEL)] * 4 + [cnew_spec, pnew_spec]
    return pl.pallas_call(
        functools.partial(_mixin_prompt_body, rows=tile),
        grid=(batch, nt), in_specs=in_specs, out_specs=out_specs, out_shape=out_shape,
        scratch_shapes=[pltpu.VMEM((8 + tile, CONV_CH), F32), pltpu.VMEM((16 + tile, D_MODEL), F32)],
        compiler_params=pltpu.CompilerParams(dimension_semantics=("arbitrary", "arbitrary"),
                                             vmem_limit_bytes=VMEM_LIMIT),
        name="mixin_prompt",
    )(x, nm, w_big, cw, alog, dtb, pw, ps)


def _mixin_sample_body(x_ref, nm_ref, w_ref, cw_ref, alog_ref, dtb_ref, pw_ref, ps_ref, chist_ref, phist_ref,
                       q_ref, k_ref, v_ref, g_ref, b_ref, zs_ref, ypg_ref, sga_ref, sgb_ref, cnew_ref, pnew_ref,
                       *, steps, bb, pos0):
    rows = steps * bb
    h = _rmsnorm(x_ref[...].reshape(rows, D_MODEL), nm_ref[...]).astype(BF16)

    def delayed(new, hist_ref, nhist, d, ls_new, ls_hist):
        parts = []
        for t in range(steps):
            src = t - d
            parts.append(new[src * bb:(src + 1) * bb, ls_new] if src >= 0 else hist_ref[nhist + src, :, ls_hist])
        return jnp.concatenate(parts, axis=0)

    def conv_tap(j, c0, xc):
        return delayed(xc, chist_ref, CONV_W - 1, CONV_W - 1 - j, slice(None), slice(c0, c0 + D_MODEL))

    def pool_tap(s, l0, u):
        ls = slice(l0, l0 + POOL_GC)
        return delayed(u, phist_ref, POOL_HIST, s, ls, ls)

    def store(ref, val, ls):
        val = val.reshape(steps, bb, val.shape[-1])
        if ls is None:
            ref[...] = val
        else:
            ref[:, :, ls] = val

    pos = pos0 + lax.broadcasted_iota(jnp.int32, (rows, 1), 0) // bb
    outs = (q_ref, k_ref, v_ref, g_ref, b_ref, zs_ref, ypg_ref, sga_ref, sgb_ref)
    xcs, u = _mixin_core(h, w_ref, cw_ref, alog_ref, dtb_ref, pw_ref, ps_ref, outs, pos, conv_tap, pool_tap, store)
    for i in range(CONV_W - 1):
        src = steps + i - (CONV_W - 1)
        for seg in range(3):
            ls = slice(seg * D_MODEL, (seg + 1) * D_MODEL)
            cnew_ref[i, :, ls] = (xcs[seg][src * bb:(src + 1) * bb, :] if src >= 0
                                  else chist_ref[CONV_W - 1 + src, :, ls])
    for i in range(POOL_HIST):
        src = steps + i - POOL_HIST
        pnew_ref[i] = u[src * bb:(src + 1) * bb, :] if src >= 0 else phist_ref[POOL_HIST + src]


def _mixin_sample(x, nm, w_big, cw, alog, dtb, pw, ps, chist, phist, *, bb, pos0):
    steps, nb, _ = x.shape
    slab_spec = lambda n, c: pl.BlockSpec((n, bb, c), lambda i: (0, i, 0))
    in_specs = [slab_spec(steps, D_MODEL), _full((1, D_MODEL)), _full((D_MODEL, _W_COLS)), _full((CONV_W, CONV_CH)),
                _full((1, 128)), _full((1, 128)), _full((4, POOL_GC, POOL_GC)), _full((1, D_MODEL)),
                slab_spec(CONV_W - 1, CONV_CH), slab_spec(POOL_HIST, D_MODEL)]
    big = jax.ShapeDtypeStruct((steps, nb, D_MODEL), F32)
    small = jax.ShapeDtypeStruct((steps, nb, 128), F32)
    out_shape = [big, big, big, small, small, big, big, big, big,
                 jax.ShapeDtypeStruct(chist.shape, F32), jax.ShapeDtypeStruct(phist.shape, F32)]
    out_specs = ([slab_spec(steps, D_MODEL)] * 3 + [slab_spec(steps, 128)] * 2 + [slab_spec(steps, D_MODEL)] * 4
                 + [slab_spec(CONV_W - 1, CONV_CH), slab_spec(POOL_HIST, D_MODEL)])
    return pl.pallas_call(
        functools.partial(_mixin_sample_body, steps=steps, bb=bb, pos0=pos0),
        grid=(nb // bb,), in_specs=in_specs, out_specs=out_specs, out_shape=out_shape,
        compiler_params=pltpu.CompilerParams(dimension_semantics=("arbitrary",), vmem_limit_bytes=VMEM_LIMIT),
        name="mixin_sample",
    )(x, nm, w_big, cw, alog, dtb, pw, ps, chist, phist)


def _delta_body(q_ref, k_ref, v_ref, g_ref, b_ref, o_ref, sout_ref, s_ref, *, tile):
    t = pl.program_id(1)

    @pl.when(t == 0)
    def _():
        s_ref[...] = jnp.zeros(s_ref.shape, F32)

    r = lax.broadcasted_iota(jnp.int32, (BLOCK, BLOCK), 0)
    c = lax.broadcasted_iota(jnp.int32, (BLOCK, BLOCK), 1)
    same = (r // CHUNK) == (c // CHUNK)
    low = same & (r >= c)
    strict = same & (r > c)
    eye = jnp.where(r == c, 1.0, 0.0).astype(F32)
    l_blk = jnp.where(low, 1.0, 0.0).astype(BF16)
    ones0 = jnp.where(c < CHUNK, 1.0, 0.0).astype(BF16)
    ones1 = jnp.where(c >= CHUNK, 1.0, 0.0).astype(BF16)
    first = lax.broadcasted_iota(jnp.int32, (BLOCK, 1), 0) < CHUNK
    zeros_half = jnp.zeros((CHUNK, HEAD_DIM), F32)

    def block(i, carry):
        r0 = pl.multiple_of(i * BLOCK, BLOCK)
        gp = g_ref[pl.ds(r0, BLOCK), :]
        bt = b_ref[pl.ds(r0, BLOCK), :]
        g_cum = _mm_exact_rhs(l_blk, gp)
        tot0 = _mm_exact_rhs(ones0, gp)
        tot1 = _mm_exact_rhs(ones1, gp)
        g_t = g_cum.T
        e_g = jnp.exp(g_cum)
        tot_sel = jnp.where(first, tot0, tot1)
        e_tail = jnp.exp(tot_sel - g_cum)
        e_tot0 = jnp.exp(tot0)
        e_tot1 = jnp.exp(tot1)
        heads = range(N_HEADS)
        ls = [slice(hh * HEAD_DIM, (hh + 1) * HEAD_DIM) for hh in heads]
        col = lambda a, hh: a[:, hh:hh + 1]
        rows = pl.ds(r0, BLOCK)
        cat = jnp.concatenate
        kh = [k_ref[rows, ls[hh]] for hh in heads]
        qh = [q_ref[rows, ls[hh]] for hh in heads]
        decay = [jnp.exp(jnp.where(low, col(g_cum, hh) - g_t[hh:hh + 1, :], -jnp.inf)) for hh in heads]
        kq = [_mm(cat([kh[hh], qh[hh]], axis=0), kh[hh], nt=True) for hh in heads]
        p = [-jnp.where(strict, kq[hh][:BLOCK] * decay[hh] * col(bt, hh), 0.0) for hh in heads]
        qk = [kq[hh][BLOCK:] * decay[hh] for hh in heads]
        tinv = [eye + p[hh] for hh in heads]
        p = [_mm(p[hh], p[hh]) for hh in heads]
        for _ in range(4):
            pp = [_mm(p[hh], cat([p[hh], tinv[hh]], axis=1)) for hh in heads]
            p = [pp[hh][:, :BLOCK] for hh in heads]
            tinv = [tinv[hh] + pp[hh][:, BLOCK:] for hh in heads]
        tinv = [tinv[hh] + _mm(p[hh], tinv[hh]) for hh in heads]
        sol = [_mm(tinv[hh], cat([v_ref[rows, ls[hh]] * col(bt, hh),
                                  kh[hh] * (col(bt, hh) * col(e_g, hh))], axis=1)) for hh in heads]
        wv = [sol[hh][:, :HEAD_DIM] for hh in heads]
        wk = [sol[hh][:, HEAD_DIM:] for hh in heads]
        qd = [qh[hh] * col(e_g, hh) for hh in heads]
        kt_t = [(kh[hh] * col(e_tail, hh)).T for hh in heads]
        s_cur = [s_ref[hh] for hh in heads]
        o_parts = []
        for half, e_tot in ((0, e_tot0), (1, e_tot1)):
            hs = slice(half * CHUNK, (half + 1) * CHUNK)
            res = [_mm(cat([wk[hh][hs], qd[hh][hs]], axis=0), s_cur[hh]) for hh in heads]
            u_new = [wv[hh][hs] - res[hh][:CHUNK] for hh in heads]
            u_pad = [cat([u_new[hh], zeros_half] if half == 0 else [zeros_half, u_new[hh]], axis=0)
                     for hh in heads]
            upd = [_mm(cat([qk[hh][hs], kt_t[hh]], axis=0), u_pad[hh]) for hh in heads]
            o_parts.append([res[hh][CHUNK:] + upd[hh][:CHUNK] for hh in heads])
            s_cur = [s_cur[hh] * col(e_tot, hh) + upd[hh][CHUNK:] for hh in heads]
        for hh in heads:
            s_ref[hh] = s_cur[hh]
            o_ref[rows, ls[hh]] = cat([o_parts[0][hh], o_parts[1][hh]], axis=0)
        return carry

    lax.fori_loop(0, tile // BLOCK, block, 0)
    sout_ref[...] = s_ref[...]


def _delta_prompt(q, k, v, g, beta, *, batch, seq, tile):
    nt = seq // tile
    row_spec = lambda c: pl.BlockSpec((tile, c), lambda b, t: (b * nt + t, 0))
    s_shape = (N_HEADS, HEAD_DIM, HEAD_DIM)
    return pl.pallas_call(
        functools.partial(_delta_body, tile=tile),
        grid=(batch, nt),
        in_specs=[row_spec(D_MODEL)] * 3 + [row_spec(128)] * 2,
        out_specs=[row_spec(D_MODEL), pl.BlockSpec((None,) + s_shape, lambda b, t: (b, 0, 0, 0))],
        out_shape=[jax.ShapeDtypeStruct(q.shape, F32), jax.ShapeDtypeStruct((batch,) + s_shape, F32)],
        scratch_shapes=[pltpu.VMEM(s_shape, F32)],
        compiler_params=pltpu.CompilerParams(dimension_semantics=("arbitrary", "arbitrary"),
                                             vmem_limit_bytes=VMEM_LIMIT),
        name="delta_prompt",
    )(q, k, v, g, beta)


def _head_indicators():
    d = lax.broadcasted_iota(jnp.int32, (D_MODEL, 128), 0) // HEAD_DIM
    hcol = lax.broadcasted_iota(jnp.int32, (D_MODEL, 128), 1)
    e_sum = jnp.where(d == hcol, 1.0, 0.0).astype(BF16)
    hrow = lax.broadcasted_iota(jnp.int32, (128, D_MODEL), 0)
    d2 = lax.broadcasted_iota(jnp.int32, (128, D_MODEL), 1) // HEAD_DIM
    e_bc = jnp.where(hrow == d2, 1.0, 0.0).astype(BF16)
    return e_sum, e_bc


def _sample_prep_body(q_ref, k_ref, v_ref, g_ref, b_ref, wkqd_ref, wv_ref, kt_ref, qkd_ref, glx_ref,
                      *, steps, nb):
    e_sum, e_bc = _head_indicators()
    expand = lambda x: _mm_exact_lhs(x, e_bc)
    hsum = lambda y: _mm_exact_lhs(y, e_sum)
    sl = lambda ref, i: ref[i * nb:(i + 1) * nb, :]
    q = [sl(q_ref, i) for i in range(steps)]
    k = [sl(k_ref, i) for i in range(steps)]
    v = [sl(v_ref, i) for i in range(steps)]
    beta = [sl(b_ref, i) for i in range(steps)]
    g_cum = []
    for i in range(steps):
        gi = sl(g_ref, i)
        g_cum.append(gi if i == 0 else g_cum[-1] + gi)
    wv, wk = [], []
    for i in range(steps):
        acc_v = v[i] * expand(beta[i])
        acc_k = k[i] * expand(beta[i] * jnp.exp(g_cum[i]))
        for j in range(i):
            a_ij = expand(hsum(k[i] * k[j]) * jnp.exp(g_cum[i] - g_cum[j]) * beta[i])
            acc_v = acc_v - a_ij * wv[j]
            acc_k = acc_k - a_ij * wk[j]
        wv.append(acc_v)
        wk.append(acc_k)
    zeros = jnp.zeros((nb, D_MODEL), F32)
    for i in range(steps):
        wkqd_ref[i * nb:(i + 1) * nb, :] = wk[i]
        wkqd_ref[(steps + i) * nb:(steps + i + 1) * nb, :] = q[i] * expand(jnp.exp(g_cum[i]))
        wv_ref[i * nb:(i + 1) * nb, :] = wv[i]
        wv_ref[(steps + i) * nb:(steps + i + 1) * nb, :] = zeros
        kt_ref[i * nb:(i + 1) * nb, :] = k[i] * expand(jnp.exp(g_cum[steps - 1] - g_cum[i]))
        kt_ref[(steps + i) * nb:(steps + i + 1) * nb, :] = zeros
        for j in range(steps):
            idx = i * steps + j
            if j <= i:
                qkd_ref[idx * nb:(idx + 1) * nb, :] = hsum(q[i] * k[j]) * jnp.exp(g_cum[i] - g_cum[j])
            else:
                qkd_ref[idx * nb:(idx + 1) * nb, :] = jnp.zeros((nb, 128), F32)
    glx_ref[...] = expand(jnp.exp(g_cum[steps - 1]))


def _sample_prep(q, k, v, g, beta, *, steps, nb):
    m = steps * nb
    out_shape = [jax.ShapeDtypeStruct((2 * m, D_MODEL), F32)] * 3 + [
        jax.ShapeDtypeStruct((steps * steps * nb, 128), F32), jax.ShapeDtypeStruct((nb, D_MODEL), F32)]
    return pl.pallas_call(
        functools.partial(_sample_prep_body, steps=steps, nb=nb),
        grid=(1,),
        in_specs=[_full((m, D_MODEL))] * 3 + [_full((m, 128))] * 2,
        out_specs=[_full(s.shape) for s in out_shape],
        out_shape=out_shape,
        compiler_params=pltpu.CompilerParams(dimension_semantics=("arbitrary",), vmem_limit_bytes=VMEM_LIMIT),
        name="sample_prep",
    )(q, k, v, g, beta)


def _sample_state_body(s_ref, wkqd_ref, wv_ref, kt_ref, glx_ref, r_ref, snew_ref, *, bb):
    zeros_pad = jnp.zeros((HEAD_DIM - 8, HEAD_DIM), F32)

    def per_seq(b, carry):
        for hh in range(N_HEADS):
            ls = slice(hh * HEAD_DIM, (hh + 1) * HEAD_DIM)
            s0 = s_ref[b, hh]
            lhs = jnp.concatenate([wkqd_ref[b, :, ls], zeros_pad], axis=0)
            res = _mm(lhs, s0)[:8]
            r_ref[b, :, ls] = res
            u = wv_ref[b, :, ls] - res
            kt_t = jnp.concatenate([kt_ref[b, :, ls], zeros_pad], axis=0).T
            upd = _mm(kt_t, jnp.concatenate([u, zeros_pad], axis=0))
            snew_ref[b, hh] = s0 * glx_ref[b, :, ls] + upd
        return carry

    lax.fori_loop(0, bb, per_seq, 0)


def _sample_state(s0, wkqd, wv, kt, glx, *, bb):
    nb = s0.shape[0]
    s_spec = pl.BlockSpec((bb, N_HEADS, HEAD_DIM, HEAD_DIM), lambda i: (i, 0, 0, 0))
    slot_spec = pl.BlockSpec((bb, 8, D_MODEL), lambda i: (i, 0, 0))
    return pl.pallas_call(
        functools.partial(_sample_state_body, bb=bb),
        grid=(nb // bb,),
        in_specs=[s_spec, slot_spec, slot_spec, slot_spec, pl.BlockSpec((bb, 1, D_MODEL), lambda i: (i, 0, 0))],
        out_specs=[slot_spec, s_spec],
        out_shape=[jax.ShapeDtypeStruct((nb, 8, D_MODEL), F32), jax.ShapeDtypeStruct(s0.shape, F32)],
        compiler_params=pltpu.CompilerParams(dimension_semantics=("arbitrary",), vmem_limit_bytes=VMEM_LIMIT),
        name="sample_state",
    )(s0, wkqd, wv, kt, glx.reshape(nb, 1, D_MODEL))


def _sample_out_body(r_ref, wv_ref, qkd_ref, o_ref, *, steps, nb):
    _, e_bc = _head_indicators()
    sl = lambda ref, i: ref[i * nb:(i + 1) * nb, :]
    u = [sl(wv_ref, j) - sl(r_ref, j) for j in range(steps)]
    for i in range(steps):
        acc = sl(r_ref, steps + i)
        for j in range(i + 1):
            acc = acc + _mm_exact_lhs(sl(qkd_ref, i * steps + j), e_bc) * u[j]
        o_ref[i * nb:(i + 1) * nb, :] = acc


def _sample_out(r, wv, qkd, *, steps, nb):
    m = steps * nb
    return pl.pallas_call(
        functools.partial(_sample_out_body, steps=steps, nb=nb),
        grid=(1,),
        in_specs=[_full(r.shape), _full(wv.shape), _full(qkd.shape)],
        out_specs=_full((m, D_MODEL)),
        out_shape=jax.ShapeDtypeStruct((m, D_MODEL), F32),
        compiler_params=pltpu.CompilerParams(dimension_semantics=("arbitrary",), vmem_limit_bytes=VMEM_LIMIT),
        name="sample_out",
    )(r, wv, qkd)


def _merge_body(o_ref, zs_ref, ypg_ref, sga_ref, sgb_ref, x_ref, p_ref, gn_ref, wpa_ref, wpb_ref, wout_ref,
                npl_ref, wpg_ref, wpp_ref, fn_ref, y_ref, *, final):
    gn = gn_ref[...]
    for hh in range(N_HEADS):
        ls = slice(hh * HEAD_DIM, (hh + 1) * HEAD_DIM)
        oh = o_ref[:, ls]
        on = oh * lax.rsqrt(jnp.mean(oh * oh, axis=-1, keepdims=True) + EPS) * gn
        part = _dot((on * zs_ref[:, ls]).astype(BF16), wpa_ref[ls, :])
        y_a = part if hh == 0 else y_a + part
    y_b = _mm(ypg_ref[...], wpb_ref[...])
    m = sga_ref[...] * y_a + sgb_ref[...] * y_b
    x1 = x_ref[...] + _mm(m, wout_ref[...])
    gate = jax.nn.sigmoid(_mm(_rmsnorm(x1, npl_ref[...]), wpg_ref[...]))
    x2 = x1 + gate * _mm(p_ref[...], wpp_ref[...])
    if final:
        x2 = _rmsnorm(x2, fn_ref[...])
    y_ref[...] = x2


def _merge(o, zs, ypg, sga, sgb, x, p, gn, wpa, wpb, wout, npl, wpg, wpp, fn, *, tile, final):
    m = x.shape[0]
    ple = p.shape[1]
    row_spec = lambda c: pl.BlockSpec((tile, c), lambda i: (i, 0))
    sq = _full((D_MODEL, D_MODEL))
    return pl.pallas_call(
        functools.partial(_merge_body, final=final),
        grid=(m // tile,),
        in_specs=[row_spec(D_MODEL)] * 6 + [row_spec(ple), _full((1, HEAD_DIM)), sq, sq, sq, _full((1, D_MODEL)),
                                            sq, _full((ple, D_MODEL)), _full((1, D_MODEL))],
        out_specs=row_spec(D_MODEL),
        out_shape=jax.ShapeDtypeStruct((m, D_MODEL), F32),
        compiler_params=pltpu.CompilerParams(dimension_semantics=("arbitrary",), vmem_limit_bytes=VMEM_LIMIT),
        name="merge",
    )(o, zs, ypg, sga, sgb, x, p, gn, wpa, wpb, wout, npl, wpg, wpp, fn)


def _pad_cols(w, n):
    return jnp.pad(w, ((0, 0), (0, n - w.shape[1])))


def _prep_layer(i, norm_mix, w_in, conv_w, a_log, dt_bias, gdn_norm, w_proj_a, pool_w, pool_scale, w_proj_b,
                w_out, norm_ple, w_ple_gate, w_ple_proj):
    w = w_in[i]
    qkvz = w[:, 0:4096]
    a_w = _pad_cols(w[:, 4096:4104], 128)
    b_w = _pad_cols(w[:, 4104:4112], 128)
    rest = w[:, 4112:8208]
    w_big = jnp.concatenate([qkvz, rest, a_w, b_w], axis=1).astype(BF16)
    row = lambda a: a.reshape(1, -1)
    return dict(
        nm=row(norm_mix[i]), w_big=w_big, cw=conv_w[i],
        alog=_pad_cols(row(a_log[i]), 128), dtb=_pad_cols(row(dt_bias[i]), 128),
        pw=pool_w[i].astype(BF16), ps=row(pool_scale[i]),
        gn=row(gdn_norm[i]), wpa=w_proj_a[i].astype(BF16), wpb=w_proj_b[i].astype(BF16),
        wout=w_out[i].astype(BF16), npl=row(norm_ple[i]), wpg=w_ple_gate[i].astype(BF16),
        wpp=w_ple_proj[i].astype(BF16))


def kernel(x_prompt, x_sample, p_prompt, p_sample, state_conv, state_delta, state_pool, norm_mix, w_in, conv_w,
           a_log, dt_bias, gdn_norm, w_proj_a, pool_w, pool_scale, w_proj_b, w_out, norm_ple, w_ple_gate,
           w_ple_proj, final_norm):
    depth = w_in.shape[0]
    batch, seq, _ = x_prompt.shape
    nb, steps, _ = x_sample.shape
    fn = final_norm.reshape(1, -1)
    layers = [_prep_layer(i, norm_mix, w_in, conv_w, a_log, dt_bias, gdn_norm, w_proj_a, pool_w, pool_scale,
                          w_proj_b, w_out, norm_ple, w_ple_gate, w_ple_proj) for i in range(depth)]

    xp = x_prompt.reshape(batch * seq, D_MODEL)
    conv_p, delta_p, pool_p = [], [], []
    for i, L in enumerate(layers):
        q, k, v, g, beta, zs, ypg, sga, sgb, cnew, pnew = _mixin_prompt(
            xp, L["nm"], L["w_big"], L["cw"], L["alog"], L["dtb"], L["pw"], L["ps"],
            batch=batch, seq=seq, tile=256)
        o, s_new = _delta_prompt(q, k, v, g, beta, batch=batch, seq=seq, tile=512)
        xp = _merge(o, zs, ypg, sga, sgb, xp, p_prompt[i].reshape(batch * seq, -1), L["gn"], L["wpa"], L["wpb"],
                    L["wout"], L["npl"], L["wpg"], L["wpp"], fn, tile=512, final=(i == depth - 1))
        conv_p.append(cnew)
        delta_p.append(s_new)
        pool_p.append(pnew)
    y_prompt = xp.reshape(batch, seq, D_MODEL)

    swap = lambda a: jnp.swapaxes(a, 0, 1)
    flat = lambda a: a.reshape(-1, a.shape[-1])
    bmajor = lambda a: swap(a.reshape(-1, nb, a.shape[-1]))
    xs = swap(x_sample)
    conv_s, delta_s, pool_s = [], [], []
    for i, L in enumerate(layers):
        q, k, v, g, beta, zs, ypg, sga, sgb, cnew, pnew = _mixin_sample(
            xs, L["nm"], L["w_big"], L["cw"], L["alog"], L["dtb"], L["pw"], L["ps"],
            swap(state_conv[i]), swap(state_pool[i]), bb=32, pos0=PAST_LEN)
        wkqd, wv, kt, qkd, glx = _sample_prep(flat(q), flat(k), flat(v), flat(g), flat(beta), steps=steps, nb=nb)
        r, s_new = _sample_state(state_delta[i], bmajor(wkqd), bmajor(wv), bmajor(kt), glx, bb=8)
        o = _sample_out(flat(swap(r)), wv, qkd, steps=steps, nb=nb)
        xs = _merge(o, flat(zs), flat(ypg), flat(sga), flat(sgb), flat(xs), flat(swap(p_sample[i])), L["gn"],
                    L["wpa"], L["wpb"], L["wout"], L["npl"], L["wpg"], L["wpp"], fn, tile=steps * nb,
                    final=(i == depth - 1)).reshape(steps, nb, D_MODEL)
        conv_s.append(swap(cnew))
        delta_s.append(s_new)
        pool_s.append(swap(pnew))
    y_sample = swap(xs)

    return (y_prompt, y_sample, jnp.stack(conv_p), jnp.stack(delta_p), jnp.stack(pool_p),
            jnp.stack(conv_s), jnp.stack(delta_s), jnp.stack(pool_s))
```

```python
import functools

import jax
import jax.numpy as jnp
from jax import lax
from jax.experimental import pallas as pl
from jax.experimental.pallas import tpu as pltpu

F32 = jnp.float32
BF16 = jnp.bfloat16

D_MODEL = 1024
N_HEADS = 8
HEAD_DIM = 128
CONV_W = 4
CONV_CH = 3 * D_MODEL
POOL_WINDOWS = (2, 4, 8, 16)
POOL_HIST = 15
POOL_GC = 256
EPS = 1e-6
PAST_LEN = 16384
CHUNK = 64
BLOCK = 2 * CHUNK
VMEM_LIMIT = 56 * 1024 * 1024

_C_QKV, _C_Z, _C_U, _C_GP, _C_GA, _C_GB, _C_A, _C_B = 0, 3072, 4096, 5120, 6144, 7168, 8192, 8320
_W_COLS = 8448


def _dot(a, b, nt=False):
    dims = (((1,), (1,)), ((), ())) if nt else (((1,), (0,)), ((), ()))
    return lax.dot_general(a, b, dims, preferred_element_type=F32)


def _mm(a, b, nt=False):
    return _dot(a.astype(BF16), b.astype(BF16), nt)


def _split3(x):
    x0 = x.astype(BF16)
    r = x - x0.astype(F32)
    x1 = r.astype(BF16)
    x2 = (r - x1.astype(F32)).astype(BF16)
    return x0, x1, x2


def _mm_exact_rhs(a_bf, b):
    b0, b1, b2 = _split3(b)
    return _dot(a_bf, b0) + _dot(a_bf, b1) + _dot(a_bf, b2)


def _mm_exact_lhs(a, b_bf):
    a0, a1, a2 = _split3(a)
    return _dot(a0, b_bf) + _dot(a1, b_bf) + _dot(a2, b_bf)


def _silu(x):
    return x * jax.nn.sigmoid(x)


def _rmsnorm(x, w):
    return x * lax.rsqrt(jnp.mean(x * x, axis=-1, keepdims=True) + EPS) * w


def _full(shape):
    n = len(shape)
    return pl.BlockSpec(shape, lambda *_: (0,) * n, pipeline_mode=pl.Buffered(1))


def _layer(li, shape):
    n = len(shape)
    return pl.BlockSpec((None,) + tuple(shape), lambda *_: (li,) + (0,) * n, pipeline_mode=pl.Buffered(1))


def _wprep_body(w_ref, o_ref):
    o_ref[:, 0:4096] = w_ref[:, 0:4096].astype(BF16)
    o_ref[:, 4096:8192] = w_ref[:, 4112:8208].astype(BF16)
    ab = w_ref[:, 4096:4224]
    lane = lax.broadcasted_iota(jnp.int32, ab.shape, 1)
    o_ref[:, _C_A:_C_A + 128] = jnp.where(lane < N_HEADS, ab, 0.0).astype(BF16)
    o_ref[:, _C_B:_C_B + 128] = jnp.where(lane < N_HEADS, pltpu.roll(ab, 128 - N_HEADS, 1), 0.0).astype(BF16)


def _wprep(w_in, *, tile):
    depth, k, n = w_in.shape
    return pl.pallas_call(
        _wprep_body, grid=(depth, k // tile),
        in_specs=[pl.BlockSpec((None, tile, n), lambda l, r: (l, r, 0))],
        out_specs=pl.BlockSpec((None, tile, _W_COLS), lambda l, r: (l, r, 0)),
        out_shape=jax.ShapeDtypeStruct((depth, k, _W_COLS), BF16),
        compiler_params=pltpu.CompilerParams(dimension_semantics=("arbitrary", "arbitrary"),
                                             vmem_limit_bytes=VMEM_LIMIT),
        name="wprep",
    )(w_in)


def _mixin_core(h, w_ref, cw_ref, alog_ref, dtb_ref, pw_ref, ps_ref, outs, pos, conv_tap, pool_sums, store):
    q_ref, k_ref, v_ref, g_ref, b_ref, zs_ref, ypg_ref, sga_ref, sgb_ref = outs
    xcs = []
    for seg, out_ref in enumerate((q_ref, k_ref, v_ref)):
        c0 = seg * D_MODEL
        xc = _dot(h, w_ref[:, _C_QKV + c0:_C_QKV + c0 + D_MODEL])
        xcs.append(xc)
        acc = xc * cw_ref[CONV_W - 1:CONV_W, c0:c0 + D_MODEL]
        for j in range(CONV_W - 1):
            acc = acc + conv_tap(j, c0, xc) * cw_ref[j:j + 1, c0:c0 + D_MODEL]
        y = _silu(acc)
        if seg < 2:
            for hh in range(N_HEADS):
                ls = slice(hh * HEAD_DIM, (hh + 1) * HEAD_DIM)
                yh = y[:, ls]
                inv = lax.rsqrt(jnp.sum(yh * yh, axis=-1, keepdims=True) + EPS)
                if seg == 0:
                    inv = inv * (HEAD_DIM ** -0.5)
                store(out_ref, yh * inv, ls)
        else:
            store(out_ref, y, None)

    a_r = _dot(h, w_ref[:, _C_A:_C_A + 128])
    b_r = _dot(h, w_ref[:, _C_B:_C_B + 128])
    xs = a_r + dtb_ref[...]
    softplus = jnp.maximum(xs, 0.0) + jnp.log1p(jnp.exp(-jnp.abs(xs)))
    store(g_ref, -jnp.exp(alog_ref[...]) * softplus, None)
    store(b_ref, jax.nn.sigmoid(b_r), None)

    store(zs_ref, _silu(_dot(h, w_ref[:, _C_Z:_C_Z + D_MODEL])), None)
    store(sga_ref, jax.nn.sigmoid(_dot(h, w_ref[:, _C_GA:_C_GA + D_MODEL])), None)
    store(sgb_ref, jax.nn.sigmoid(_dot(h, w_ref[:, _C_GB:_C_GB + D_MODEL])), None)

    u = _dot(h, w_ref[:, _C_U:_C_U + D_MODEL])
    sgp = _silu(_dot(h, w_ref[:, _C_GP:_C_GP + D_MODEL]))
    sums = pool_sums(u)
    for gi, win in enumerate(POOL_WINDOWS):
        l0 = gi * POOL_GC
        ls = slice(l0, l0 + POOL_GC)
        count = jnp.minimum(pos + 1, win).astype(F32)
        y = sums[gi] / count - u[:, ls]
        yp = _mm(y, pw_ref[gi]) * ps_ref[:, ls]
        store(ypg_ref, yp * sgp[:, ls], ls)
    return xcs, u


def _mixin_weight_specs(li):
    return [_layer(li, (1, D_MODEL)), _layer(li, (D_MODEL, _W_COLS)), _layer(li, (CONV_W, CONV_CH)),
            _layer(li, (1, 128)), _layer(li, (1, 128)), _layer(li, (4, POOL_GC, POOL_GC)), _layer(li, (1, D_MODEL))]


def _mixin_prompt_body(x_ref, nm_ref, w_ref, cw_ref, alog_ref, dtb_ref, pw_ref, ps_ref,
                       q_ref, k_ref, v_ref, g_ref, b_ref, zs_ref, ypg_ref, sga_ref, sgb_ref, cnew_ref, pnew_ref,
                       cext, pext, *, rows):
    hc, hp = 8, 16
    t = pl.program_id(1)

    @pl.when(t == 0)
    def _():
        cext[0:hc, :] = jnp.zeros((hc, CONV_CH), F32)
        pext[0:hp, :] = jnp.zeros((hp, D_MODEL), F32)

    h = _rmsnorm(x_ref[...], nm_ref[...]).astype(BF16)

    def conv_tap(j, c0, xc):
        if j == 0:
            cext[hc:hc + rows, c0:c0 + D_MODEL] = xc
        off = hc - (CONV_W - 1 - j)
        return cext[off:off + rows, c0:c0 + D_MODEL]

    def pool_sums(u):
        pext[hp:hp + rows, :] = u
        lvl = pext[...]
        out = []
        for gi, win in enumerate(POOL_WINDOWS):
            lvl = lvl + pltpu.roll(lvl, win // 2, 0)
            out.append(lvl[hp:, :POOL_GC])
            lvl = lvl[:, POOL_GC:]
        return out

    def store(ref, val, ls):
        if ls is None:
            ref[...] = val
        else:
            ref[:, ls] = val

    pos = t * rows + lax.broadcasted_iota(jnp.int32, (rows, 1), 0)
    outs = (q_ref, k_ref, v_ref, g_ref, b_ref, zs_ref, ypg_ref, sga_ref, sgb_ref)
    _mixin_core(h, w_ref, cw_ref, alog_ref, dtb_ref, pw_ref, ps_ref, outs, pos, conv_tap, pool_sums, store)
    cnew_ref[...] = cext[hc + rows - (CONV_W - 1):hc + rows, :]
    pnew_ref[...] = pext[hp + rows - POOL_HIST:hp + rows, :]
    cext[0:hc, :] = cext[rows:rows + hc, :]
    pext[0:hp, :] = pext[rows:rows + hp, :]


def _mixin_prompt(li, x, nm, w_big, cw, alog, dtb, pw, ps, *, batch, seq, tile):
    m = x.shape[0]
    nt = seq // tile
    row_spec = lambda c: pl.BlockSpec((tile, c), lambda b, t: (b * nt + t, 0))
    in_specs = [row_spec(D_MODEL)] + _mixin_weight_specs(li)
    cnew_spec = pl.BlockSpec((None, CONV_W - 1, CONV_CH), lambda b, t: (b, 0, 0))
    pnew_spec = pl.BlockSpec((None, POOL_HIST, D_MODEL), lambda b, t: (b, 0, 0))
    big = jax.ShapeDtypeStruct((m, D_MODEL), F32)
    small = jax.ShapeDtypeStruct((m, 128), F32)
    out_shape = [big, big, big, small, small, big, big, big, big,
                 jax.ShapeDtypeStruct((batch, CONV_W - 1, CONV_CH), F32),
                 jax.ShapeDtypeStruct((batch, POOL_HIST, D_MODEL), F32)]
    out_specs = [row_spec(D_MODEL)] * 3 + [row_spec(128)] * 2 + [row_spec(D_MODEL)] * 4 + [cnew_spec, pnew_spec]
    return pl.pallas_call(
        functools.partial(_mixin_prompt_body, rows=tile),
        grid=(batch, nt), in_specs=in_specs, out_specs=out_specs, out_shape=out_shape,
        scratch_shapes=[pltpu.VMEM((8 + tile, CONV_CH), F32), pltpu.VMEM((16 + tile, D_MODEL), F32)],
        compiler_params=pltpu.CompilerParams(dimension_semantics=("arbitrary", "arbitrary"),
                                             vmem_limit_bytes=VMEM_LIMIT),
        name="mixin_prompt",
    )(x, nm, w_big, cw, alog, dtb, pw, ps)


def _mixin_sample_body(x_ref, nm_ref, w_ref, cw_ref, alog_ref, dtb_ref, pw_ref, ps_ref, chist_ref, phist_ref,
                       q_ref, k_ref, v_ref, g_ref, b_ref, zs_ref, ypg_ref, sga_ref, sgb_ref, cnew_ref, pnew_ref,
                       *, steps, bb, pos0):
    rows = steps * bb
    h = _rmsnorm(x_ref[...].reshape(rows, D_MODEL), nm_ref[...]).astype(BF16)

    def delayed(new, hist_ref, nhist, d, ls_new, ls_hist):
        parts = []
        for t in range(steps):
            src = t - d
            parts.append(new[src * bb:(src + 1) * bb, ls_new] if src >= 0 else hist_ref[nhist + src, :, ls_hist])
        return jnp.concatenate(parts, axis=0)

    def conv_tap(j, c0, xc):
        return delayed(xc, chist_ref, CONV_W - 1, CONV_W - 1 - j, slice(None), slice(c0, c0 + D_MODEL))

    def pool_sums(u):
        out = []
        for gi, win in enumerate(POOL_WINDOWS):
            ls = slice(gi * POOL_GC, (gi + 1) * POOL_GC)
            acc = u[:, ls]
            for s in range(1, win):
                acc = acc + delayed(u, phist_ref, POOL_HIST, s, ls, ls)
            out.append(acc)
        return out

    def store(ref, val, ls):
        val = val.reshape(steps, bb, val.shape[-1])
        if ls is None:
            ref[...] = val
        else:
            ref[:, :, ls] = val

    pos = pos0 + lax.broadcasted_iota(jnp.int32, (rows, 1), 0) // bb
    outs = (q_ref, k_ref, v_ref, g_ref, b_ref, zs_ref, ypg_ref, sga_ref, sgb_ref)
    xcs, u = _mixin_core(h, w_ref, cw_ref, alog_ref, dtb_ref, pw_ref, ps_ref, outs, pos, conv_tap, pool_sums, store)
    for i in range(CONV_W - 1):
        src = steps + i - (CONV_W - 1)
        for seg in range(3):
            ls = slice(seg * D_MODEL, (seg + 1) * D_MODEL)
            cnew_ref[i, :, ls] = (xcs[seg][src * bb:(src + 1) * bb, :] if src >= 0
                                  else chist_ref[CONV_W - 1 + src, :, ls])
    for i in range(POOL_HIST):
        src = steps + i - POOL_HIST
        pnew_ref[i] = u[src * bb:(src + 1) * bb, :] if src >= 0 else phist_ref[POOL_HIST + src]


def _mixin_sample(li, x, nm, w_big, cw, alog, dtb, pw, ps, chist, phist, *, bb, pos0):
    steps, nb, _ = x.shape
    slab_spec = lambda n, c: pl.BlockSpec((n, bb, c), lambda i: (0, i, 0))
    in_specs = ([slab_spec(steps, D_MODEL)] + _mixin_weight_specs(li)
                + [slab_spec(CONV_W - 1, CONV_CH), slab_spec(POOL_HIST, D_MODEL)])
    big = jax.ShapeDtypeStruct((steps, nb, D_MODEL), F32)
    small = jax.ShapeDtypeStruct((steps, nb, 128), F32)
    out_shape = [big, big, big, small, small, big, big, big, big,
                 jax.ShapeDtypeStruct(chist.shape, F32), jax.ShapeDtypeStruct(phist.shape, F32)]
    out_specs = ([slab_spec(steps, D_MODEL)] * 3 + [slab_spec(steps, 128)] * 2 + [slab_spec(steps, D_MODEL)] * 4
                 + [slab_spec(CONV_W - 1, CONV_CH), slab_spec(POOL_HIST, D_MODEL)])
    return pl.pallas_call(
        functools.partial(_mixin_sample_body, steps=steps, bb=bb, pos0=pos0),
        grid=(nb // bb,), in_specs=in_specs, out_specs=out_specs, out_shape=out_shape,
        compiler_params=pltpu.CompilerParams(dimension_semantics=("arbitrary",), vmem_limit_bytes=VMEM_LIMIT),
        name="mixin_sample",
    )(x, nm, w_big, cw, alog, dtb, pw, ps, chist, phist)


def _delta_body(q_ref, k_ref, v_ref, g_ref, b_ref, o_ref, sout_ref, s_ref, *, tile):
    t = pl.program_id(1)

    @pl.when(t == 0)
    def _():
        s_ref[...] = jnp.zeros(s_ref.shape, F32)

    r = lax.broadcasted_iota(jnp.int32, (BLOCK, BLOCK), 0)
    c = lax.broadcasted_iota(jnp.int32, (BLOCK, BLOCK), 1)
    same = (r // CHUNK) == (c // CHUNK)
    low = same & (r >= c)
    strict = same & (r > c)
    eye = jnp.where(r == c, 1.0, 0.0).astype(F32)
    l_cat = jnp.concatenate([jnp.where(low, 1.0, 0.0), jnp.where(c < CHUNK, 1.0, 0.0),
                             jnp.where(c >= CHUNK, 1.0, 0.0)], axis=0).astype(BF16)
    first = lax.broadcasted_iota(jnp.int32, (BLOCK, 1), 0) < CHUNK
    zeros_half = jnp.zeros((CHUNK, HEAD_DIM), F32)

    def block(i, carry):
        r0 = pl.multiple_of(i * BLOCK, BLOCK)
        gp = g_ref[pl.ds(r0, BLOCK), :]
        bt = b_ref[pl.ds(r0, BLOCK), :]
        g_all = _mm_exact_rhs(l_cat, gp)
        g_cum = g_all[:BLOCK]
        tot0 = g_all[BLOCK:2 * BLOCK]
        tot1 = g_all[2 * BLOCK:]
        g_t = g_cum.T
        e_g = jnp.exp(g_cum)
        tot_sel = jnp.where(first, tot0, tot1)
        e_tail = jnp.exp(tot_sel - g_cum)
        e_tot0 = jnp.exp(tot0)
        e_tot1 = jnp.exp(tot1)
        heads = range(N_HEADS)
        ls = [slice(hh * HEAD_DIM, (hh + 1) * HEAD_DIM) for hh in heads]
        col = lambda a, hh: a[:, hh:hh + 1]
        rows = pl.ds(r0, BLOCK)
        cat = jnp.concatenate
        kh = [k_ref[rows, ls[hh]] for hh in heads]
        qh = [q_ref[rows, ls[hh]] for hh in heads]
        decay = [jnp.exp(jnp.where(low, col(g_cum, hh) - g_t[hh:hh + 1, :], -jnp.inf)) for hh in heads]
        kq = [_mm(cat([kh[hh], qh[hh]], axis=0), kh[hh], nt=True) for hh in heads]
        p = [-jnp.where(strict, kq[hh][:BLOCK] * decay[hh] * col(bt, hh), 0.0) for hh in heads]
        qk = [kq[hh][BLOCK:] * decay[hh] for hh in heads]
        tinv = [eye + p[hh] for hh in heads]
        p = [_mm(p[hh], p[hh]) for hh in heads]
        for _ in range(4):
            pp = [_mm(p[hh], cat([p[hh], tinv[hh]], axis=1)) for hh in heads]
            p = [pp[hh][:, :BLOCK] for hh in heads]
            tinv = [tinv[hh] + pp[hh][:, BLOCK:] for hh in heads]
        tinv = [tinv[hh] + _mm(p[hh], tinv[hh]) for hh in heads]
        sol = [_mm(tinv[hh], cat([v_ref[rows, ls[hh]] * col(bt, hh),
                                  kh[hh] * (col(bt, hh) * col(e_g, hh))], axis=1)) for hh in heads]
        wv = [sol[hh][:, :HEAD_DIM] for hh in heads]
        wk = [sol[hh][:, HEAD_DIM:] for hh in heads]
        qd = [qh[hh] * col(e_g, hh) for hh in heads]
        kt_t = [(kh[hh] * col(e_tail, hh)).T for hh in heads]
        s_cur = [s_ref[hh] for hh in heads]
        o_parts = []
        for half, e_tot in ((0, e_tot0), (1, e_tot1)):
            hs = slice(half * CHUNK, (half + 1) * CHUNK)
            res = [_mm(cat([wk[hh][hs], qd[hh][hs]], axis=0), s_cur[hh]) for hh in heads]
            u_new = [wv[hh][hs] - res[hh][:CHUNK] for hh in heads]
            u_pad = [cat([u_new[hh], zeros_half] if half == 0 else [zeros_half, u_new[hh]], axis=0)
                     for hh in heads]
            upd = [_mm(cat([qk[hh][hs], kt_t[hh]], axis=0), u_pad[hh]) for hh in heads]
            o_parts.append([res[hh][CHUNK:] + upd[hh][:CHUNK] for hh in heads])
            s_cur = [s_cur[hh] * col(e_tot, hh) + upd[hh][CHUNK:] for hh in heads]
        for hh in heads:
            s_ref[hh] = s_cur[hh]
            o_ref[rows, ls[hh]] = cat([o_parts[0][hh], o_parts[1][hh]], axis=0)
        return carry

    lax.fori_loop(0, tile // BLOCK, block, 0)
    sout_ref[...] = s_ref[...]


def _delta_prompt(q, k, v, g, beta, *, batch, seq, tile):
    nt = seq // tile
    row_spec = lambda c: pl.BlockSpec((tile, c), lambda b, t: (b * nt + t, 0))
    s_shape = (N_HEADS, HEAD_DIM, HEAD_DIM)
    return pl.pallas_call(
        functools.partial(_delta_body, tile=tile),
        grid=(batch, nt),
        in_specs=[row_spec(D_MODEL)] * 3 + [row_spec(128)] * 2,
        out_specs=[row_spec(D_MODEL), pl.BlockSpec((None,) + s_shape, lambda b, t: (b, 0, 0, 0))],
        out_shape=[jax.ShapeDtypeStruct(q.shape, F32), jax.ShapeDtypeStruct((batch,) + s_shape, F32)],
        scratch_shapes=[pltpu.VMEM(s_shape, F32)],
        compiler_params=pltpu.CompilerParams(dimension_semantics=("arbitrary", "arbitrary"),
                                             vmem_limit_bytes=VMEM_LIMIT),
        name="delta_prompt",
    )(q, k, v, g, beta)


def _head_indicators():
    d = lax.broadcasted_iota(jnp.int32, (D_MODEL, 128), 0) // HEAD_DIM
    hcol = lax.broadcasted_iota(jnp.int32, (D_MODEL, 128), 1)
    e_sum = jnp.where(d == hcol, 1.0, 0.0).astype(BF16)
    hrow = lax.broadcasted_iota(jnp.int32, (128, D_MODEL), 0)
    d2 = lax.broadcasted_iota(jnp.int32, (128, D_MODEL), 1) // HEAD_DIM
    e_bc = jnp.where(hrow == d2, 1.0, 0.0).astype(BF16)
    return e_sum, e_bc


def _sample_prep_body(q_ref, k_ref, v_ref, g_ref, b_ref, wkqd_ref, wv_ref, kt_ref, qkd_ref, glx_ref,
                      *, steps, nb):
    e_sum, e_bc = _head_indicators()
    expand = lambda x: _mm_exact_lhs(x, e_bc)
    hsum = lambda y: _mm_exact_lhs(y, e_sum)
    sl = lambda ref, i: ref[i * nb:(i + 1) * nb, :]
    q = [sl(q_ref, i) for i in range(steps)]
    k = [sl(k_ref, i) for i in range(steps)]
    v = [sl(v_ref, i) for i in range(steps)]
    beta = [sl(b_ref, i) for i in range(steps)]
    g_cum = []
    for i in range(steps):
        gi = sl(g_ref, i)
        g_cum.append(gi if i == 0 else g_cum[-1] + gi)
    wv, wk = [], []
    for i in range(steps):
        acc_v = v[i] * expand(beta[i])
        acc_k = k[i] * expand(beta[i] * jnp.exp(g_cum[i]))
        for j in range(i):
            a_ij = expand(hsum(k[i] * k[j]) * jnp.exp(g_cum[i] - g_cum[j]) * beta[i])
            acc_v = acc_v - a_ij * wv[j]
            acc_k = acc_k - a_ij * wk[j]
        wv.append(acc_v)
        wk.append(acc_k)
    zeros = jnp.zeros((nb, D_MODEL), F32)
    for i in range(steps):
        wkqd_ref[i * nb:(i + 1) * nb, :] = wk[i]
        wkqd_ref[(steps + i) * nb:(steps + i + 1) * nb, :] = q[i] * expand(jnp.exp(g_cum[i]))
        wv_ref[i * nb:(i + 1) * nb, :] = wv[i]
        wv_ref[(steps + i) * nb:(steps + i + 1) * nb, :] = zeros
        kt_ref[i * nb:(i + 1) * nb, :] = k[i] * expand(jnp.exp(g_cum[steps - 1] - g_cum[i]))
        kt_ref[(steps + i) * nb:(steps + i + 1) * nb, :] = zeros
        for j in range(steps):
            idx = i * steps + j
            if j <= i:
                qkd_ref[idx * nb:(idx + 1) * nb, :] = hsum(q[i] * k[j]) * jnp.exp(g_cum[i] - g_cum[j])
            else:
                qkd_ref[idx * nb:(idx + 1) * nb, :] = jnp.zeros((nb, 128), F32)
    glx_ref[...] = expand(jnp.exp(g_cum[steps - 1]))


def _sample_prep(q, k, v, g, beta, *, steps, nb):
    m = steps * nb
    out_shape = [jax.ShapeDtypeStruct((2 * m, D_MODEL), F32)] * 3 + [
        jax.ShapeDtypeStruct((steps * steps * nb, 128), F32), jax.ShapeDtypeStruct((nb, D_MODEL), F32)]
    return pl.pallas_call(
        functools.partial(_sample_prep_body, steps=steps, nb=nb),
        grid=(1,),
        in_specs=[_full((m, D_MODEL))] * 3 + [_full((m, 128))] * 2,
        out_specs=[_full(s.shape) for s in out_shape],
        out_shape=out_shape,
        compiler_params=pltpu.CompilerParams(dimension_semantics=("arbitrary",), vmem_limit_bytes=VMEM_LIMIT),
        name="sample_prep",
    )(q, k, v, g, beta)


def _sample_state_body(*refs, bb):
    s_ref, wkqd_ref, wv_ref, kt_ref, glx_ref = refs[:5]
    r_ref, snew_ref = refs[-2:]
    zeros_pad = jnp.zeros((HEAD_DIM - 8, HEAD_DIM), F32)

    def per_seq(b, carry):
        heads = range(N_HEADS)
        ls = [slice(hh * HEAD_DIM, (hh + 1) * HEAD_DIM) for hh in heads]
        pad = lambda a: jnp.concatenate([a, zeros_pad], axis=0)
        s0 = [s_ref[b, hh] for hh in heads]
        res = [_mm(pad(wkqd_ref[b, :, ls[hh]]), s0[hh])[:8] for hh in heads]
        kt_t = [pad(kt_ref[b, :, ls[hh]]).T for hh in heads]
        upd = [_mm(kt_t[hh], pad(wv_ref[b, :, ls[hh]] - res[hh])) for hh in heads]
        for hh in heads:
            r_ref[b, :, ls[hh]] = res[hh]
            snew_ref[b, hh] = s0[hh] * glx_ref[b, :, ls[hh]] + upd[hh]
        return carry

    lax.fori_loop(0, bb, per_seq, 0)


def _sample_state(li, state, wkqd, wv, kt, glx, prev, *, bb):
    nb = state.shape[1]
    s_spec = pl.BlockSpec((None, bb, N_HEADS, HEAD_DIM, HEAD_DIM), lambda i: (li, i, 0, 0, 0))
    slot_spec = pl.BlockSpec((bb, 8, D_MODEL), lambda i: (i, 0, 0))
    in_specs = [s_spec, slot_spec, slot_spec, slot_spec, pl.BlockSpec((bb, 1, D_MODEL), lambda i: (i, 0, 0))]
    args = [state, wkqd, wv, kt, glx.reshape(nb, 1, D_MODEL)]
    aliases = {}
    if prev is not None:
        in_specs.append(pl.BlockSpec(memory_space=pl.ANY))
        args.append(prev)
        aliases = {len(args) - 1: 1}
    return pl.pallas_call(
        functools.partial(_sample_state_body, bb=bb),
        grid=(nb // bb,),
        in_specs=in_specs,
        out_specs=[slot_spec, s_spec],
        out_shape=[jax.ShapeDtypeStruct((nb, 8, D_MODEL), F32), jax.ShapeDtypeStruct(state.shape, F32)],
        input_output_aliases=aliases,
        compiler_params=pltpu.CompilerParams(dimension_semantics=("arbitrary",), vmem_limit_bytes=VMEM_LIMIT),
        name="sample_state",
    )(*args)


def _sample_out_body(r_ref, wv_ref, qkd_ref, o_ref, *, steps, nb):
    _, e_bc = _head_indicators()
    sl = lambda ref, i: ref[i * nb:(i + 1) * nb, :]
    u = [sl(wv_ref, j) - sl(r_ref, j) for j in range(steps)]
    for i in range(steps):
        acc = sl(r_ref, steps + i)
        for j in range(i + 1):
            acc = acc + _mm_exact_lhs(sl(qkd_ref, i * steps + j), e_bc) * u[j]
        o_ref[i * nb:(i + 1) * nb, :] = acc


def _sample_out(r, wv, qkd, *, steps, nb):
    m = steps * nb
    return pl.pallas_call(
        functools.partial(_sample_out_body, steps=steps, nb=nb),
        grid=(1,),
        in_specs=[_full(r.shape), _full(wv.shape), _full(qkd.shape)],
        out_specs=_full((m, D_MODEL)),
        out_shape=jax.ShapeDtypeStruct((m, D_MODEL), F32),
        compiler_params=pltpu.CompilerParams(dimension_semantics=("arbitrary",), vmem_limit_bytes=VMEM_LIMIT),
        name="sample_out",
    )(r, wv, qkd)


def _merge_body(o_ref, zs_ref, ypg_ref, sga_ref, sgb_ref, x_ref, p_ref, gn_ref, wpa_ref, wpb_ref, wout_ref,
                npl_ref, wpg_ref, wpp_ref, fn_ref, y_ref, *, final, parts):
    step = o_ref.shape[0] // parts
    groups = [slice(i * step, (i + 1) * step) for i in range(parts)]
    gn = gn_ref[...]
    y_a = []
    for rs in groups:
        gated = []
        for hh in range(N_HEADS):
            ls = slice(hh * HEAD_DIM, (hh + 1) * HEAD_DIM)
            oh = o_ref[rs, ls]
            on = oh * lax.rsqrt(jnp.mean(oh * oh, axis=-1, keepdims=True) + EPS) * gn
            gated.append((on * zs_ref[rs, ls]).astype(BF16))
        y_a.append(_dot(jnp.concatenate(gated, axis=1), wpa_ref[...]))
    y_b = [_mm(ypg_ref[rs, :], wpb_ref[...]) for rs in groups]
    m = [sga_ref[rs, :] * y_a[i] + sgb_ref[rs, :] * y_b[i] for i, rs in enumerate(groups)]
    x1 = [x_ref[rs, :] + _mm(m[i], wout_ref[...]) for i, rs in enumerate(groups)]
    gate = [jax.nn.sigmoid(_mm(_rmsnorm(x1[i], npl_ref[...]), wpg_ref[...])) for i in range(parts)]
    pe = [_mm(p_ref[rs, :], wpp_ref[...]) for rs in groups]
    for i, rs in enumerate(groups):
        x2 = x1[i] + gate[i] * pe[i]
        if final:
            x2 = _rmsnorm(x2, fn_ref[...])
        y_ref[rs, :] = x2


def _merge(li, o, zs, ypg, sga, sgb, x, p, gn, wpa, wpb, wout, npl, wpg, wpp, fn, *, tile, final):
    m = x.shape[0]
    ple = p.shape[-1]
    row_spec = lambda c: pl.BlockSpec((tile, c), lambda i: (i, 0))
    sq = _layer(li, (D_MODEL, D_MODEL))
    return pl.pallas_call(
        functools.partial(_merge_body, final=final, parts=2),
        grid=(m // tile,),
        in_specs=[row_spec(D_MODEL)] * 6 + [pl.BlockSpec((None, tile, ple), lambda i: (li, i, 0)),
                                            _layer(li, (1, HEAD_DIM)), sq, sq, sq, _layer(li, (1, D_MODEL)),
                                            sq, _layer(li, (ple, D_MODEL)), _full((1, D_MODEL))],
        out_specs=row_spec(D_MODEL),
        out_shape=jax.ShapeDtypeStruct((m, D_MODEL), F32),
        compiler_params=pltpu.CompilerParams(dimension_semantics=("arbitrary",), vmem_limit_bytes=VMEM_LIMIT),
        name="merge",
    )(o, zs, ypg, sga, sgb, x, p, gn, wpa, wpb, wout, npl, wpg, wpp, fn)


def kernel(x_prompt, x_sample, p_prompt, p_sample, state_conv, state_delta, state_pool, norm_mix, w_in, conv_w,
           a_log, dt_bias, gdn_norm, w_proj_a, pool_w, pool_scale, w_proj_b, w_out, norm_ple, w_ple_gate,
           w_ple_proj, final_norm):
    depth = w_in.shape[0]
    batch, seq, _ = x_prompt.shape
    nb, steps, _ = x_sample.shape
    rowvec = lambda a: a.reshape(depth, 1, -1)
    pad128 = lambda a: jnp.pad(a, ((0, 0), (0, 128 - a.shape[1]))).reshape(depth, 1, 128)
    mix_w = (rowvec(norm_mix), _wprep(w_in, tile=256), conv_w, pad128(a_log), pad128(dt_bias),
             pool_w.astype(BF16), rowvec(pool_scale))
    merge_w = (rowvec(gdn_norm), w_proj_a.astype(BF16), w_proj_b.astype(BF16), w_out.astype(BF16),
               rowvec(norm_ple), w_ple_gate.astype(BF16), w_ple_proj.astype(BF16), final_norm.reshape(1, -1))

    xp = x_prompt.reshape(batch * seq, D_MODEL)
    pp = p_prompt.reshape(depth, batch * seq, -1)
    conv_p, delta_p, pool_p = [], [], []
    for li in range(depth):
        q, k, v, g, beta, zs, ypg, sga, sgb, cnew, pnew = _mixin_prompt(li, xp, *mix_w, batch=batch, seq=seq, tile=256)
        o, s_new = _delta_prompt(q, k, v, g, beta, batch=batch, seq=seq, tile=512)
        xp = _merge(li, o, zs, ypg, sga, sgb, xp, pp, *merge_w, tile=512, final=(li == depth - 1))
        conv_p.append(cnew)
        delta_p.append(s_new)
        pool_p.append(pnew)
    y_prompt = xp.reshape(batch, seq, D_MODEL)

    swap = lambda a: jnp.swapaxes(a, 0, 1)
    flat = lambda a: a.reshape(-1, a.shape[-1])
    bmajor = lambda a: swap(a.reshape(-1, nb, a.shape[-1]))
    xs = swap(x_sample)
    ps = jnp.swapaxes(p_sample, 1, 2).reshape(depth, steps * nb, -1)
    conv_s, delta_s, pool_s = [], None, []
    for li in range(depth):
        q, k, v, g, beta, zs, ypg, sga, sgb, cnew, pnew = _mixin_sample(
            li, xs, *mix_w, swap(state_conv[li]), swap(state_pool[li]), bb=32, pos0=PAST_LEN)
        wkqd, wv, kt, qkd, glx = _sample_prep(flat(q), flat(k), flat(v), flat(g), flat(beta), steps=steps, nb=nb)
        r, delta_s = _sample_state(li, state_delta, bmajor(wkqd), bmajor(wv), bmajor(kt), glx, delta_s, bb=8)
        o = _sample_out(flat(swap(r)), wv, qkd, steps=steps, nb=nb)
        xs = _merge(li, o, flat(zs), flat(ypg), flat(sga), flat(sgb), flat(xs), ps, *merge_w, tile=steps * nb,
                    final=(li == depth - 1)).reshape(steps, nb, D_MODEL)
        conv_s.append(swap(cnew))
        pool_s.append(swap(pnew))
    y_sample = swap(xs)

    return (y_prompt, y_sample, jnp.stack(conv_p), jnp.stack(delta_p), jnp.stack(pool_p),
            jnp.stack(conv_s), delta_s, jnp.stack(pool_s))
```

```python
import functools

import jax
import jax.numpy as jnp
from jax import lax
from jax.experimental import pallas as pl
from jax.experimental.pallas import tpu as pltpu

F32 = jnp.float32
BF16 = jnp.bfloat16

D_MODEL = 1024
N_HEADS = 8
HEAD_DIM = 128
CONV_W = 4
CONV_CH = 3 * D_MODEL
POOL_WINDOWS = (2, 4, 8, 16)
POOL_HIST = 15
POOL_GC = 256
EPS = 1e-6
PAST_LEN = 16384
CHUNK = 64
BLOCK = 2 * CHUNK
VMEM_LIMIT = 56 * 1024 * 1024

_C_QKV, _C_Z, _C_U, _C_GP, _C_GA, _C_GB, _C_A, _C_B = 0, 3072, 4096, 5120, 6144, 7168, 8192, 8320
_W_COLS = 8448


def _dot(a, b, nt=False):
    dims = (((1,), (1,)), ((), ())) if nt else (((1,), (0,)), ((), ()))
    return lax.dot_general(a, b, dims, preferred_element_type=F32)


def _mm(a, b, nt=False):
    return _dot(a.astype(BF16), b.astype(BF16), nt)


def _split3(x):
    x0 = x.astype(BF16)
    r = x - x0.astype(F32)
    x1 = r.astype(BF16)
    x2 = (r - x1.astype(F32)).astype(BF16)
    return x0, x1, x2


def _mm_exact_rhs(a_bf, b):
    b0, b1, b2 = _split3(b)
    return _dot(a_bf, b0) + _dot(a_bf, b1) + _dot(a_bf, b2)


def _mm_exact_lhs(a, b_bf):
    a0, a1, a2 = _split3(a)
    return _dot(a0, b_bf) + _dot(a1, b_bf) + _dot(a2, b_bf)


def _silu(x):
    return x * jax.nn.sigmoid(x)


def _rmsnorm(x, w):
    return x * lax.rsqrt(jnp.mean(x * x, axis=-1, keepdims=True) + EPS) * w


def _full(shape):
    n = len(shape)
    return pl.BlockSpec(shape, lambda *_: (0,) * n, pipeline_mode=pl.Buffered(1))


def _layer(li, shape):
    n = len(shape)
    return pl.BlockSpec((None,) + tuple(shape), lambda *_: (li,) + (0,) * n, pipeline_mode=pl.Buffered(1))


def _wprep_body(w_ref, o_ref):
    o_ref[:, 0:4096] = w_ref[:, 0:4096].astype(BF16)
    o_ref[:, 4096:8192] = w_ref[:, 4112:8208].astype(BF16)
    ab = w_ref[:, 4096:4224]
    lane = lax.broadcasted_iota(jnp.int32, ab.shape, 1)
    o_ref[:, _C_A:_C_A + 128] = jnp.where(lane < N_HEADS, ab, 0.0).astype(BF16)
    o_ref[:, _C_B:_C_B + 128] = jnp.where(lane < N_HEADS, pltpu.roll(ab, 128 - N_HEADS, 1), 0.0).astype(BF16)


def _wprep(w_in, *, tile):
    depth, k, n = w_in.shape
    return pl.pallas_call(
        _wprep_body, grid=(depth, k // tile),
        in_specs=[pl.BlockSpec((tile, n), lambda l, r: (l * (k // tile) + r, 0))],
        out_specs=pl.BlockSpec((None, tile, _W_COLS), lambda l, r: (l, r, 0)),
        out_shape=jax.ShapeDtypeStruct((depth, k, _W_COLS), BF16),
        compiler_params=pltpu.CompilerParams(dimension_semantics=("arbitrary", "arbitrary"),
                                             vmem_limit_bytes=VMEM_LIMIT),
        name="wprep",
    )(w_in.reshape(depth * k, n))


def _mixin_core(h, w_ref, cw_ref, alog_ref, dtb_ref, pw_ref, ps_ref, outs, pos, conv_tap, pool_sums, store):
    q_ref, k_ref, v_ref, g_ref, b_ref, zs_ref, ypg_ref, sga_ref, sgb_ref = outs
    proj = lambda c0, n=D_MODEL: _dot(h, w_ref[:, c0:c0 + n])
    xcs = [proj(_C_QKV + seg * D_MODEL) for seg in range(3)]
    gates = [proj(_C_Z), proj(_C_GA), proj(_C_GB)]
    a_r = proj(_C_A, 128)
    b_r = proj(_C_B, 128)
    u = proj(_C_U)
    gp = proj(_C_GP)

    for seg, out_ref in enumerate((q_ref, k_ref, v_ref)):
        c0 = seg * D_MODEL
        xc = xcs[seg]
        acc = xc * cw_ref[CONV_W - 1:CONV_W, c0:c0 + D_MODEL]
        for j in range(CONV_W - 1):
            acc = acc + conv_tap(j, c0, xc) * cw_ref[j:j + 1, c0:c0 + D_MODEL]
        y = _silu(acc)
        if seg < 2:
            for hh in range(N_HEADS):
                ls = slice(hh * HEAD_DIM, (hh + 1) * HEAD_DIM)
                yh = y[:, ls]
                inv = lax.rsqrt(jnp.sum(yh * yh, axis=-1, keepdims=True) + EPS)
                if seg == 0:
                    inv = inv * (HEAD_DIM ** -0.5)
                store(out_ref, yh * inv, ls)
        else:
            store(out_ref, y, None)

    store(zs_ref, _silu(gates[0]), None)
    store(sga_ref, jax.nn.sigmoid(gates[1]), None)
    store(sgb_ref, jax.nn.sigmoid(gates[2]), None)
    xs = a_r + dtb_ref[...]
    softplus = jnp.maximum(xs, 0.0) + jnp.log1p(jnp.exp(-jnp.abs(xs)))
    store(g_ref, -jnp.exp(alog_ref[...]) * softplus, None)
    store(b_ref, jax.nn.sigmoid(b_r), None)

    sgp = _silu(gp)
    sums = pool_sums(u)
    for gi, win in enumerate(POOL_WINDOWS):
        l0 = gi * POOL_GC
        ls = slice(l0, l0 + POOL_GC)
        count = jnp.minimum(pos + 1, win).astype(F32)
        y = sums[gi] / count - u[:, ls]
        yp = _mm(y, pw_ref[gi]) * ps_ref[:, ls]
        store(ypg_ref, yp * sgp[:, ls], ls)
    return xcs, u


def _mixin_weight_specs(li):
    return [_layer(li, (1, D_MODEL)), _layer(li, (D_MODEL, _W_COLS)), _layer(li, (CONV_W, CONV_CH)),
            _layer(li, (1, 128)), _layer(li, (1, 128)), _layer(li, (4, POOL_GC, POOL_GC)), _layer(li, (1, D_MODEL))]


def _mixin_prompt_body(x_ref, nm_ref, w_ref, cw_ref, alog_ref, dtb_ref, pw_ref, ps_ref,
                       q_ref, k_ref, v_ref, g_ref, b_ref, zs_ref, ypg_ref, sga_ref, sgb_ref, cnew_ref, pnew_ref,
                       cext, pext, *, rows):
    hc, hp = 8, 16
    t = pl.program_id(1)

    @pl.when(t == 0)
    def _():
        cext[0:hc, :] = jnp.zeros((hc, CONV_CH), F32)
        pext[0:hp, :] = jnp.zeros((hp, D_MODEL), F32)

    h = _rmsnorm(x_ref[...], nm_ref[...]).astype(BF16)

    def conv_tap(j, c0, xc):
        if j == 0:
            cext[hc:hc + rows, c0:c0 + D_MODEL] = xc
        off = hc - (CONV_W - 1 - j)
        return cext[off:off + rows, c0:c0 + D_MODEL]

    def pool_sums(u):
        pext[hp:hp + rows, :] = u
        lvl = pext[...]
        out = []
        for gi, win in enumerate(POOL_WINDOWS):
            lvl = lvl + pltpu.roll(lvl, win // 2, 0)
            out.append(lvl[hp:, :POOL_GC])
            lvl = lvl[:, POOL_GC:]
        return out

    def store(ref, val, ls):
        if ls is None:
            ref[...] = val.astype(ref.dtype)
        else:
            ref[:, ls] = val.astype(ref.dtype)

    pos = t * rows + lax.broadcasted_iota(jnp.int32, (rows, 1), 0)
    outs = (q_ref, k_ref, v_ref, g_ref, b_ref, zs_ref, ypg_ref, sga_ref, sgb_ref)
    _mixin_core(h, w_ref, cw_ref, alog_ref, dtb_ref, pw_ref, ps_ref, outs, pos, conv_tap, pool_sums, store)
    cnew_ref[...] = cext[hc + rows - (CONV_W - 1):hc + rows, :]
    pnew_ref[...] = pext[hp + rows - POOL_HIST:hp + rows, :]
    cext[0:hc, :] = cext[rows:rows + hc, :]
    pext[0:hp, :] = pext[rows:rows + hp, :]


def _mixin_prompt(li, x, nm, w_big, cw, alog, dtb, pw, ps, *, batch, seq, tile):
    m = x.shape[0]
    nt = seq // tile
    row_spec = lambda c: pl.BlockSpec((tile, c), lambda b, t: (b * nt + t, 0))
    in_specs = [row_spec(D_MODEL)] + _mixin_weight_specs(li)
    cnew_spec = pl.BlockSpec((None, CONV_W - 1, CONV_CH), lambda b, t: (b, 0, 0))
    pnew_spec = pl.BlockSpec((None, POOL_HIST, D_MODEL), lambda b, t: (b, 0, 0))
    big = jax.ShapeDtypeStruct((m, D_MODEL), F32)
    small = jax.ShapeDtypeStruct((m, 128), F32)
    gate = jax.ShapeDtypeStruct((m, D_MODEL), BF16)
    out_shape = [big, big, big, small, small, gate, gate, gate, gate,
                 jax.ShapeDtypeStruct((batch, CONV_W - 1, CONV_CH), F32),
                 jax.ShapeDtypeStruct((batch, POOL_HIST, D_MODEL), F32)]
    out_specs = [row_spec(D_MODEL)] * 3 + [row_spec(128)] * 2 + [row_spec(D_MODEL)] * 4 + [cnew_spec, pnew_spec]
    return pl.pallas_call(
        functools.partial(_mixin_prompt_body, rows=tile),
        grid=(batch, nt), in_specs=in_specs, out_specs=out_specs, out_shape=out_shape,
        scratch_shapes=[pltpu.VMEM((8 + tile, CONV_CH), F32), pltpu.VMEM((16 + tile, D_MODEL), F32)],
        compiler_params=pltpu.CompilerParams(dimension_semantics=("arbitrary", "arbitrary"),
                                             vmem_limit_bytes=VMEM_LIMIT),
        name="mixin_prompt",
    )(x, nm, w_big, cw, alog, dtb, pw, ps)


def _mixin_sample_body(x_ref, nm_ref, w_ref, cw_ref, alog_ref, dtb_ref, pw_ref, ps_ref, chist_ref, phist_ref,
                       q_ref, k_ref, v_ref, g_ref, b_ref, zs_ref, ypg_ref, sga_ref, sgb_ref, cnew_ref, pnew_ref,
                       *, steps, bb, pos0):
    rows = steps * bb
    h = _rmsnorm(x_ref[...].reshape(rows, D_MODEL), nm_ref[...]).astype(BF16)

    def delayed(new, hist_ref, nhist, d, ls_new, ls_hist):
        parts = []
        for t in range(steps):
            src = t - d
            parts.append(new[src * bb:(src + 1) * bb, ls_new] if src >= 0 else hist_ref[nhist + src, :, ls_hist])
        return jnp.concatenate(parts, axis=0)

    def conv_tap(j, c0, xc):
        return delayed(xc, chist_ref, CONV_W - 1, CONV_W - 1 - j, slice(None), slice(c0, c0 + D_MODEL))

    def pool_sums(u):
        out = []
        for gi, win in enumerate(POOL_WINDOWS):
            ls = slice(gi * POOL_GC, (gi + 1) * POOL_GC)
            acc = u[:, ls]
            for s in range(1, win):
                acc = acc + delayed(u, phist_ref, POOL_HIST, s, ls, ls)
            out.append(acc)
        return out

    def store(ref, val, ls):
        val = val.reshape(steps, bb, val.shape[-1]).astype(ref.dtype)
        if ls is None:
            ref[...] = val
        else:
            ref[:, :, ls] = val

    pos = pos0 + lax.broadcasted_iota(jnp.int32, (rows, 1), 0) // bb
    outs = (q_ref, k_ref, v_ref, g_ref, b_ref, zs_ref, ypg_ref, sga_ref, sgb_ref)
    xcs, u = _mixin_core(h, w_ref, cw_ref, alog_ref, dtb_ref, pw_ref, ps_ref, outs, pos, conv_tap, pool_sums, store)
    for i in range(CONV_W - 1):
        src = steps + i - (CONV_W - 1)
        for seg in range(3):
            ls = slice(seg * D_MODEL, (seg + 1) * D_MODEL)
            cnew_ref[i, :, ls] = (xcs[seg][src * bb:(src + 1) * bb, :] if src >= 0
                                  else chist_ref[CONV_W - 1 + src, :, ls])
    for i in range(POOL_HIST):
        src = steps + i - POOL_HIST
        pnew_ref[i] = u[src * bb:(src + 1) * bb, :] if src >= 0 else phist_ref[POOL_HIST + src]


def _mixin_sample(li, x, nm, w_big, cw, alog, dtb, pw, ps, chist, phist, *, bb, pos0):
    steps, nb, _ = x.shape
    slab_spec = lambda n, c: pl.BlockSpec((n, bb, c), lambda i: (0, i, 0))
    in_specs = ([slab_spec(steps, D_MODEL)] + _mixin_weight_specs(li)
                + [slab_spec(CONV_W - 1, CONV_CH), slab_spec(POOL_HIST, D_MODEL)])
    big = jax.ShapeDtypeStruct((steps, nb, D_MODEL), F32)
    small = jax.ShapeDtypeStruct((steps, nb, 128), F32)
    gate = jax.ShapeDtypeStruct((steps, nb, D_MODEL), BF16)
    out_shape = [big, big, big, small, small, gate, gate, gate, gate,
                 jax.ShapeDtypeStruct(chist.shape, F32), jax.ShapeDtypeStruct(phist.shape, F32)]
    out_specs = ([slab_spec(steps, D_MODEL)] * 3 + [slab_spec(steps, 128)] * 2 + [slab_spec(steps, D_MODEL)] * 4
                 + [slab_spec(CONV_W - 1, CONV_CH), slab_spec(POOL_HIST, D_MODEL)])
    return pl.pallas_call(
        functools.partial(_mixin_sample_body, steps=steps, bb=bb, pos0=pos0),
        grid=(nb // bb,), in_specs=in_specs, out_specs=out_specs, out_shape=out_shape,
        compiler_params=pltpu.CompilerParams(dimension_semantics=("arbitrary",), vmem_limit_bytes=VMEM_LIMIT),
        name="mixin_sample",
    )(x, nm, w_big, cw, alog, dtb, pw, ps, chist, phist)


def _delta_body(q_ref, k_ref, v_ref, g_ref, b_ref, o_ref, sout_ref, s_ref, *, tile, nseq):
    t = pl.program_id(1)

    @pl.when(t == 0)
    def _():
        s_ref[...] = jnp.zeros(s_ref.shape, F32)

    r = lax.broadcasted_iota(jnp.int32, (BLOCK, BLOCK), 0)
    c = lax.broadcasted_iota(jnp.int32, (BLOCK, BLOCK), 1)
    same = (r // CHUNK) == (c // CHUNK)
    low = same & (r >= c)
    strict = same & (r > c)
    eye = jnp.where(r == c, 1.0, 0.0).astype(F32)
    l_blk = jnp.where(low, 1.0, 0.0).astype(BF16)
    ones0 = jnp.where(c < CHUNK, 1.0, 0.0).astype(BF16)
    ones1 = jnp.where(c >= CHUNK, 1.0, 0.0).astype(BF16)
    first = lax.broadcasted_iota(jnp.int32, (BLOCK, 1), 0) < CHUNK
    zeros_half = jnp.zeros((CHUNK, HEAD_DIM), F32)
    cat = jnp.concatenate
    col = lambda a, hh: a[:, hh:hh + 1]
    units = [(sq, hh) for sq in range(nseq) for hh in range(N_HEADS)]
    ls = [slice(hh * HEAD_DIM, (hh + 1) * HEAD_DIM) for hh in range(N_HEADS)]

    def block(i, carry):
        rows = pl.ds(pl.multiple_of(i * BLOCK, BLOCK), BLOCK)
        gp = [g_ref[sq, rows, :] for sq in range(nseq)]
        bt = [b_ref[sq, rows, :] for sq in range(nseq)]
        g_cum = [_mm_exact_rhs(l_blk, x) for x in gp]
        tot0 = [_mm_exact_rhs(ones0, x) for x in gp]
        tot1 = [_mm_exact_rhs(ones1, x) for x in gp]
        g_t = [x.T for x in g_cum]
        e_g = [jnp.exp(x) for x in g_cum]
        e_tail = [jnp.exp(jnp.where(first, tot0[sq], tot1[sq]) - g_cum[sq]) for sq in range(nseq)]
        e_tot = ([jnp.exp(x) for x in tot0], [jnp.exp(x) for x in tot1])

        kh = [k_ref[sq, rows, ls[hh]] for sq, hh in units]
        qh = [q_ref[sq, rows, ls[hh]] for sq, hh in units]
        beta = [col(bt[sq], hh) for sq, hh in units]
        eg = [col(e_g[sq], hh) for sq, hh in units]
        decay = [jnp.exp(jnp.where(low, col(g_cum[sq], hh) - g_t[sq][hh:hh + 1, :], -jnp.inf)) for sq, hh in units]
        n_units = range(len(units))
        kq = [_mm(cat([kh[n], qh[n]], axis=0), kh[n], nt=True) for n in n_units]
        p = [-jnp.where(strict, kq[n][:BLOCK] * decay[n] * beta[n], 0.0) for n in n_units]
        qk = [kq[n][BLOCK:] * decay[n] for n in n_units]
        tinv = [eye + p[n] for n in n_units]
        p = [_mm(p[n], p[n]) for n in n_units]
        for _ in range(4):
            pp = [_mm(p[n], cat([p[n], tinv[n]], axis=1)) for n in n_units]
            p = [pp[n][:, :BLOCK] for n in n_units]
            tinv = [tinv[n] + pp[n][:, BLOCK:] for n in n_units]
        tinv = [tinv[n] + _mm(p[n], tinv[n]) for n in n_units]
        sol = [_mm(tinv[n], cat([v_ref[sq, rows, ls[hh]] * beta[n], kh[n] * (beta[n] * eg[n])], axis=1))
               for n, (sq, hh) in enumerate(units)]
        wv = [sol[n][:, :HEAD_DIM] for n in n_units]
        wk = [sol[n][:, HEAD_DIM:] for n in n_units]
        qd = [qh[n] * eg[n] for n in n_units]
        kt_t = [(kh[n] * col(e_tail[sq], hh)).T for n, (sq, hh) in enumerate(units)]
        s_cur = [s_ref[sq, hh] for sq, hh in units]
        o_parts = []
        for half in range(2):
            hs = slice(half * CHUNK, (half + 1) * CHUNK)
            res = [_mm(cat([wk[n][hs], qd[n][hs]], axis=0), s_cur[n]) for n in n_units]
            u_new = [wv[n][hs] - res[n][:CHUNK] for n in n_units]
            u_pad = [cat([u_new[n], zeros_half] if half == 0 else [zeros_half, u_new[n]], axis=0) for n in n_units]
            upd = [_mm(cat([qk[n][hs], kt_t[n]], axis=0), u_pad[n]) for n in n_units]
            o_parts.append([res[n][CHUNK:] + upd[n][:CHUNK] for n in n_units])
            s_cur = [s_cur[n] * col(e_tot[half][sq], hh) + upd[n][CHUNK:] for n, (sq, hh) in enumerate(units)]
        for n, (sq, hh) in enumerate(units):
            s_ref[sq, hh] = s_cur[n]
            o_ref[sq, rows, ls[hh]] = cat([o_parts[0][n], o_parts[1][n]], axis=0)
        return carry

    lax.fori_loop(0, tile // BLOCK, block, 0)
    sout_ref[...] = s_ref[...]


def _delta_prompt(q, k, v, g, beta, *, batch, seq, tile, nseq):
    as3d = lambda a: a.reshape(batch, seq, a.shape[-1])
    row_spec = lambda c: pl.BlockSpec((nseq, tile, c), lambda b, t: (b, t, 0))
    s_shape = (N_HEADS, HEAD_DIM, HEAD_DIM)
    o, s_new = pl.pallas_call(
        functools.partial(_delta_body, tile=tile, nseq=nseq),
        grid=(batch // nseq, seq // tile),
        in_specs=[row_spec(D_MODEL)] * 3 + [row_spec(128)] * 2,
        out_specs=[row_spec(D_MODEL), pl.BlockSpec((nseq,) + s_shape, lambda b, t: (b, 0, 0, 0))],
        out_shape=[jax.ShapeDtypeStruct((batch, seq, D_MODEL), F32), jax.ShapeDtypeStruct((batch,) + s_shape, F32)],
        scratch_shapes=[pltpu.VMEM((nseq,) + s_shape, F32)],
        compiler_params=pltpu.CompilerParams(dimension_semantics=("arbitrary", "arbitrary"),
                                             vmem_limit_bytes=VMEM_LIMIT),
        name="delta_prompt",
    )(as3d(q), as3d(k), as3d(v), as3d(g), as3d(beta))
    return o.reshape(batch * seq, D_MODEL), s_new


def _head_indicators():
    d = lax.broadcasted_iota(jnp.int32, (D_MODEL, 128), 0) // HEAD_DIM
    hcol = lax.broadcasted_iota(jnp.int32, (D_MODEL, 128), 1)
    e_sum = jnp.where(d == hcol, 1.0, 0.0).astype(BF16)
    hrow = lax.broadcasted_iota(jnp.int32, (128, D_MODEL), 0)
    d2 = lax.broadcasted_iota(jnp.int32, (128, D_MODEL), 1) // HEAD_DIM
    e_bc = jnp.where(hrow == d2, 1.0, 0.0).astype(BF16)
    return e_sum, e_bc


def _sample_prep_body(q_ref, k_ref, v_ref, g_ref, b_ref, wkqd_ref, wv_ref, kt_ref, qkd_ref, glx_ref,
                      *, steps, nb):
    e_sum, e_bc = _head_indicators()
    expand = lambda x: _mm_exact_lhs(x, e_bc)
    hsum = lambda y: _mm_exact_lhs(y, e_sum)
    sl = lambda ref, i: ref[i * nb:(i + 1) * nb, :]
    q = [sl(q_ref, i) for i in range(steps)]
    k = [sl(k_ref, i) for i in range(steps)]
    v = [sl(v_ref, i) for i in range(steps)]
    beta = [sl(b_ref, i) for i in range(steps)]
    g_cum = []
    for i in range(steps):
        gi = sl(g_ref, i)
        g_cum.append(gi if i == 0 else g_cum[-1] + gi)
    wv, wk = [], []
    for i in range(steps):
        acc_v = v[i] * expand(beta[i])
        acc_k = k[i] * expand(beta[i] * jnp.exp(g_cum[i]))
        for j in range(i):
            a_ij = expand(hsum(k[i] * k[j]) * jnp.exp(g_cum[i] - g_cum[j]) * beta[i])
            acc_v = acc_v - a_ij * wv[j]
            acc_k = acc_k - a_ij * wk[j]
        wv.append(acc_v)
        wk.append(acc_k)
    zeros = jnp.zeros((nb, D_MODEL), F32)
    for i in range(steps):
        wkqd_ref[i * nb:(i + 1) * nb, :] = wk[i]
        wkqd_ref[(steps + i) * nb:(steps + i + 1) * nb, :] = q[i] * expand(jnp.exp(g_cum[i]))
        wv_ref[i * nb:(i + 1) * nb, :] = wv[i]
        wv_ref[(steps + i) * nb:(steps + i + 1) * nb, :] = zeros
        kt_ref[i * nb:(i + 1) * nb, :] = k[i] * expand(jnp.exp(g_cum[steps - 1] - g_cum[i]))
        kt_ref[(steps + i) * nb:(steps + i + 1) * nb, :] = zeros
        for j in range(steps):
            idx = i * steps + j
            if j <= i:
                qkd_ref[idx * nb:(idx + 1) * nb, :] = hsum(q[i] * k[j]) * jnp.exp(g_cum[i] - g_cum[j])
            else:
                qkd_ref[idx * nb:(idx + 1) * nb, :] = jnp.zeros((nb, 128), F32)
    glx_ref[...] = expand(jnp.exp(g_cum[steps - 1]))


def _sample_prep(q, k, v, g, beta, *, steps, nb):
    m = steps * nb
    out_shape = [jax.ShapeDtypeStruct((2 * m, D_MODEL), F32)] * 3 + [
        jax.ShapeDtypeStruct((steps * steps * nb, 128), F32), jax.ShapeDtypeStruct((nb, D_MODEL), F32)]
    return pl.pallas_call(
        functools.partial(_sample_prep_body, steps=steps, nb=nb),
        grid=(1,),
        in_specs=[_full((m, D_MODEL))] * 3 + [_full((m, 128))] * 2,
        out_specs=[_full(s.shape) for s in out_shape],
        out_shape=out_shape,
        compiler_params=pltpu.CompilerParams(dimension_semantics=("arbitrary",), vmem_limit_bytes=VMEM_LIMIT),
        name="sample_prep",
    )(q, k, v, g, beta)


def _sample_state_body(*refs, bb):
    s_ref, wkqd_ref, wv_ref, kt_ref, glx_ref = refs[:5]
    r_ref, snew_ref = refs[-2:]
    zeros_pad = jnp.zeros((HEAD_DIM - 8, HEAD_DIM), F32)

    def per_seq(b, carry):
        heads = range(N_HEADS)
        ls = [slice(hh * HEAD_DIM, (hh + 1) * HEAD_DIM) for hh in heads]
        pad = lambda a: jnp.concatenate([a, zeros_pad], axis=0)
        s0 = [s_ref[b, hh] for hh in heads]
        res = [_mm(pad(wkqd_ref[b, :, ls[hh]]), s0[hh])[:8] for hh in heads]
        kt_t = [pad(kt_ref[b, :, ls[hh]]).T for hh in heads]
        upd = [_mm(kt_t[hh], pad(wv_ref[b, :, ls[hh]] - res[hh])) for hh in heads]
        for hh in heads:
            r_ref[b, :, ls[hh]] = res[hh]
            snew_ref[b, hh] = s0[hh] * glx_ref[b, :, ls[hh]] + upd[hh]
        return carry

    lax.fori_loop(0, bb, per_seq, 0)


def _sample_state(li, state, wkqd, wv, kt, glx, prev, *, bb):
    nb = state.shape[1]
    s_spec = pl.BlockSpec((None, bb, N_HEADS, HEAD_DIM, HEAD_DIM), lambda i: (li, i, 0, 0, 0))
    slot_spec = pl.BlockSpec((bb, 8, D_MODEL), lambda i: (i, 0, 0))
    in_specs = [s_spec, slot_spec, slot_spec, slot_spec, pl.BlockSpec((bb, 1, D_MODEL), lambda i: (i, 0, 0))]
    args = [state, wkqd, wv, kt, glx.reshape(nb, 1, D_MODEL)]
    aliases = {}
    if prev is not None:
        in_specs.append(pl.BlockSpec(memory_space=pl.ANY))
        args.append(prev)
        aliases = {len(args) - 1: 1}
    return pl.pallas_call(
        functools.partial(_sample_state_body, bb=bb),
        grid=(nb // bb,),
        in_specs=in_specs,
        out_specs=[slot_spec, s_spec],
        out_shape=[jax.ShapeDtypeStruct((nb, 8, D_MODEL), F32), jax.ShapeDtypeStruct(state.shape, F32)],
        input_output_aliases=aliases,
        compiler_params=pltpu.CompilerParams(dimension_semantics=("arbitrary",), vmem_limit_bytes=VMEM_LIMIT),
        name="sample_state",
    )(*args)


def _sample_out_body(r_ref, wv_ref, qkd_ref, o_ref, *, steps, nb):
    _, e_bc = _head_indicators()
    sl = lambda ref, i: ref[i * nb:(i + 1) * nb, :]
    u = [sl(wv_ref, j) - sl(r_ref, j) for j in range(steps)]
    for i in range(steps):
        acc = sl(r_ref, steps + i)
        for j in range(i + 1):
            acc = acc + _mm_exact_lhs(sl(qkd_ref, i * steps + j), e_bc) * u[j]
        o_ref[i * nb:(i + 1) * nb, :] = acc


def _sample_out(r, wv, qkd, *, steps, nb):
    m = steps * nb
    return pl.pallas_call(
        functools.partial(_sample_out_body, steps=steps, nb=nb),
        grid=(1,),
        in_specs=[_full(r.shape), _full(wv.shape), _full(qkd.shape)],
        out_specs=_full((m, D_MODEL)),
        out_shape=jax.ShapeDtypeStruct((m, D_MODEL), F32),
        compiler_params=pltpu.CompilerParams(dimension_semantics=("arbitrary",), vmem_limit_bytes=VMEM_LIMIT),
        name="sample_out",
    )(r, wv, qkd)


def _merge_body(o_ref, zs_ref, ypg_ref, sga_ref, sgb_ref, x_ref, p_ref, gn_ref, wpa_ref, wpb_ref, wout_ref,
                npl_ref, wpg_ref, wpp_ref, fn_ref, y_ref, *, final, parts):
    step = o_ref.shape[0] // parts
    groups = [slice(i * step, (i + 1) * step) for i in range(parts)]
    gn = gn_ref[...]
    y_a = []
    for rs in groups:
        gated = []
        for hh in range(N_HEADS):
            ls = slice(hh * HEAD_DIM, (hh + 1) * HEAD_DIM)
            oh = o_ref[rs, ls]
            on = oh * lax.rsqrt(jnp.mean(oh * oh, axis=-1, keepdims=True) + EPS) * gn
            gated.append((on * zs_ref[rs, ls]).astype(BF16))
        y_a.append(_dot(jnp.concatenate(gated, axis=1), wpa_ref[...]))
    y_b = [_mm(ypg_ref[rs, :], wpb_ref[...]) for rs in groups]
    m = [sga_ref[rs, :] * y_a[i] + sgb_ref[rs, :] * y_b[i] for i, rs in enumerate(groups)]
    x1 = [x_ref[rs, :] + _mm(m[i], wout_ref[...]) for i, rs in enumerate(groups)]
    gate = [jax.nn.sigmoid(_mm(_rmsnorm(x1[i], npl_ref[...]), wpg_ref[...])) for i in range(parts)]
    pe = [_mm(p_ref[rs, :], wpp_ref[...]) for rs in groups]
    for i, rs in enumerate(groups):
        x2 = x1[i] + gate[i] * pe[i]
        if final:
            x2 = _rmsnorm(x2, fn_ref[...])
        y_ref[rs, :] = x2


def _merge(li, o, zs, ypg, sga, sgb, x, p, gn, wpa, wpb, wout, npl, wpg, wpp, fn, *, tile, final):
    m = x.shape[0]
    ple = p.shape[-1]
    row_spec = lambda c: pl.BlockSpec((tile, c), lambda i: (i, 0))
    sq = _layer(li, (D_MODEL, D_MODEL))
    return pl.pallas_call(
        functools.partial(_merge_body, final=final, parts=2),
        grid=(m // tile,),
        in_specs=[row_spec(D_MODEL)] * 6 + [pl.BlockSpec((None, tile, ple), lambda i: (li, i, 0)),
                                            _layer(li, (1, HEAD_DIM)), sq, sq, sq, _layer(li, (1, D_MODEL)),
                                            sq, _layer(li, (ple, D_MODEL)), _full((1, D_MODEL))],
        out_specs=row_spec(D_MODEL),
        out_shape=jax.ShapeDtypeStruct((m, D_MODEL), F32),
        compiler_params=pltpu.CompilerParams(dimension_semantics=("arbitrary",), vmem_limit_bytes=VMEM_LIMIT),
        name="merge",
    )(o, zs, ypg, sga, sgb, x, p, gn, wpa, wpb, wout, npl, wpg, wpp, fn)


def kernel(x_prompt, x_sample, p_prompt, p_sample, state_conv, state_delta, state_pool, norm_mix, w_in, conv_w,
           a_log, dt_bias, gdn_norm, w_proj_a, pool_w, pool_scale, w_proj_b, w_out, norm_ple, w_ple_gate,
           w_ple_proj, final_norm):
    depth = w_in.shape[0]
    batch, seq, _ = x_prompt.shape
    nb, steps, _ = x_sample.shape
    rowvec = lambda a: a.reshape(depth, 1, -1)
    pad128 = lambda a: jnp.pad(a, ((0, 0), (0, 128 - a.shape[1]))).reshape(depth, 1, 128)
    mix_w = (rowvec(norm_mix), _wprep(w_in, tile=256), conv_w, pad128(a_log), pad128(dt_bias),
             pool_w.astype(BF16), rowvec(pool_scale))
    merge_w = (rowvec(gdn_norm), w_proj_a.astype(BF16), w_proj_b.astype(BF16), w_out.astype(BF16),
               rowvec(norm_ple), w_ple_gate.astype(BF16), w_ple_proj.astype(BF16), final_norm.reshape(1, -1))

    xp = x_prompt.reshape(batch * seq, D_MODEL)
    pp = p_prompt.reshape(depth, batch * seq, -1)
    conv_p, delta_p, pool_p = [], [], []
    for li in range(depth):
        q, k, v, g, beta, zs, ypg, sga, sgb, cnew, pnew = _mixin_prompt(li, xp, *mix_w, batch=batch, seq=seq, tile=256)
        o, s_new = _delta_prompt(q, k, v, g, beta, batch=batch, seq=seq, tile=256, nseq=4)
        xp = _merge(li, o, zs, ypg, sga, sgb, xp, pp, *merge_w, tile=512, final=(li == depth - 1))
        conv_p.append(cnew)
        delta_p.append(s_new)
        pool_p.append(pnew)
    y_prompt = xp.reshape(batch, seq, D_MODEL)

    swap = lambda a: jnp.swapaxes(a, 0, 1)
    flat = lambda a: a.reshape(-1, a.shape[-1])
    bmajor = lambda a: swap(a.reshape(-1, nb, a.shape[-1]))
    xs = swap(x_sample)
    ps = jnp.swapaxes(p_sample, 1, 2).reshape(depth, steps * nb, -1)
    conv_s, delta_s, pool_s = [], None, []
    for li in range(depth):
        q, k, v, g, beta, zs, ypg, sga, sgb, cnew, pnew = _mixin_sample(
            li, xs, *mix_w, swap(state_conv[li]), swap(state_pool[li]), bb=32, pos0=PAST_LEN)
        wkqd, wv, kt, qkd, glx = _sample_prep(flat(q), flat(k), flat(v), flat(g), flat(beta), steps=steps, nb=nb)
        r, delta_s = _sample_state(li, state_delta, bmajor(wkqd), bmajor(wv), bmajor(kt), glx, delta_s, bb=8)
        o = _sample_out(flat(swap(r)), wv, qkd, steps=steps, nb=nb)
        xs = _merge(li, o, flat(zs), flat(ypg), flat(sga), flat(sgb), flat(xs), ps, *merge_w, tile=steps * nb,
                    final=(li == depth - 1)).reshape(steps, nb, D_MODEL)
        conv_s.append(swap(cnew))
        pool_s.append(swap(pnew))
    y_sample = swap(xs)

    return (y_prompt, y_sample, jnp.stack(conv_p), jnp.stack(delta_p), jnp.stack(pool_p),
            jnp.stack(conv_s), delta_s, jnp.stack(pool_s))
```

```python
import functools

import jax
import jax.numpy as jnp
from jax import lax
from jax.experimental import pallas as pl
from jax.experimental.pallas import tpu as pltpu

F32 = jnp.float32
BF16 = jnp.bfloat16

D_MODEL = 1024
N_HEADS = 8
HEAD_DIM = 128
CONV_W = 4
CONV_CH = 3 * D_MODEL
POOL_WINDOWS = (2, 4, 8, 16)
POOL_HIST = 15
POOL_GC = 256
EPS = 1e-6
PAST_LEN = 16384
CHUNK = 64
BLOCK = 2 * CHUNK
VMEM_LIMIT = 56 * 1024 * 1024

_C_QKV, _C_Z, _C_U, _C_GP, _C_GA, _C_GB, _C_A, _C_B = 0, 3072, 4096, 5120, 6144, 7168, 8192, 8320
_W_COLS = 8448


def _dot(a, b, nt=False):
    dims = (((1,), (1,)), ((), ())) if nt else (((1,), (0,)), ((), ()))
    return lax.dot_general(a, b, dims, preferred_element_type=F32)


def _mm(a, b, nt=False):
    return _dot(a.astype(BF16), b.astype(BF16), nt)


def _split3(x):
    x0 = x.astype(BF16)
    r = x - x0.astype(F32)
    x1 = r.astype(BF16)
    x2 = (r - x1.astype(F32)).astype(BF16)
    return x0, x1, x2


def _mm_exact_rhs(a_bf, b):
    b0, b1, b2 = _split3(b)
    return _dot(a_bf, b0) + _dot(a_bf, b1) + _dot(a_bf, b2)


def _mm_exact_lhs(a, b_bf):
    a0, a1, a2 = _split3(a)
    return _dot(a0, b_bf) + _dot(a1, b_bf) + _dot(a2, b_bf)


def _silu(x):
    return x * jax.nn.sigmoid(x)


def _rmsnorm(x, w):
    return x * lax.rsqrt(jnp.mean(x * x, axis=-1, keepdims=True) + EPS) * w


def _full(shape):
    n = len(shape)
    return pl.BlockSpec(shape, lambda *_: (0,) * n, pipeline_mode=pl.Buffered(1))


def _layer(li, shape):
    n = len(shape)
    return pl.BlockSpec((None,) + tuple(shape), lambda *_: (li,) + (0,) * n, pipeline_mode=pl.Buffered(1))


def _wprep_body(w_ref, o_ref):
    o_ref[:, 0:4096] = w_ref[:, 0:4096].astype(BF16)
    o_ref[:, 4096:8192] = w_ref[:, 4112:8208].astype(BF16)
    ab = w_ref[:, 4096:4224]
    lane = lax.broadcasted_iota(jnp.int32, ab.shape, 1)
    o_ref[:, _C_A:_C_A + 128] = jnp.where(lane < N_HEADS, ab, 0.0).astype(BF16)
    o_ref[:, _C_B:_C_B + 128] = jnp.where(lane < N_HEADS, pltpu.roll(ab, 128 - N_HEADS, 1), 0.0).astype(BF16)


def _wprep(w_in, *, tile):
    depth, k, n = w_in.shape
    return pl.pallas_call(
        _wprep_body, grid=(depth, k // tile),
        in_specs=[pl.BlockSpec((tile, n), lambda l, r: (l * (k // tile) + r, 0))],
        out_specs=pl.BlockSpec((None, tile, _W_COLS), lambda l, r: (l, r, 0)),
        out_shape=jax.ShapeDtypeStruct((depth, k, _W_COLS), BF16),
        compiler_params=pltpu.CompilerParams(dimension_semantics=("arbitrary", "arbitrary"),
                                             vmem_limit_bytes=VMEM_LIMIT),
        name="wprep",
    )(w_in.reshape(depth * k, n))


def _mixin_core(h, w_ref, cw_ref, alog_ref, dtb_ref, pw_ref, ps_ref, outs, pos, conv_tap, pool_sums, store):
    q_ref, k_ref, v_ref, g_ref, b_ref, zs_ref, ypg_ref, sga_ref, sgb_ref = outs
    proj = lambda c0, n=D_MODEL: _dot(h, w_ref[:, c0:c0 + n])
    xcs = [proj(_C_QKV + seg * D_MODEL) for seg in range(3)]
    gates = [proj(_C_Z), proj(_C_GA), proj(_C_GB)]
    a_r = proj(_C_A, 128)
    b_r = proj(_C_B, 128)
    u = proj(_C_U)
    gp = proj(_C_GP)

    for seg, out_ref in enumerate((q_ref, k_ref, v_ref)):
        c0 = seg * D_MODEL
        xc = xcs[seg]
        acc = xc * cw_ref[CONV_W - 1:CONV_W, c0:c0 + D_MODEL]
        for j in range(CONV_W - 1):
            acc = acc + conv_tap(j, c0, xc) * cw_ref[j:j + 1, c0:c0 + D_MODEL]
        y = _silu(acc)
        if seg < 2:
            for hh in range(N_HEADS):
                ls = slice(hh * HEAD_DIM, (hh + 1) * HEAD_DIM)
                yh = y[:, ls]
                inv = lax.rsqrt(jnp.sum(yh * yh, axis=-1, keepdims=True) + EPS)
                if seg == 0:
                    inv = inv * (HEAD_DIM ** -0.5)
                store(out_ref, yh * inv, ls)
        else:
            store(out_ref, y, None)

    store(zs_ref, _silu(gates[0]), None)
    store(sga_ref, jax.nn.sigmoid(gates[1]), None)
    store(sgb_ref, jax.nn.sigmoid(gates[2]), None)
    xs = a_r + dtb_ref[...]
    softplus = jnp.maximum(xs, 0.0) + jnp.log1p(jnp.exp(-jnp.abs(xs)))
    store(g_ref, -jnp.exp(alog_ref[...]) * softplus, None)
    store(b_ref, jax.nn.sigmoid(b_r), None)

    sgp = _silu(gp)
    sums = pool_sums(u)
    for gi, win in enumerate(POOL_WINDOWS):
        l0 = gi * POOL_GC
        ls = slice(l0, l0 + POOL_GC)
        count = jnp.minimum(pos + 1, win).astype(F32)
        y = sums[gi] / count - u[:, ls]
        yp = _mm(y, pw_ref[gi]) * ps_ref[:, ls]
        store(ypg_ref, yp * sgp[:, ls], ls)
    return xcs, u


def _mixin_weight_specs(li):
    return [_layer(li, (1, D_MODEL)), _layer(li, (D_MODEL, _W_COLS)), _layer(li, (CONV_W, CONV_CH)),
            _layer(li, (1, 128)), _layer(li, (1, 128)), _layer(li, (4, POOL_GC, POOL_GC)), _layer(li, (1, D_MODEL))]


def _mixin_prompt_body(x_ref, nm_ref, w_ref, cw_ref, alog_ref, dtb_ref, pw_ref, ps_ref,
                       q_ref, k_ref, v_ref, g_ref, b_ref, zs_ref, ypg_ref, sga_ref, sgb_ref, cnew_ref, pnew_ref,
                       cext, pext, *, rows):
    hc, hp = 8, 16
    t = pl.program_id(1)

    @pl.when(t == 0)
    def _():
        cext[0:hc, :] = jnp.zeros((hc, CONV_CH), F32)
        pext[0:hp, :] = jnp.zeros((hp, D_MODEL), F32)

    h = _rmsnorm(x_ref[...], nm_ref[...]).astype(BF16)

    def conv_tap(j, c0, xc):
        if j == 0:
            cext[hc:hc + rows, c0:c0 + D_MODEL] = xc
        off = hc - (CONV_W - 1 - j)
        return cext[off:off + rows, c0:c0 + D_MODEL]

    def pool_sums(u):
        pext[hp:hp + rows, :] = u
        lvl = pext[...]
        out = []
        for gi, win in enumerate(POOL_WINDOWS):
            lvl = lvl + pltpu.roll(lvl, win // 2, 0)
            out.append(lvl[hp:, :POOL_GC])
            lvl = lvl[:, POOL_GC:]
        return out

    def store(ref, val, ls):
        if ls is None:
            ref[...] = val.astype(ref.dtype)
        else:
            ref[:, ls] = val.astype(ref.dtype)

    pos = t * rows + lax.broadcasted_iota(jnp.int32, (rows, 1), 0)
    outs = (q_ref, k_ref, v_ref, g_ref, b_ref, zs_ref, ypg_ref, sga_ref, sgb_ref)
    _mixin_core(h, w_ref, cw_ref, alog_ref, dtb_ref, pw_ref, ps_ref, outs, pos, conv_tap, pool_sums, store)
    cnew_ref[...] = cext[hc + rows - (CONV_W - 1):hc + rows, :]
    pnew_ref[...] = pext[hp + rows - POOL_HIST:hp + rows, :]
    cext[0:hc, :] = cext[rows:rows + hc, :]
    pext[0:hp, :] = pext[rows:rows + hp, :]


def _mixin_prompt(li, x, nm, w_big, cw, alog, dtb, pw, ps, *, batch, seq, tile):
    m = x.shape[0]
    nt = seq // tile
    row_spec = lambda c: pl.BlockSpec((tile, c), lambda b, t: (b * nt + t, 0))
    in_specs = [row_spec(D_MODEL)] + _mixin_weight_specs(li)
    cnew_spec = pl.BlockSpec((None, CONV_W - 1, CONV_CH), lambda b, t: (b, 0, 0))
    pnew_spec = pl.BlockSpec((None, POOL_HIST, D_MODEL), lambda b, t: (b, 0, 0))
    big = jax.ShapeDtypeStruct((m, D_MODEL), F32)
    small = jax.ShapeDtypeStruct((m, 128), F32)
    gate = jax.ShapeDtypeStruct((m, D_MODEL), BF16)
    out_shape = [big, big, big, small, small, gate, gate, gate, gate,
                 jax.ShapeDtypeStruct((batch, CONV_W - 1, CONV_CH), F32),
                 jax.ShapeDtypeStruct((batch, POOL_HIST, D_MODEL), F32)]
    out_specs = [row_spec(D_MODEL)] * 3 + [row_spec(128)] * 2 + [row_spec(D_MODEL)] * 4 + [cnew_spec, pnew_spec]
    return pl.pallas_call(
        functools.partial(_mixin_prompt_body, rows=tile),
        grid=(batch, nt), in_specs=in_specs, out_specs=out_specs, out_shape=out_shape,
        scratch_shapes=[pltpu.VMEM((8 + tile, CONV_CH), F32), pltpu.VMEM((16 + tile, D_MODEL), F32)],
        compiler_params=pltpu.CompilerParams(dimension_semantics=("arbitrary", "arbitrary"),
                                             vmem_limit_bytes=VMEM_LIMIT),
        name="mixin_prompt",
    )(x, nm, w_big, cw, alog, dtb, pw, ps)


def _mixin_sample_body(x_ref, nm_ref, w_ref, cw_ref, alog_ref, dtb_ref, pw_ref, ps_ref, chist_ref, phist_ref,
                       q_ref, k_ref, v_ref, g_ref, b_ref, zs_ref, ypg_ref, sga_ref, sgb_ref, cnew_ref, pnew_ref,
                       *, steps, bb, pos0):
    rows = steps * bb
    h = _rmsnorm(x_ref[...].reshape(rows, D_MODEL), nm_ref[...]).astype(BF16)

    def delayed(new, hist_ref, nhist, d, ls_new, ls_hist):
        parts = []
        for t in range(steps):
            src = t - d
            parts.append(new[src * bb:(src + 1) * bb, ls_new] if src >= 0 else hist_ref[nhist + src, :, ls_hist])
        return jnp.concatenate(parts, axis=0)

    def conv_tap(j, c0, xc):
        return delayed(xc, chist_ref, CONV_W - 1, CONV_W - 1 - j, slice(None), slice(c0, c0 + D_MODEL))

    def pool_sums(u):
        out = []
        for gi, win in enumerate(POOL_WINDOWS):
            ls = slice(gi * POOL_GC, (gi + 1) * POOL_GC)
            acc = u[:, ls]
            for s in range(1, win):
                acc = acc + delayed(u, phist_ref, POOL_HIST, s, ls, ls)
            out.append(acc)
        return out

    def store(ref, val, ls):
        val = val.reshape(steps, bb, val.shape[-1]).astype(ref.dtype)
        if ls is None:
            ref[...] = val
        else:
            ref[:, :, ls] = val

    pos = pos0 + lax.broadcasted_iota(jnp.int32, (rows, 1), 0) // bb
    outs = (q_ref, k_ref, v_ref, g_ref, b_ref, zs_ref, ypg_ref, sga_ref, sgb_ref)
    xcs, u = _mixin_core(h, w_ref, cw_ref, alog_ref, dtb_ref, pw_ref, ps_ref, outs, pos, conv_tap, pool_sums, store)
    for i in range(CONV_W - 1):
        src = steps + i - (CONV_W - 1)
        for seg in range(3):
            ls = slice(seg * D_MODEL, (seg + 1) * D_MODEL)
            cnew_ref[i, :, ls] = (xcs[seg][src * bb:(src + 1) * bb, :] if src >= 0
                                  else chist_ref[CONV_W - 1 + src, :, ls])
    for i in range(POOL_HIST):
        src = steps + i - POOL_HIST
        pnew_ref[i] = u[src * bb:(src + 1) * bb, :] if src >= 0 else phist_ref[POOL_HIST + src]


def _mixin_sample(li, x, nm, w_big, cw, alog, dtb, pw, ps, chist, phist, *, bb, pos0):
    steps, nb, _ = x.shape
    slab_spec = lambda n, c: pl.BlockSpec((n, bb, c), lambda i: (0, i, 0))
    in_specs = ([slab_spec(steps, D_MODEL)] + _mixin_weight_specs(li)
                + [slab_spec(CONV_W - 1, CONV_CH), slab_spec(POOL_HIST, D_MODEL)])
    big = jax.ShapeDtypeStruct((steps, nb, D_MODEL), F32)
    small = jax.ShapeDtypeStruct((steps, nb, 128), F32)
    gate = jax.ShapeDtypeStruct((steps, nb, D_MODEL), BF16)
    out_shape = [big, big, big, small, small, gate, gate, gate, gate,
                 jax.ShapeDtypeStruct(chist.shape, F32), jax.ShapeDtypeStruct(phist.shape, F32)]
    out_specs = ([slab_spec(steps, D_MODEL)] * 3 + [slab_spec(steps, 128)] * 2 + [slab_spec(steps, D_MODEL)] * 4
                 + [slab_spec(CONV_W - 1, CONV_CH), slab_spec(POOL_HIST, D_MODEL)])
    return pl.pallas_call(
        functools.partial(_mixin_sample_body, steps=steps, bb=bb, pos0=pos0),
        grid=(nb // bb,), in_specs=in_specs, out_specs=out_specs, out_shape=out_shape,
        compiler_params=pltpu.CompilerParams(dimension_semantics=("arbitrary",), vmem_limit_bytes=VMEM_LIMIT),
        name="mixin_sample",
    )(x, nm, w_big, cw, alog, dtb, pw, ps, chist, phist)


def _delta_body(q_ref, k_ref, v_ref, g_ref, b_ref, o_ref, sout_ref, s_ref, *, tile, nseq):
    t = pl.program_id(1)

    @pl.when(t == 0)
    def _():
        s_ref[...] = jnp.zeros(s_ref.shape, F32)

    r = lax.broadcasted_iota(jnp.int32, (BLOCK, BLOCK), 0)
    c = lax.broadcasted_iota(jnp.int32, (BLOCK, BLOCK), 1)
    same = (r // CHUNK) == (c // CHUNK)
    low = same & (r >= c)
    strict = same & (r > c)
    eye = jnp.where(r == c, 1.0, 0.0).astype(F32)
    l_blk = jnp.where(low, 1.0, 0.0).astype(BF16)
    ones0 = jnp.where(c < CHUNK, 1.0, 0.0).astype(BF16)
    ones1 = jnp.where(c >= CHUNK, 1.0, 0.0).astype(BF16)
    first = lax.broadcasted_iota(jnp.int32, (BLOCK, 1), 0) < CHUNK
    zeros_half = jnp.zeros((CHUNK, HEAD_DIM), F32)
    cat = jnp.concatenate
    col = lambda a, hh: a[:, hh:hh + 1]
    row = lambda a, hh: a[hh:hh + 1, :]
    square = (BLOCK, BLOCK)
    units = [(sq, hh) for sq in range(nseq) for hh in range(N_HEADS)]
    n_units = range(len(units))
    ls = [slice(hh * HEAD_DIM, (hh + 1) * HEAD_DIM) for hh in range(N_HEADS)]

    def block(i, carry):
        rows = pl.ds(pl.multiple_of(i * BLOCK, BLOCK), BLOCK)
        gp = [g_ref[sq, rows, :] for sq in range(nseq)]
        bt = [b_ref[sq, rows, :] for sq in range(nseq)]
        g_cum = [_mm_exact_rhs(l_blk, x) for x in gp]
        tot0 = [_mm_exact_rhs(ones0, x) for x in gp]
        tot1 = [_mm_exact_rhs(ones1, x) for x in gp]
        g_t = [x.T for x in g_cum]
        tot_t = ([x.T for x in tot0], [x.T for x in tot1])

        def operands(sq, hh):
            kh = k_ref[sq, rows, ls[hh]]
            qh = q_ref[sq, rows, ls[hh]]
            beta = jnp.broadcast_to(col(bt[sq], hh), square)
            gcol = jnp.broadcast_to(col(g_cum[sq], hh), square)
            decay = jnp.exp(jnp.where(low, gcol - row(g_t[sq], hh), -jnp.inf))
            eg = jnp.exp(gcol)
            e_tail = jnp.exp(jnp.where(first, row(tot_t[0][sq], hh), row(tot_t[1][sq], hh)) - gcol)
            kb = kh * beta
            return dict(
                kbq=cat([kb, qh], axis=0).astype(BF16), k=kh.astype(BF16), decay=decay,
                rhs=cat([v_ref[sq, rows, ls[hh]] * beta, kb * eg], axis=1).astype(BF16),
                qd=(qh * eg).astype(BF16), kt_t=(kh * e_tail).T.astype(BF16))

        ops = [operands(sq, hh) for sq, hh in units]
        e_tot = [[jnp.exp(row(tot_t[half][sq], hh)) for sq, hh in units] for half in range(2)]
        kq = [_dot(ops[n]["kbq"], ops[n]["k"], nt=True) for n in n_units]
        p = [-jnp.where(strict, kq[n][:BLOCK] * ops[n]["decay"], 0.0) for n in n_units]
        qk = [(kq[n][BLOCK:] * ops[n]["decay"]).astype(BF16) for n in n_units]
        tinv = [eye + p[n] for n in n_units]
        p = [_mm(p[n], p[n]) for n in n_units]
        for _ in range(4):
            pp = [_mm(p[n], cat([p[n], tinv[n]], axis=1)) for n in n_units]
            p = [pp[n][:, :BLOCK] for n in n_units]
            tinv = [tinv[n] + pp[n][:, BLOCK:] for n in n_units]
        tinv = [tinv[n] + _mm(p[n], tinv[n]) for n in n_units]
        sol = [_mm(tinv[n], ops[n]["rhs"]) for n in n_units]
        wv = [sol[n][:, :HEAD_DIM] for n in n_units]
        wk = [sol[n][:, HEAD_DIM:].astype(BF16) for n in n_units]
        s_cur = [s_ref[sq, hh] for sq, hh in units]
        o_parts = []
        for half in range(2):
            hs = slice(half * CHUNK, (half + 1) * CHUNK)
            res = [_mm(cat([wk[n][hs], ops[n]["qd"][hs]], axis=0), s_cur[n]) for n in n_units]
            u_new = [wv[n][hs] - res[n][:CHUNK] for n in n_units]
            u_pad = [cat([u_new[n], zeros_half] if half == 0 else [zeros_half, u_new[n]], axis=0) for n in n_units]
            upd = [_mm(cat([qk[n][hs], ops[n]["kt_t"]], axis=0), u_pad[n]) for n in n_units]
            o_parts.append([res[n][CHUNK:] + upd[n][:CHUNK] for n in n_units])
            s_cur = [s_cur[n] * e_tot[half][n] + upd[n][CHUNK:] for n in n_units]
        for n, (sq, hh) in enumerate(units):
            s_ref[sq, hh] = s_cur[n]
            o_ref[sq, rows, ls[hh]] = cat([o_parts[0][n], o_parts[1][n]], axis=0)
        return carry

    lax.fori_loop(0, tile // BLOCK, block, 0)
    sout_ref[...] = s_ref[...]


def _delta_prompt(q, k, v, g, beta, *, batch, seq, tile, nseq):
    as3d = lambda a: a.reshape(batch, seq, a.shape[-1])
    row_spec = lambda c: pl.BlockSpec((nseq, tile, c), lambda b, t: (b, t, 0))
    s_shape = (N_HEADS, HEAD_DIM, HEAD_DIM)
    o, s_new = pl.pallas_call(
        functools.partial(_delta_body, tile=tile, nseq=nseq),
        grid=(batch // nseq, seq // tile),
        in_specs=[row_spec(D_MODEL)] * 3 + [row_spec(128)] * 2,
        out_specs=[row_spec(D_MODEL), pl.BlockSpec((nseq,) + s_shape, lambda b, t: (b, 0, 0, 0))],
        out_shape=[jax.ShapeDtypeStruct((batch, seq, D_MODEL), F32), jax.ShapeDtypeStruct((batch,) + s_shape, F32)],
        scratch_shapes=[pltpu.VMEM((nseq,) + s_shape, F32)],
        compiler_params=pltpu.CompilerParams(dimension_semantics=("arbitrary", "arbitrary"),
                                             vmem_limit_bytes=VMEM_LIMIT),
        name="delta_prompt",
    )(as3d(q), as3d(k), as3d(v), as3d(g), as3d(beta))
    return o.reshape(batch * seq, D_MODEL), s_new


def _head_indicators():
    d = lax.broadcasted_iota(jnp.int32, (D_MODEL, 128), 0) // HEAD_DIM
    hcol = lax.broadcasted_iota(jnp.int32, (D_MODEL, 128), 1)
    e_sum = jnp.where(d == hcol, 1.0, 0.0).astype(BF16)
    hrow = lax.broadcasted_iota(jnp.int32, (128, D_MODEL), 0)
    d2 = lax.broadcasted_iota(jnp.int32, (128, D_MODEL), 1) // HEAD_DIM
    e_bc = jnp.where(hrow == d2, 1.0, 0.0).astype(BF16)
    return e_sum, e_bc


def _sample_prep_body(q_ref, k_ref, v_ref, g_ref, b_ref, wkqd_ref, wv_ref, kt_ref, qkd_ref, glx_ref,
                      *, steps, nb):
    e_sum, e_bc = _head_indicators()
    expand = lambda x: _mm_exact_lhs(x, e_bc)
    hsum = lambda y: _mm_exact_lhs(y, e_sum)
    sl = lambda ref, i: ref[i * nb:(i + 1) * nb, :]
    q = [sl(q_ref, i) for i in range(steps)]
    k = [sl(k_ref, i) for i in range(steps)]
    v = [sl(v_ref, i) for i in range(steps)]
    beta = [sl(b_ref, i) for i in range(steps)]
    g_cum = []
    for i in range(steps):
        gi = sl(g_ref, i)
        g_cum.append(gi if i == 0 else g_cum[-1] + gi)
    wv, wk = [], []
    for i in range(steps):
        acc_v = v[i] * expand(beta[i])
        acc_k = k[i] * expand(beta[i] * jnp.exp(g_cum[i]))
        for j in range(i):
            a_ij = expand(hsum(k[i] * k[j]) * jnp.exp(g_cum[i] - g_cum[j]) * beta[i])
            acc_v = acc_v - a_ij * wv[j]
            acc_k = acc_k - a_ij * wk[j]
        wv.append(acc_v)
        wk.append(acc_k)
    zeros = jnp.zeros((nb, D_MODEL), F32)
    for i in range(steps):
        wkqd_ref[i * nb:(i + 1) * nb, :] = wk[i]
        wkqd_ref[(steps + i) * nb:(steps + i + 1) * nb, :] = q[i] * expand(jnp.exp(g_cum[i]))
        wv_ref[i * nb:(i + 1) * nb, :] = wv[i]
        wv_ref[(steps + i) * nb:(steps + i + 1) * nb, :] = zeros
        kt_ref[i * nb:(i + 1) * nb, :] = k[i] * expand(jnp.exp(g_cum[steps - 1] - g_cum[i]))
        kt_ref[(steps + i) * nb:(steps + i + 1) * nb, :] = zeros
        for j in range(steps):
            idx = i * steps + j
            if j <= i:
                qkd_ref[idx * nb:(idx + 1) * nb, :] = hsum(q[i] * k[j]) * jnp.exp(g_cum[i] - g_cum[j])
            else:
                qkd_ref[idx * nb:(idx + 1) * nb, :] = jnp.zeros((nb, 128), F32)
    glx_ref[...] = expand(jnp.exp(g_cum[steps - 1]))


def _sample_prep(q, k, v, g, beta, *, steps, nb):
    m = steps * nb
    out_shape = [jax.ShapeDtypeStruct((2 * m, D_MODEL), F32)] * 3 + [
        jax.ShapeDtypeStruct((steps * steps * nb, 128), F32), jax.ShapeDtypeStruct((nb, D_MODEL), F32)]
    return pl.pallas_call(
        functools.partial(_sample_prep_body, steps=steps, nb=nb),
        grid=(1,),
        in_specs=[_full((m, D_MODEL))] * 3 + [_full((m, 128))] * 2,
        out_specs=[_full(s.shape) for s in out_shape],
        out_shape=out_shape,
        compiler_params=pltpu.CompilerParams(dimension_semantics=("arbitrary",), vmem_limit_bytes=VMEM_LIMIT),
        name="sample_prep",
    )(q, k, v, g, beta)


def _sample_state_body(*refs, bb, group):
    s_ref, wkqd_ref, wv_ref, kt_ref, glx_ref = refs[:5]
    r_ref, snew_ref = refs[-2:]
    zeros_pad = jnp.zeros((HEAD_DIM - 8, HEAD_DIM), F32)

    def per_group(gi, carry):
        units = [(gi * group + j, hh) for j in range(group) for hh in range(N_HEADS)]
        ls = [slice(hh * HEAD_DIM, (hh + 1) * HEAD_DIM) for hh in range(N_HEADS)]
        pad = lambda a, n: jnp.concatenate([a, zeros_pad[:n - 8]], axis=0)
        s0 = [s_ref[b, hh] for b, hh in units]
        res = [_mm(pad(wkqd_ref[b, :, ls[hh]], 16), s0[n])[:8] for n, (b, hh) in enumerate(units)]
        kt_t = [pad(kt_ref[b, :, ls[hh]], HEAD_DIM).T for b, hh in units]
        upd = [_mm(kt_t[n], pad(wv_ref[b, :, ls[hh]] - res[n], HEAD_DIM)) for n, (b, hh) in enumerate(units)]
        for n, (b, hh) in enumerate(units):
            r_ref[b, :, ls[hh]] = res[n]
            snew_ref[b, hh] = s0[n] * glx_ref[b, :, ls[hh]] + upd[n]
        return carry

    lax.fori_loop(0, bb // group, per_group, 0)


def _sample_state(li, state, wkqd, wv, kt, glx, prev, *, bb):
    nb = state.shape[1]
    s_spec = pl.BlockSpec((None, bb, N_HEADS, HEAD_DIM, HEAD_DIM), lambda i: (li, i, 0, 0, 0))
    slot_spec = pl.BlockSpec((bb, 8, D_MODEL), lambda i: (i, 0, 0))
    in_specs = [s_spec, slot_spec, slot_spec, slot_spec, pl.BlockSpec((bb, 1, D_MODEL), lambda i: (i, 0, 0))]
    args = [state, wkqd, wv, kt, glx.reshape(nb, 1, D_MODEL)]
    aliases = {}
    if prev is not None:
        in_specs.append(pl.BlockSpec(memory_space=pl.ANY))
        args.append(prev)
        aliases = {len(args) - 1: 1}
    return pl.pallas_call(
        functools.partial(_sample_state_body, bb=bb, group=2),
        grid=(nb // bb,),
        in_specs=in_specs,
        out_specs=[slot_spec, s_spec],
        out_shape=[jax.ShapeDtypeStruct((nb, 8, D_MODEL), F32), jax.ShapeDtypeStruct(state.shape, F32)],
        input_output_aliases=aliases,
        compiler_params=pltpu.CompilerParams(dimension_semantics=("arbitrary",), vmem_limit_bytes=VMEM_LIMIT),
        name="sample_state",
    )(*args)


def _sample_out_body(r_ref, wv_ref, qkd_ref, o_ref, *, steps, nb):
    _, e_bc = _head_indicators()
    sl = lambda ref, i: ref[i * nb:(i + 1) * nb, :]
    u = [sl(wv_ref, j) - sl(r_ref, j) for j in range(steps)]
    for i in range(steps):
        acc = sl(r_ref, steps + i)
        for j in range(i + 1):
            acc = acc + _mm_exact_lhs(sl(qkd_ref, i * steps + j), e_bc) * u[j]
        o_ref[i * nb:(i + 1) * nb, :] = acc


def _sample_out(r, wv, qkd, *, steps, nb):
    m = steps * nb
    return pl.pallas_call(
        functools.partial(_sample_out_body, steps=steps, nb=nb),
        grid=(1,),
        in_specs=[_full(r.shape), _full(wv.shape), _full(qkd.shape)],
        out_specs=_full((m, D_MODEL)),
        out_shape=jax.ShapeDtypeStruct((m, D_MODEL), F32),
        compiler_params=pltpu.CompilerParams(dimension_semantics=("arbitrary",), vmem_limit_bytes=VMEM_LIMIT),
        name="sample_out",
    )(r, wv, qkd)


def _merge_body(o_ref, zs_ref, ypg_ref, sga_ref, sgb_ref, x_ref, p_ref, gn_ref, wpa_ref, wpb_ref, wout_ref,
                npl_ref, wpg_ref, wpp_ref, fn_ref, y_ref, *, final, parts):
    step = o_ref.shape[0] // parts
    groups = [slice(i * step, (i + 1) * step) for i in range(parts)]
    gn = gn_ref[...]
    y_a = []
    for rs in groups:
        gated = []
        for hh in range(N_HEADS):
            ls = slice(hh * HEAD_DIM, (hh + 1) * HEAD_DIM)
            oh = o_ref[rs, ls]
            on = oh * lax.rsqrt(jnp.mean(oh * oh, axis=-1, keepdims=True) + EPS) * gn
            gated.append((on * zs_ref[rs, ls]).astype(BF16))
        y_a.append(_dot(jnp.concatenate(gated, axis=1), wpa_ref[...]))
    y_b = [_mm(ypg_ref[rs, :], wpb_ref[...]) for rs in groups]
    m = [sga_ref[rs, :] * y_a[i] + sgb_ref[rs, :] * y_b[i] for i, rs in enumerate(groups)]
    x1 = [x_ref[rs, :] + _mm(m[i], wout_ref[...]) for i, rs in enumerate(groups)]
    gate = [jax.nn.sigmoid(_mm(_rmsnorm(x1[i], npl_ref[...]), wpg_ref[...])) for i in range(parts)]
    pe = [_mm(p_ref[rs, :], wpp_ref[...]) for rs in groups]
    for i, rs in enumerate(groups):
        x2 = x1[i] + gate[i] * pe[i]
        if final:
            x2 = _rmsnorm(x2, fn_ref[...])
        y_ref[rs, :] = x2


def _merge(li, o, zs, ypg, sga, sgb, x, p, gn, wpa, wpb, wout, npl, wpg, wpp, fn, *, tile, final):
    m = x.shape[0]
    ple = p.shape[-1]
    row_spec = lambda c: pl.BlockSpec((tile, c), lambda i: (i, 0))
    sq = _layer(li, (D_MODEL, D_MODEL))
    return pl.pallas_call(
        functools.partial(_merge_body, final=final, parts=2),
        grid=(m // tile,),
        in_specs=[row_spec(D_MODEL)] * 6 + [pl.BlockSpec((None, tile, ple), lambda i: (li, i, 0)),
                                            _layer(li, (1, HEAD_DIM)), sq, sq, sq, _layer(li, (1, D_MODEL)),
                                            sq, _layer(li, (ple, D_MODEL)), _full((1, D_MODEL))],
        out_specs=row_spec(D_MODEL),
        out_shape=jax.ShapeDtypeStruct((m, D_MODEL), F32),
        compiler_params=pltpu.CompilerParams(dimension_semantics=("arbitrary",), vmem_limit_bytes=VMEM_LIMIT),
        name="merge",
    )(o, zs, ypg, sga, sgb, x, p, gn, wpa, wpb, wout, npl, wpg, wpp, fn)


def kernel(x_prompt, x_sample, p_prompt, p_sample, state_conv, state_delta, state_pool, norm_mix, w_in, conv_w,
           a_log, dt_bias, gdn_norm, w_proj_a, pool_w, pool_scale, w_proj_b, w_out, norm_ple, w_ple_gate,
           w_ple_proj, final_norm):
    depth = w_in.shape[0]
    batch, seq, _ = x_prompt.shape
    nb, steps, _ = x_sample.shape
    rowvec = lambda a: a.reshape(depth, 1, -1)
    pad128 = lambda a: jnp.pad(a, ((0, 0), (0, 128 - a.shape[1]))).reshape(depth, 1, 128)
    mix_w = (rowvec(norm_mix), _wprep(w_in, tile=256), conv_w, pad128(a_log), pad128(dt_bias),
             pool_w.astype(BF16), rowvec(pool_scale))
    merge_w = (rowvec(gdn_norm), w_proj_a.astype(BF16), w_proj_b.astype(BF16), w_out.astype(BF16),
               rowvec(norm_ple), w_ple_gate.astype(BF16), w_ple_proj.astype(BF16), final_norm.reshape(1, -1))

    xp = x_prompt.reshape(batch * seq, D_MODEL)
    pp = p_prompt.reshape(depth, batch * seq, -1)
    conv_p, delta_p, pool_p = [], [], []
    for li in range(depth):
        q, k, v, g, beta, zs, ypg, sga, sgb, cnew, pnew = _mixin_prompt(li, xp, *mix_w, batch=batch, seq=seq, tile=256)
        o, s_new = _delta_prompt(q, k, v, g, beta, batch=batch, seq=seq, tile=256, nseq=4)
        xp = _merge(li, o, zs, ypg, sga, sgb, xp, pp, *merge_w, tile=512, final=(li == depth - 1))
        conv_p.append(cnew)
        delta_p.append(s_new)
        pool_p.append(pnew)
    y_prompt = xp.reshape(batch, seq, D_MODEL)

    swap = lambda a: jnp.swapaxes(a, 0, 1)
    flat = lambda a: a.reshape(-1, a.shape[-1])
    bmajor = lambda a: swap(a.reshape(-1, nb, a.shape[-1]))
    xs = swap(x_sample)
    ps = jnp.swapaxes(p_sample, 1, 2).reshape(depth, steps * nb, -1)
    conv_s, delta_s, pool_s = [], None, []
    for li in range(depth):
        q, k, v, g, beta, zs, ypg, sga, sgb, cnew, pnew = _mixin_sample(
            li, xs, *mix_w, swap(state_conv[li]), swap(state_pool[li]), bb=32, pos0=PAST_LEN)
        wkqd, wv, kt, qkd, glx = _sample_prep(flat(q), flat(k), flat(v), flat(g), flat(beta), steps=steps, nb=nb)
        r, delta_s = _sample_state(li, state_delta, bmajor(wkqd), bmajor(wv), bmajor(kt), glx, delta_s, bb=8)
        o = _sample_out(flat(swap(r)), wv, qkd, steps=steps, nb=nb)
        xs = _merge(li, o, flat(zs), flat(ypg), flat(sga), flat(sgb), flat(xs), ps, *merge_w, tile=steps * nb,
                    final=(li == depth - 1)).reshape(steps, nb, D_MODEL)
        conv_s.append(swap(cnew))
        pool_s.append(swap(pnew))
    y_sample = swap(xs)

    return (y_prompt, y_sample, jnp.stack(conv_p), jnp.stack(delta_p), jnp.stack(pool_p),
            jnp.stack(conv_s), delta_s, jnp.stack(pool_s))
```

```python
import functools

import jax
import jax.numpy as jnp
from jax import lax
from jax.experimental import pallas as pl
from jax.experimental.pallas import tpu as pltpu

F32 = jnp.float32
BF16 = jnp.bfloat16

D_MODEL = 1024
N_HEADS = 8
HEAD_DIM = 128
CONV_W = 4
CONV_CH = 3 * D_MODEL
POOL_WINDOWS = (2, 4, 8, 16)
POOL_HIST = 15
POOL_GC = 256
EPS = 1e-6
PAST_LEN = 16384
CHUNK = 64
BLOCK = 2 * CHUNK
VMEM_LIMIT = 56 * 1024 * 1024

_SPLIT = 4096
_C_QKV, _C_Z = 0, 3072
_C_U, _C_GP, _C_GA, _C_GB, _C_A, _C_B = 0, 1024, 2048, 3072, 4096, 4224
_W2_COLS = 4352


def _dot(a, b, nt=False):
    dims = (((1,), (1,)), ((), ())) if nt else (((1,), (0,)), ((), ()))
    return lax.dot_general(a, b, dims, preferred_element_type=F32)


def _mm(a, b, nt=False):
    return _dot(a.astype(BF16), b.astype(BF16), nt)


def _split3(x):
    x0 = x.astype(BF16)
    r = x - x0.astype(F32)
    x1 = r.astype(BF16)
    x2 = (r - x1.astype(F32)).astype(BF16)
    return x0, x1, x2


def _mm_exact_rhs(a_bf, b):
    b0, b1, b2 = _split3(b)
    return _dot(a_bf, b0) + _dot(a_bf, b1) + _dot(a_bf, b2)


def _mm_exact_lhs(a, b_bf):
    a0, a1, a2 = _split3(a)
    return _dot(a0, b_bf) + _dot(a1, b_bf) + _dot(a2, b_bf)


def _silu(x):
    return x * jax.nn.sigmoid(x)


def _rmsnorm(x, w):
    return x * lax.rsqrt(jnp.mean(x * x, axis=-1, keepdims=True) + EPS) * w


def _full(shape):
    n = len(shape)
    return pl.BlockSpec(shape, lambda *_: (0,) * n, pipeline_mode=pl.Buffered(1))


def _layer(li, shape):
    n = len(shape)
    return pl.BlockSpec((None,) + tuple(shape), lambda *_: (li,) + (0,) * n, pipeline_mode=pl.Buffered(1))


def _wprep_body(wt_ref, oa_ref, ob_ref):
    n_tail = wt_ref.shape[0] - _SPLIT - 2 * N_HEADS
    for g in range(_SPLIT // 128):
        oa_ref[:, g * 128:(g + 1) * 128] = wt_ref[g * 128:(g + 1) * 128, :].T.astype(BF16)
    for g in range(n_tail // 128):
        r0 = _SPLIT + 2 * N_HEADS + g * 128
        ob_ref[:, g * 128:(g + 1) * 128] = wt_ref[r0:r0 + 128, :].T.astype(BF16)
    ab = wt_ref[_SPLIT:_SPLIT + 128, :].T
    lane = lax.broadcasted_iota(jnp.int32, ab.shape, 1)
    ob_ref[:, _C_A:_C_A + 128] = jnp.where(lane < N_HEADS, ab, 0.0).astype(BF16)
    ob_ref[:, _C_B:_C_B + 128] = jnp.where(lane < N_HEADS, pltpu.roll(ab, 128 - N_HEADS, 1), 0.0).astype(BF16)


def _wprep(w_in, *, tk):
    depth, k, n = w_in.shape
    return pl.pallas_call(
        _wprep_body, grid=(depth, k // tk),
        in_specs=[pl.BlockSpec((None, n, tk), lambda l, r: (l, 0, r))],
        out_specs=[pl.BlockSpec((None, tk, _SPLIT), lambda l, r: (l, r, 0)),
                   pl.BlockSpec((None, tk, _W2_COLS), lambda l, r: (l, r, 0))],
        out_shape=[jax.ShapeDtypeStruct((depth, k, _SPLIT), BF16), jax.ShapeDtypeStruct((depth, k, _W2_COLS), BF16)],
        compiler_params=pltpu.CompilerParams(dimension_semantics=("arbitrary", "arbitrary"),
                                             vmem_limit_bytes=VMEM_LIMIT),
        name="wprep",
    )(jnp.swapaxes(w_in, 1, 2))


def _mixin_core(h, w_ref, w2_ref, cw_ref, alog_ref, dtb_ref, pw_ref, ps_ref, outs, pos, conv_tap, pool_sums, store):
    q_ref, k_ref, v_ref, g_ref, b_ref, zs_ref, ypg_ref, sga_ref, sgb_ref = outs
    proj = lambda ref, c0, n=D_MODEL: _dot(h, ref[:, c0:c0 + n])
    xcs = [proj(w_ref, _C_QKV + seg * D_MODEL) for seg in range(3)]
    gates = [proj(w_ref, _C_Z), proj(w2_ref, _C_GA), proj(w2_ref, _C_GB)]
    a_r = proj(w2_ref, _C_A, 128)
    b_r = proj(w2_ref, _C_B, 128)
    u = proj(w2_ref, _C_U)
    gp = proj(w2_ref, _C_GP)

    for seg, out_ref in enumerate((q_ref, k_ref, v_ref)):
        c0 = seg * D_MODEL
        xc = xcs[seg]
        acc = xc * cw_ref[CONV_W - 1:CONV_W, c0:c0 + D_MODEL]
        for j in range(CONV_W - 1):
            acc = acc + conv_tap(j, c0, xc) * cw_ref[j:j + 1, c0:c0 + D_MODEL]
        y = _silu(acc)
        if seg < 2:
            for hh in range(N_HEADS):
                ls = slice(hh * HEAD_DIM, (hh + 1) * HEAD_DIM)
                yh = y[:, ls]
                inv = lax.rsqrt(jnp.sum(yh * yh, axis=-1, keepdims=True) + EPS)
                if seg == 0:
                    inv = inv * (HEAD_DIM ** -0.5)
                store(out_ref, yh * inv, ls)
        else:
            store(out_ref, y, None)

    store(zs_ref, _silu(gates[0]), None)
    store(sga_ref, jax.nn.sigmoid(gates[1]), None)
    store(sgb_ref, jax.nn.sigmoid(gates[2]), None)
    xs = a_r + dtb_ref[...]
    softplus = jnp.maximum(xs, 0.0) + jnp.log1p(jnp.exp(-jnp.abs(xs)))
    store(g_ref, -jnp.exp(alog_ref[...]) * softplus, None)
    store(b_ref, jax.nn.sigmoid(b_r), None)

    sgp = _silu(gp)
    sums = pool_sums(u)
    for gi, win in enumerate(POOL_WINDOWS):
        l0 = gi * POOL_GC
        ls = slice(l0, l0 + POOL_GC)
        count = jnp.minimum(pos + 1, win).astype(F32)
        y = sums[gi] / count - u[:, ls]
        yp = _mm(y, pw_ref[gi]) * ps_ref[:, ls]
        store(ypg_ref, yp * sgp[:, ls], ls)
    return xcs, u


def _mixin_weight_specs(li):
    return [_layer(li, (1, D_MODEL)), _layer(li, (D_MODEL, _SPLIT)), _layer(li, (D_MODEL, _W2_COLS)), _layer(li, (CONV_W, CONV_CH)),
            _layer(li, (1, 128)), _layer(li, (1, 128)), _layer(li, (4, POOL_GC, POOL_GC)), _layer(li, (1, D_MODEL))]


def _mixin_prompt_body(x_ref, nm_ref, w_ref, w2_ref, cw_ref, alog_ref, dtb_ref, pw_ref, ps_ref,
                       q_ref, k_ref, v_ref, g_ref, b_ref, zs_ref, ypg_ref, sga_ref, sgb_ref, cnew_ref, pnew_ref,
                       cext, pext, *, rows):
    hc, hp = 8, 16
    t = pl.program_id(1)

    @pl.when(t == 0)
    def _():
        cext[0:hc, :] = jnp.zeros((hc, CONV_CH), F32)
        pext[0:hp, :] = jnp.zeros((hp, D_MODEL), F32)

    h = _rmsnorm(x_ref[...], nm_ref[...]).astype(BF16)

    def conv_tap(j, c0, xc):
        if j == 0:
            cext[hc:hc + rows, c0:c0 + D_MODEL] = xc
        off = hc - (CONV_W - 1 - j)
        return cext[off:off + rows, c0:c0 + D_MODEL]

    def pool_sums(u):
        pext[hp:hp + rows, :] = u
        lvl = pext[...]
        out = []
        for gi, win in enumerate(POOL_WINDOWS):
            lvl = lvl + pltpu.roll(lvl, win // 2, 0)
            out.append(lvl[hp:, :POOL_GC])
            lvl = lvl[:, POOL_GC:]
        return out

    def store(ref, val, ls):
        if ls is None:
            ref[...] = val.astype(ref.dtype)
        else:
            ref[:, ls] = val.astype(ref.dtype)

    pos = t * rows + lax.broadcasted_iota(jnp.int32, (rows, 1), 0)
    outs = (q_ref, k_ref, v_ref, g_ref, b_ref, zs_ref, ypg_ref, sga_ref, sgb_ref)
    _mixin_core(h, w_ref, w2_ref, cw_ref, alog_ref, dtb_ref, pw_ref, ps_ref, outs, pos, conv_tap, pool_sums, store)
    cnew_ref[...] = cext[hc + rows - (CONV_W - 1):hc + rows, :]
    pnew_ref[...] = pext[hp + rows - POOL_HIST:hp + rows, :]
    cext[0:hc, :] = cext[rows:rows + hc, :]
    pext[0:hp, :] = pext[rows:rows + hp, :]


def _mixin_prompt(li, x, nm, w_a, w_b, cw, alog, dtb, pw, ps, *, batch, seq, tile):
    m = x.shape[0]
    nt = seq // tile
    row_spec = lambda c: pl.BlockSpec((tile, c), lambda b, t: (b * nt + t, 0))
    in_specs = [row_spec(D_MODEL)] + _mixin_weight_specs(li)
    cnew_spec = pl.BlockSpec((None, CONV_W - 1, CONV_CH), lambda b, t: (b, 0, 0))
    pnew_spec = pl.BlockSpec((None, POOL_HIST, D_MODEL), lambda b, t: (b, 0, 0))
    big = jax.ShapeDtypeStruct((m, D_MODEL), F32)
    small = jax.ShapeDtypeStruct((m, 128), F32)
    gate = jax.ShapeDtypeStruct((m, D_MODEL), BF16)
    out_shape = [big, big, big, small, small, gate, gate, gate, gate,
                 jax.ShapeDtypeStruct((batch, CONV_W - 1, CONV_CH), F32),
                 jax.ShapeDtypeStruct((batch, POOL_HIST, D_MODEL), F32)]
    out_specs = [row_spec(D_MODEL)] * 3 + [row_spec(128)] * 2 + [row_spec(D_MODEL)] * 4 + [cnew_spec, pnew_spec]
    return pl.pallas_call(
        functools.partial(_mixin_prompt_body, rows=tile),
        grid=(batch, nt), in_specs=in_specs, out_specs=out_specs, out_shape=out_shape,
        scratch_shapes=[pltpu.VMEM((8 + tile, CONV_CH), F32), pltpu.VMEM((16 + tile, D_MODEL), F32)],
        compiler_params=pltpu.CompilerParams(dimension_semantics=("arbitrary", "arbitrary"),
                                             vmem_limit_bytes=VMEM_LIMIT),
        name="mixin_prompt",
    )(x, nm, w_a, w_b, cw, alog, dtb, pw, ps)


def _mixin_sample_body(x_ref, nm_ref, w_ref, w2_ref, cw_ref, alog_ref, dtb_ref, pw_ref, ps_ref, chist_ref, phist_ref,
                       q_ref, k_ref, v_ref, g_ref, b_ref, zs_ref, ypg_ref, sga_ref, sgb_ref, cnew_ref, pnew_ref,
                       *, steps, bb, pos0):
    rows = steps * bb
    h = _rmsnorm(x_ref[...].reshape(rows, D_MODEL), nm_ref[...]).astype(BF16)

    def delayed(new, hist_ref, nhist, d, ls_new, ls_hist):
        parts = []
        for t in range(steps):
            src = t - d
            parts.append(new[src * bb:(src + 1) * bb, ls_new] if src >= 0 else hist_ref[nhist + src, :, ls_hist])
        return jnp.concatenate(parts, axis=0)

    def conv_tap(j, c0, xc):
        return delayed(xc, chist_ref, CONV_W - 1, CONV_W - 1 - j, slice(None), slice(c0, c0 + D_MODEL))

    def pool_sums(u):
        out = []
        for gi, win in enumerate(POOL_WINDOWS):
            ls = slice(gi * POOL_GC, (gi + 1) * POOL_GC)
            acc = u[:, ls]
            for s in range(1, win):
                acc = acc + delayed(u, phist_ref, POOL_HIST, s, ls, ls)
            out.append(acc)
        return out

    def store(ref, val, ls):
        val = val.reshape(steps, bb, val.shape[-1]).astype(ref.dtype)
        if ls is None:
            ref[...] = val
        else:
            ref[:, :, ls] = val

    pos = pos0 + lax.broadcasted_iota(jnp.int32, (rows, 1), 0) // bb
    outs = (q_ref, k_ref, v_ref, g_ref, b_ref, zs_ref, ypg_ref, sga_ref, sgb_ref)
    xcs, u = _mixin_core(h, w_ref, w2_ref, cw_ref, alog_ref, dtb_ref, pw_ref, ps_ref, outs, pos, conv_tap, pool_sums, store)
    for i in range(CONV_W - 1):
        src = steps + i - (CONV_W - 1)
        for seg in range(3):
            ls = slice(seg * D_MODEL, (seg + 1) * D_MODEL)
            cnew_ref[i, :, ls] = (xcs[seg][src * bb:(src + 1) * bb, :] if src >= 0
                                  else chist_ref[CONV_W - 1 + src, :, ls])
    for i in range(POOL_HIST):
        src = steps + i - POOL_HIST
        pnew_ref[i] = u[src * bb:(src + 1) * bb, :] if src >= 0 else phist_ref[POOL_HIST + src]


def _mixin_sample(li, x, nm, w_a, w_b, cw, alog, dtb, pw, ps, chist, phist, *, bb, pos0):
    steps, nb, _ = x.shape
    slab_spec = lambda n, c: pl.BlockSpec((n, bb, c), lambda i: (0, i, 0))
    in_specs = ([slab_spec(steps, D_MODEL)] + _mixin_weight_specs(li)
                + [slab_spec(CONV_W - 1, CONV_CH), slab_spec(POOL_HIST, D_MODEL)])
    big = jax.ShapeDtypeStruct((steps, nb, D_MODEL), F32)
    small = jax.ShapeDtypeStruct((steps, nb, 128), F32)
    gate = jax.ShapeDtypeStruct((steps, nb, D_MODEL), BF16)
    out_shape = [big, big, big, small, small, gate, gate, gate, gate,
                 jax.ShapeDtypeStruct(chist.shape, F32), jax.ShapeDtypeStruct(phist.shape, F32)]
    out_specs = ([slab_spec(steps, D_MODEL)] * 3 + [slab_spec(steps, 128)] * 2 + [slab_spec(steps, D_MODEL)] * 4
                 + [slab_spec(CONV_W - 1, CONV_CH), slab_spec(POOL_HIST, D_MODEL)])
    return pl.pallas_call(
        functools.partial(_mixin_sample_body, steps=steps, bb=bb, pos0=pos0),
        grid=(nb // bb,), in_specs=in_specs, out_specs=out_specs, out_shape=out_shape,
        compiler_params=pltpu.CompilerParams(dimension_semantics=("arbitrary",), vmem_limit_bytes=VMEM_LIMIT),
        name="mixin_sample",
    )(x, nm, w_a, w_b, cw, alog, dtb, pw, ps, chist, phist)


def _delta_body(q_ref, k_ref, v_ref, g_ref, b_ref, o_ref, sout_ref, s_ref, *, tile, nseq):
    t = pl.program_id(1)

    @pl.when(t == 0)
    def _():
        s_ref[...] = jnp.zeros(s_ref.shape, F32)

    r = lax.broadcasted_iota(jnp.int32, (BLOCK, BLOCK), 0)
    c = lax.broadcasted_iota(jnp.int32, (BLOCK, BLOCK), 1)
    same = (r // CHUNK) == (c // CHUNK)
    low = same & (r >= c)
    strict = same & (r > c)
    eye = jnp.where(r == c, 1.0, 0.0).astype(F32)
    l_blk = jnp.where(low, 1.0, 0.0).astype(BF16)
    ones0 = jnp.where(c < CHUNK, 1.0, 0.0).astype(BF16)
    ones1 = jnp.where(c >= CHUNK, 1.0, 0.0).astype(BF16)
    first = lax.broadcasted_iota(jnp.int32, (BLOCK, 1), 0) < CHUNK
    zeros_half = jnp.zeros((CHUNK, HEAD_DIM), F32)
    cat = jnp.concatenate
    col = lambda a, hh: a[:, hh:hh + 1]
    units = [(sq, hh) for sq in range(nseq) for hh in range(N_HEADS)]
    n_units = range(len(units))
    ls = [slice(hh * HEAD_DIM, (hh + 1) * HEAD_DIM) for hh in range(N_HEADS)]

    def block(i, carry):
        rows = pl.ds(pl.multiple_of(i * BLOCK, BLOCK), BLOCK)
        gp = [g_ref[sq, rows, :] for sq in range(nseq)]
        bt = [b_ref[sq, rows, :] for sq in range(nseq)]
        g_cum = [_mm_exact_rhs(l_blk, x) for x in gp]
        tot0 = [_mm_exact_rhs(ones0, x) for x in gp]
        tot1 = [_mm_exact_rhs(ones1, x) for x in gp]
        g_t = [x.T for x in g_cum]
        e_g = [jnp.exp(x) for x in g_cum]
        e_tail = [jnp.exp(jnp.where(first, tot0[sq], tot1[sq]) - g_cum[sq]) for sq in range(nseq)]
        e_tot = ([jnp.exp(x) for x in tot0], [jnp.exp(x) for x in tot1])

        kh = [k_ref[sq, rows, ls[hh]] for sq, hh in units]
        qh = [q_ref[sq, rows, ls[hh]] for sq, hh in units]
        beta = [col(bt[sq], hh) for sq, hh in units]
        eg = [col(e_g[sq], hh) for sq, hh in units]
        decay = [jnp.exp(jnp.where(low, col(g_cum[sq], hh) - g_t[sq][hh:hh + 1, :], -jnp.inf)) for sq, hh in units]
        kq = [_mm(cat([kh[n], qh[n]], axis=0), kh[n], nt=True) for n in n_units]
        p = [-jnp.where(strict, kq[n][:BLOCK] * decay[n] * beta[n], 0.0) for n in n_units]
        qk = [kq[n][BLOCK:] * decay[n] for n in n_units]
        tinv = [eye + p[n] for n in n_units]
        p = [_mm(p[n], p[n]) for n in n_units]
        for _ in range(4):
            pp = [_mm(p[n], cat([p[n], tinv[n]], axis=1)) for n in n_units]
            p = [pp[n][:, :BLOCK] for n in n_units]
            tinv = [tinv[n] + pp[n][:, BLOCK:] for n in n_units]
        tinv = [tinv[n] + _mm(p[n], tinv[n]) for n in n_units]
        sol = [_mm(tinv[n], cat([v_ref[sq, rows, ls[hh]] * beta[n], kh[n] * (beta[n] * eg[n])], axis=1))
               for n, (sq, hh) in enumerate(units)]
        wv = [sol[n][:, :HEAD_DIM] for n in n_units]
        wk = [sol[n][:, HEAD_DIM:] for n in n_units]
        qd = [qh[n] * eg[n] for n in n_units]
        kt_t = [(kh[n] * col(e_tail[sq], hh)).T for n, (sq, hh) in enumerate(units)]
        s_cur = [s_ref[sq, hh] for sq, hh in units]
        o_parts = []
        for half in range(2):
            hs = slice(half * CHUNK, (half + 1) * CHUNK)
            res = [_mm(cat([wk[n][hs], qd[n][hs]], axis=0), s_cur[n]) for n in n_units]
            u_new = [wv[n][hs] - res[n][:CHUNK] for n in n_units]
            u_pad = [cat([u_new[n], zeros_half] if half == 0 else [zeros_half, u_new[n]], axis=0) for n in n_units]
            upd = [_mm(cat([qk[n][hs], kt_t[n]], axis=0), u_pad[n]) for n in n_units]
            o_parts.append([res[n][CHUNK:] + upd[n][:CHUNK] for n in n_units])
            s_cur = [s_cur[n] * col(e_tot[half][sq], hh) + upd[n][CHUNK:] for n, (sq, hh) in enumerate(units)]
        for n, (sq, hh) in enumerate(units):
            s_ref[sq, hh] = s_cur[n]
            o_ref[sq, rows, ls[hh]] = cat([o_parts[0][n], o_parts[1][n]], axis=0)
        return carry

    lax.fori_loop(0, tile // BLOCK, block, 0)
    sout_ref[...] = s_ref[...]


def _delta_prompt(q, k, v, g, beta, *, batch, seq, tile, nseq):
    as3d = lambda a: a.reshape(batch, seq, a.shape[-1])
    row_spec = lambda c: pl.BlockSpec((nseq, tile, c), lambda b, t: (b, t, 0))
    s_shape = (N_HEADS, HEAD_DIM, HEAD_DIM)
    o, s_new = pl.pallas_call(
        functools.partial(_delta_body, tile=tile, nseq=nseq),
        grid=(batch // nseq, seq // tile),
        in_specs=[row_spec(D_MODEL)] * 3 + [row_spec(128)] * 2,
        out_specs=[row_spec(D_MODEL), pl.BlockSpec((nseq,) + s_shape, lambda b, t: (b, 0, 0, 0))],
        out_shape=[jax.ShapeDtypeStruct((batch, seq, D_MODEL), F32), jax.ShapeDtypeStruct((batch,) + s_shape, F32)],
        scratch_shapes=[pltpu.VMEM((nseq,) + s_shape, F32)],
        compiler_params=pltpu.CompilerParams(dimension_semantics=("arbitrary", "arbitrary"),
                                             vmem_limit_bytes=VMEM_LIMIT),
        name="delta_prompt",
    )(as3d(q), as3d(k), as3d(v), as3d(g), as3d(beta))
    return o.reshape(batch * seq, D_MODEL), s_new


def _head_indicators():
    d = lax.broadcasted_iota(jnp.int32, (D_MODEL, 128), 0) // HEAD_DIM
    hcol = lax.broadcasted_iota(jnp.int32, (D_MODEL, 128), 1)
    e_sum = jnp.where(d == hcol, 1.0, 0.0).astype(BF16)
    hrow = lax.broadcasted_iota(jnp.int32, (128, D_MODEL), 0)
    d2 = lax.broadcasted_iota(jnp.int32, (128, D_MODEL), 1) // HEAD_DIM
    e_bc = jnp.where(hrow == d2, 1.0, 0.0).astype(BF16)
    return e_sum, e_bc


def _sample_prep_body(q_ref, k_ref, v_ref, g_ref, b_ref, wkqd_ref, wv_ref, kt_ref, qkd_ref, glx_ref,
                      *, steps, nb):
    e_sum, e_bc = _head_indicators()
    expand = lambda x: _mm_exact_lhs(x, e_bc)
    hsum = lambda y: _mm_exact_lhs(y, e_sum)
    sl = lambda ref, i: ref[i * nb:(i + 1) * nb, :]
    q = [sl(q_ref, i) for i in range(steps)]
    k = [sl(k_ref, i) for i in range(steps)]
    v = [sl(v_ref, i) for i in range(steps)]
    beta = [sl(b_ref, i) for i in range(steps)]
    g_cum = []
    for i in range(steps):
        gi = sl(g_ref, i)
        g_cum.append(gi if i == 0 else g_cum[-1] + gi)
    wv, wk = [], []
    for i in range(steps):
        acc_v = v[i] * expand(beta[i])
        acc_k = k[i] * expand(beta[i] * jnp.exp(g_cum[i]))
        for j in range(i):
            a_ij = expand(hsum(k[i] * k[j]) * jnp.exp(g_cum[i] - g_cum[j]) * beta[i])
            acc_v = acc_v - a_ij * wv[j]
            acc_k = acc_k - a_ij * wk[j]
        wv.append(acc_v)
        wk.append(acc_k)
    zeros = jnp.zeros((nb, D_MODEL), F32)
    for i in range(steps):
        wkqd_ref[i * nb:(i + 1) * nb, :] = wk[i]
        wkqd_ref[(steps + i) * nb:(steps + i + 1) * nb, :] = q[i] * expand(jnp.exp(g_cum[i]))
        wv_ref[i * nb:(i + 1) * nb, :] = wv[i]
        wv_ref[(steps + i) * nb:(steps + i + 1) * nb, :] = zeros
        kt_ref[i * nb:(i + 1) * nb, :] = k[i] * expand(jnp.exp(g_cum[steps - 1] - g_cum[i]))
        kt_ref[(steps + i) * nb:(steps + i + 1) * nb, :] = zeros
        for j in range(steps):
            idx = i * steps + j
            if j <= i:
                qkd_ref[idx * nb:(idx + 1) * nb, :] = hsum(q[i] * k[j]) * jnp.exp(g_cum[i] - g_cum[j])
            else:
                qkd_ref[idx * nb:(idx + 1) * nb, :] = jnp.zeros((nb, 128), F32)
    glx_ref[...] = expand(jnp.exp(g_cum[steps - 1]))


def _sample_prep(q, k, v, g, beta, *, steps, nb):
    m = steps * nb
    out_shape = [jax.ShapeDtypeStruct((2 * m, D_MODEL), F32)] * 3 + [
        jax.ShapeDtypeStruct((steps * steps * nb, 128), F32), jax.ShapeDtypeStruct((nb, D_MODEL), F32)]
    return pl.pallas_call(
        functools.partial(_sample_prep_body, steps=steps, nb=nb),
        grid=(1,),
        in_specs=[_full((m, D_MODEL))] * 3 + [_full((m, 128))] * 2,
        out_specs=[_full(s.shape) for s in out_shape],
        out_shape=out_shape,
        compiler_params=pltpu.CompilerParams(dimension_semantics=("arbitrary",), vmem_limit_bytes=VMEM_LIMIT),
        name="sample_prep",
    )(q, k, v, g, beta)


def _sample_state_body(*refs, bb, group):
    s_ref, wkqd_ref, wv_ref, kt_ref, glx_ref = refs[:5]
    r_ref, snew_ref = refs[-2:]
    zeros_pad = jnp.zeros((HEAD_DIM - 8, HEAD_DIM), F32)

    def per_group(gi, carry):
        units = [(gi * group + j, hh) for j in range(group) for hh in range(N_HEADS)]
        ls = [slice(hh * HEAD_DIM, (hh + 1) * HEAD_DIM) for hh in range(N_HEADS)]
        pad = lambda a, n: jnp.concatenate([a, zeros_pad[:n - 8]], axis=0)
        s0 = [s_ref[b, hh] for b, hh in units]
        res = [_mm(pad(wkqd_ref[b, :, ls[hh]], 16), s0[n])[:8] for n, (b, hh) in enumerate(units)]
        kt_t = [pad(kt_ref[b, :, ls[hh]], HEAD_DIM).T for b, hh in units]
        upd = [_mm(kt_t[n], pad(wv_ref[b, :, ls[hh]] - res[n], HEAD_DIM)) for n, (b, hh) in enumerate(units)]
        for n, (b, hh) in enumerate(units):
            r_ref[b, :, ls[hh]] = res[n]
            snew_ref[b, hh] = s0[n] * glx_ref[b, :, ls[hh]] + upd[n]
        return carry

    lax.fori_loop(0, bb // group, per_group, 0)


def _sample_state(li, state, wkqd, wv, kt, glx, prev, *, bb):
    nb = state.shape[1]
    s_spec = pl.BlockSpec((None, bb, N_HEADS, HEAD_DIM, HEAD_DIM), lambda i: (li, i, 0, 0, 0))
    slot_spec = pl.BlockSpec((bb, 8, D_MODEL), lambda i: (i, 0, 0))
    in_specs = [s_spec, slot_spec, slot_spec, slot_spec, pl.BlockSpec((bb, 1, D_MODEL), lambda i: (i, 0, 0))]
    args = [state, wkqd, wv, kt, glx.reshape(nb, 1, D_MODEL)]
    aliases = {}
    if prev is not None:
        in_specs.append(pl.BlockSpec(memory_space=pl.ANY))
        args.append(prev)
        aliases = {len(args) - 1: 1}
    return pl.pallas_call(
        functools.partial(_sample_state_body, bb=bb, group=2),
        grid=(nb // bb,),
        in_specs=in_specs,
        out_specs=[slot_spec, s_spec],
        out_shape=[jax.ShapeDtypeStruct((nb, 8, D_MODEL), F32), jax.ShapeDtypeStruct(state.shape, F32)],
        input_output_aliases=aliases,
        compiler_params=pltpu.CompilerParams(dimension_semantics=("arbitrary",), vmem_limit_bytes=VMEM_LIMIT),
        name="sample_state",
    )(*args)


def _sample_out_body(r_ref, wv_ref, qkd_ref, o_ref, *, steps, nb):
    _, e_bc = _head_indicators()
    sl = lambda ref, i: ref[i * nb:(i + 1) * nb, :]
    u = [sl(wv_ref, j) - sl(r_ref, j) for j in range(steps)]
    for i in range(steps):
        acc = sl(r_ref, steps + i)
        for j in range(i + 1):
            acc = acc + _mm_exact_lhs(sl(qkd_ref, i * steps + j), e_bc) * u[j]
        o_ref[i * nb:(i + 1) * nb, :] = acc


def _sample_out(r, wv, qkd, *, steps, nb):
    m = steps * nb
    return pl.pallas_call(
        functools.partial(_sample_out_body, steps=steps, nb=nb),
        grid=(1,),
        in_specs=[_full(r.shape), _full(wv.shape), _full(qkd.shape)],
        out_specs=_full((m, D_MODEL)),
        out_shape=jax.ShapeDtypeStruct((m, D_MODEL), F32),
        compiler_params=pltpu.CompilerParams(dimension_semantics=("arbitrary",), vmem_limit_bytes=VMEM_LIMIT),
        name="sample_out",
    )(r, wv, qkd)


def _merge_body(o_ref, zs_ref, ypg_ref, sga_ref, sgb_ref, x_ref, p_ref, gn_ref, wpa_ref, wpb_ref, wout_ref,
                npl_ref, wpg_ref, wpp_ref, fn_ref, y_ref, *, final, parts):
    step = o_ref.shape[0] // parts
    groups = [slice(i * step, (i + 1) * step) for i in range(parts)]
    gn = gn_ref[...]
    y_a = []
    for rs in groups:
        gated = []
        for hh in range(N_HEADS):
            ls = slice(hh * HEAD_DIM, (hh + 1) * HEAD_DIM)
            oh = o_ref[rs, ls]
            on = oh * lax.rsqrt(jnp.mean(oh * oh, axis=-1, keepdims=True) + EPS) * gn
            gated.append((on * zs_ref[rs, ls]).astype(BF16))
        y_a.append(_dot(jnp.concatenate(gated, axis=1), wpa_ref[...]))
    y_b = [_mm(ypg_ref[rs, :], wpb_ref[...]) for rs in groups]
    m = [sga_ref[rs, :] * y_a[i] + sgb_ref[rs, :] * y_b[i] for i, rs in enumerate(groups)]
    x1 = [x_ref[rs, :] + _mm(m[i], wout_ref[...]) for i, rs in enumerate(groups)]
    gate = [jax.nn.sigmoid(_mm(_rmsnorm(x1[i], npl_ref[...]), wpg_ref[...])) for i in range(parts)]
    pe = [_mm(p_ref[rs, :], wpp_ref[...]) for rs in groups]
    for i, rs in enumerate(groups):
        x2 = x1[i] + gate[i] * pe[i]
        if final:
            x2 = _rmsnorm(x2, fn_ref[...])
        y_ref[rs, :] = x2


def _merge(li, o, zs, ypg, sga, sgb, x, p, gn, wpa, wpb, wout, npl, wpg, wpp, fn, *, tile, final):
    m = x.shape[0]
    ple = p.shape[-1]
    row_spec = lambda c: pl.BlockSpec((tile, c), lambda i: (i, 0))
    sq = _layer(li, (D_MODEL, D_MODEL))
    return pl.pallas_call(
        functools.partial(_merge_body, final=final, parts=2),
        grid=(m // tile,),
        in_specs=[row_spec(D_MODEL)] * 6 + [pl.BlockSpec((None, tile, ple), lambda i: (li, i, 0)),
                                            _layer(li, (1, HEAD_DIM)), sq, sq, sq, _layer(li, (1, D_MODEL)),
                                            sq, _layer(li, (ple, D_MODEL)), _full((1, D_MODEL))],
        out_specs=row_spec(D_MODEL),
        out_shape=jax.ShapeDtypeStruct((m, D_MODEL), F32),
        compiler_params=pltpu.CompilerParams(dimension_semantics=("arbitrary",), vmem_limit_bytes=VMEM_LIMIT),
        name="merge",
    )(o, zs, ypg, sga, sgb, x, p, gn, wpa, wpb, wout, npl, wpg, wpp, fn)


def kernel(x_prompt, x_sample, p_prompt, p_sample, state_conv, state_delta, state_pool, norm_mix, w_in, conv_w,
           a_log, dt_bias, gdn_norm, w_proj_a, pool_w, pool_scale, w_proj_b, w_out, norm_ple, w_ple_gate,
           w_ple_proj, final_norm):
    depth = w_in.shape[0]
    batch, seq, _ = x_prompt.shape
    nb, steps, _ = x_sample.shape
    rowvec = lambda a: a.reshape(depth, 1, -1)
    pad128 = lambda a: jnp.pad(a, ((0, 0), (0, 128 - a.shape[1]))).reshape(depth, 1, 128)
    mix_w = (rowvec(norm_mix), *_wprep(w_in, tk=128), conv_w, pad128(a_log), pad128(dt_bias),
             pool_w.astype(BF16), rowvec(pool_scale))
    merge_w = (rowvec(gdn_norm), w_proj_a.astype(BF16), w_proj_b.astype(BF16), w_out.astype(BF16),
               rowvec(norm_ple), w_ple_gate.astype(BF16), w_ple_proj.astype(BF16), final_norm.reshape(1, -1))

    xp = x_prompt.reshape(batch * seq, D_MODEL)
    pp = p_prompt.reshape(depth, batch * seq, -1)
    conv_p, delta_p, pool_p = [], [], []
    for li in range(depth):
        q, k, v, g, beta, zs, ypg, sga, sgb, cnew, pnew = _mixin_prompt(li, xp, *mix_w, batch=batch, seq=seq, tile=256)
        o, s_new = _delta_prompt(q, k, v, g, beta, batch=batch, seq=seq, tile=256, nseq=4)
        xp = _merge(li, o, zs, ypg, sga, sgb, xp, pp, *merge_w, tile=512, final=(li == depth - 1))
        conv_p.append(cnew)
        delta_p.append(s_new)
        pool_p.append(pnew)
    y_prompt = xp.reshape(batch, seq, D_MODEL)

    swap = lambda a: jnp.swapaxes(a, 0, 1)
    flat = lambda a: a.reshape(-1, a.shape[-1])
    bmajor = lambda a: swap(a.reshape(-1, nb, a.shape[-1]))
    xs = swap(x_sample)
    ps = jnp.swapaxes(p_sample, 1, 2).reshape(depth, steps * nb, -1)
    conv_s, delta_s, pool_s = [], None, []
    for li in range(depth):
        q, k, v, g, beta, zs, ypg, sga, sgb, cnew, pnew = _mixin_sample(
            li, xs, *mix_w, swap(state_conv[li]), swap(state_pool[li]), bb=32, pos0=PAST_LEN)
        wkqd, wv, kt, qkd, glx = _sample_prep(flat(q), flat(k), flat(v), flat(g), flat(beta), steps=steps, nb=nb)
        r, delta_s = _sample_state(li, state_delta, bmajor(wkqd), bmajor(wv), bmajor(kt), glx, delta_s, bb=8)
        o = _sample_out(flat(swap(r)), wv, qkd, steps=steps, nb=nb)
        xs = _merge(li, o, flat(zs), flat(ypg), flat(sga), flat(sgb), flat(xs), ps, *merge_w, tile=steps * nb,
                    final=(li == depth - 1)).reshape(steps, nb, D_MODEL)
        conv_s.append(swap(cnew))
        pool_s.append(swap(pnew))
    y_sample = swap(xs)

    return (y_prompt, y_sample, jnp.stack(conv_p), jnp.stack(delta_p), jnp.stack(pool_p),
            jnp.stack(conv_s), delta_s, jnp.stack(pool_s))
```

```python
import functools

import jax
import jax.numpy as jnp
from jax import lax
from jax.experimental import pallas as pl
from jax.experimental.pallas import tpu as pltpu

F32 = jnp.float32
BF16 = jnp.bfloat16

D_MODEL = 1024
N_HEADS = 8
HEAD_DIM = 128
CONV_W = 4
CONV_CH = 3 * D_MODEL
POOL_WINDOWS = (2, 4, 8, 16)
POOL_HIST = 15
POOL_GC = 256
EPS = 1e-6
PAST_LEN = 16384
CHUNK = 64
BLOCK = 2 * CHUNK
VMEM_LIMIT = 56 * 1024 * 1024

_SPLIT = 4096
_C_QKV, _C_Z, _C_U, _C_GP, _C_GA, _C_GB, _C_A, _C_B = 0, 3072, 4096, 5120, 6144, 7168, 8192, 8320
_W_COLS = 8448


def _dot(a, b, nt=False):
    dims = (((1,), (1,)), ((), ())) if nt else (((1,), (0,)), ((), ()))
    return lax.dot_general(a, b, dims, preferred_element_type=F32)


def _mm(a, b, nt=False):
    return _dot(a.astype(BF16), b.astype(BF16), nt)


def _split3(x):
    x0 = x.astype(BF16)
    r = x - x0.astype(F32)
    x1 = r.astype(BF16)
    x2 = (r - x1.astype(F32)).astype(BF16)
    return x0, x1, x2


def _mm_exact_rhs(a_bf, b):
    b0, b1, b2 = _split3(b)
    return _dot(a_bf, b0) + _dot(a_bf, b1) + _dot(a_bf, b2)


def _mm_exact_lhs(a, b_bf):
    a0, a1, a2 = _split3(a)
    return _dot(a0, b_bf) + _dot(a1, b_bf) + _dot(a2, b_bf)


def _silu(x):
    return x * jax.nn.sigmoid(x)


def _rmsnorm(x, w):
    return x * lax.rsqrt(jnp.mean(x * x, axis=-1, keepdims=True) + EPS) * w


def _full(shape):
    n = len(shape)
    return pl.BlockSpec(shape, lambda *_: (0,) * n, pipeline_mode=pl.Buffered(1))


def _layer(li, shape):
    n = len(shape)
    return pl.BlockSpec((None,) + tuple(shape), lambda *_: (li,) + (0,) * n, pipeline_mode=pl.Buffered(1))


def _wprep_body(wt_ref, o_ref):
    n_tail = wt_ref.shape[0] - _SPLIT - 2 * N_HEADS
    for g in range(_SPLIT // 128):
        o_ref[:, g * 128:(g + 1) * 128] = wt_ref[g * 128:(g + 1) * 128, :].T.astype(BF16)
    for g in range(n_tail // 128):
        r0 = _SPLIT + 2 * N_HEADS + g * 128
        o_ref[:, _C_U + g * 128:_C_U + (g + 1) * 128] = wt_ref[r0:r0 + 128, :].T.astype(BF16)
    ab = wt_ref[_SPLIT:_SPLIT + 128, :].T
    lane = lax.broadcasted_iota(jnp.int32, ab.shape, 1)
    o_ref[:, _C_A:_C_A + 128] = jnp.where(lane < N_HEADS, ab, 0.0).astype(BF16)
    o_ref[:, _C_B:_C_B + 128] = jnp.where(lane < N_HEADS, pltpu.roll(ab, 128 - N_HEADS, 1), 0.0).astype(BF16)


def _wprep(w_in, *, tk):
    depth, k, n = w_in.shape
    return pl.pallas_call(
        _wprep_body, grid=(depth, k // tk),
        in_specs=[pl.BlockSpec((None, n, tk), lambda l, r: (l, 0, r))],
        out_specs=pl.BlockSpec((None, tk, _W_COLS), lambda l, r: (l, r, 0)),
        out_shape=jax.ShapeDtypeStruct((depth, k, _W_COLS), BF16),
        compiler_params=pltpu.CompilerParams(dimension_semantics=("arbitrary", "arbitrary"),
                                             vmem_limit_bytes=VMEM_LIMIT),
        name="wprep",
    )(jnp.swapaxes(w_in, 1, 2))


def _mixin_core(h, w_ref, cw_ref, alog_ref, dtb_ref, pw_ref, ps_ref, outs, pos, conv_tap, pool_sums, store):
    q_ref, k_ref, v_ref, g_ref, b_ref, zs_ref, ypg_ref, sga_ref, sgb_ref = outs
    proj = lambda c0, n=D_MODEL: _dot(h, w_ref[:, c0:c0 + n])
    xcs = [proj(_C_QKV + seg * D_MODEL) for seg in range(3)]
    gates = [proj(_C_Z), proj(_C_GA), proj(_C_GB)]
    a_r = proj(_C_A, 128)
    b_r = proj(_C_B, 128)
    u = proj(_C_U)
    gp = proj(_C_GP)

    for seg, out_ref in enumerate((q_ref, k_ref, v_ref)):
        c0 = seg * D_MODEL
        xc = xcs[seg]
        acc = xc * cw_ref[CONV_W - 1:CONV_W, c0:c0 + D_MODEL]
        for j in range(CONV_W - 1):
            acc = acc + conv_tap(j, c0, xc) * cw_ref[j:j + 1, c0:c0 + D_MODEL]
        y = _silu(acc)
        if seg < 2:
            for hh in range(N_HEADS):
                ls = slice(hh * HEAD_DIM, (hh + 1) * HEAD_DIM)
                yh = y[:, ls]
                inv = lax.rsqrt(jnp.sum(yh * yh, axis=-1, keepdims=True) + EPS)
                if seg == 0:
                    inv = inv * (HEAD_DIM ** -0.5)
                store(out_ref, yh * inv, ls)
        else:
            store(out_ref, y, None)

    store(zs_ref, _silu(gates[0]), None)
    store(sga_ref, jax.nn.sigmoid(gates[1]), None)
    store(sgb_ref, jax.nn.sigmoid(gates[2]), None)
    xs = a_r + dtb_ref[...]
    softplus = jnp.maximum(xs, 0.0) + jnp.log1p(jnp.exp(-jnp.abs(xs)))
    store(g_ref, -jnp.exp(alog_ref[...]) * softplus, None)
    store(b_ref, jax.nn.sigmoid(b_r), None)

    sgp = _silu(gp)
    sums = pool_sums(u)
    for gi, win in enumerate(POOL_WINDOWS):
        l0 = gi * POOL_GC
        ls = slice(l0, l0 + POOL_GC)
        count = jnp.minimum(pos + 1, win).astype(F32)
        y = sums[gi] / count - u[:, ls]
        yp = _mm(y, pw_ref[gi]) * ps_ref[:, ls]
        store(ypg_ref, yp * sgp[:, ls], ls)
    return xcs, u


def _mixin_weight_specs(li):
    return [_layer(li, (1, D_MODEL)), _layer(li, (D_MODEL, _W_COLS)), _layer(li, (CONV_W, CONV_CH)),
            _layer(li, (1, 128)), _layer(li, (1, 128)), _layer(li, (4, POOL_GC, POOL_GC)), _layer(li, (1, D_MODEL))]


def _mixin_prompt_body(x_ref, nm_ref, w_ref, cw_ref, alog_ref, dtb_ref, pw_ref, ps_ref,
                       q_ref, k_ref, v_ref, g_ref, b_ref, zs_ref, ypg_ref, sga_ref, sgb_ref, cnew_ref, pnew_ref,
                       cext, pext, *, rows):
    hc, hp = 8, 16
    t = pl.program_id(1)

    @pl.when(t == 0)
    def _():
        cext[0:hc, :] = jnp.zeros((hc, CONV_CH), F32)
        pext[0:hp, :] = jnp.zeros((hp, D_MODEL), F32)

    h = _rmsnorm(x_ref[...], nm_ref[...]).astype(BF16)

    def conv_tap(j, c0, xc):
        if j == 0:
            cext[hc:hc + rows, c0:c0 + D_MODEL] = xc
        off = hc - (CONV_W - 1 - j)
        return cext[off:off + rows, c0:c0 + D_MODEL]

    def pool_sums(u):
        pext[hp:hp + rows, :] = u
        lvl = pext[...]
        out = []
        for gi, win in enumerate(POOL_WINDOWS):
            lvl = lvl + pltpu.roll(lvl, win // 2, 0)
            out.append(lvl[hp:, :POOL_GC])
            lvl = lvl[:, POOL_GC:]
        return out

    def store(ref, val, ls):
        if ls is None:
            ref[...] = val.astype(ref.dtype)
        else:
            ref[:, ls] = val.astype(ref.dtype)

    pos = t * rows + lax.broadcasted_iota(jnp.int32, (rows, 1), 0)
    outs = (q_ref, k_ref, v_ref, g_ref, b_ref, zs_ref, ypg_ref, sga_ref, sgb_ref)
    _mixin_core(h, w_ref, cw_ref, alog_ref, dtb_ref, pw_ref, ps_ref, outs, pos, conv_tap, pool_sums, store)
    cnew_ref[...] = cext[hc + rows - (CONV_W - 1):hc + rows, :]
    pnew_ref[...] = pext[hp + rows - POOL_HIST:hp + rows, :]
    cext[0:hc, :] = cext[rows:rows + hc, :]
    pext[0:hp, :] = pext[rows:rows + hp, :]


def _mixin_prompt(li, x, nm, w_big, cw, alog, dtb, pw, ps, *, batch, seq, tile):
    m = x.shape[0]
    nt = seq // tile
    row_spec = lambda c: pl.BlockSpec((tile, c), lambda b, t: (b * nt + t, 0))
    in_specs = [row_spec(D_MODEL)] + _mixin_weight_specs(li)
    cnew_spec = pl.BlockSpec((None, CONV_W - 1, CONV_CH), lambda b, t: (b, 0, 0))
    pnew_spec = pl.BlockSpec((None, POOL_HIST, D_MODEL), lambda b, t: (b, 0, 0))
    big = jax.ShapeDtypeStruct((m, D_MODEL), F32)
    small = jax.ShapeDtypeStruct((m, 128), F32)
    gate = jax.ShapeDtypeStruct((m, D_MODEL), BF16)
    out_shape = [big, big, big, small, small, gate, gate, gate, gate,
                 jax.ShapeDtypeStruct((batch, CONV_W - 1, CONV_CH), F32),
                 jax.ShapeDtypeStruct((batch, POOL_HIST, D_MODEL), F32)]
    out_specs = [row_spec(D_MODEL)] * 3 + [row_spec(128)] * 2 + [row_spec(D_MODEL)] * 4 + [cnew_spec, pnew_spec]
    return pl.pallas_call(
        functools.partial(_mixin_prompt_body, rows=tile),
        grid=(batch, nt), in_specs=in_specs, out_specs=out_specs, out_shape=out_shape,
        scratch_shapes=[pltpu.VMEM((8 + tile, CONV_CH), F32), pltpu.VMEM((16 + tile, D_MODEL), F32)],
        compiler_params=pltpu.CompilerParams(dimension_semantics=("arbitrary", "arbitrary"),
                                             vmem_limit_bytes=VMEM_LIMIT),
        name="mixin_prompt",
    )(x, nm, w_big, cw, alog, dtb, pw, ps)


def _mixin_sample_body(x_ref, nm_ref, w_ref, cw_ref, alog_ref, dtb_ref, pw_ref, ps_ref, chist_ref, phist_ref,
                       q_ref, k_ref, v_ref, g_ref, b_ref, zs_ref, ypg_ref, sga_ref, sgb_ref, cnew_ref, pnew_ref,
                       *, steps, bb, pos0):
    rows = steps * bb
    h = _rmsnorm(x_ref[...].reshape(rows, D_MODEL), nm_ref[...]).astype(BF16)

    def delayed(new, hist_ref, nhist, d, ls_new, ls_hist):
        parts = []
        for t in range(steps):
            src = t - d
            parts.append(new[src * bb:(src + 1) * bb, ls_new] if src >= 0 else hist_ref[nhist + src, :, ls_hist])
        return jnp.concatenate(parts, axis=0)

    def conv_tap(j, c0, xc):
        return delayed(xc, chist_ref, CONV_W - 1, CONV_W - 1 - j, slice(None), slice(c0, c0 + D_MODEL))

    def pool_sums(u):
        out = []
        for gi, win in enumerate(POOL_WINDOWS):
            ls = slice(gi * POOL_GC, (gi + 1) * POOL_GC)
            acc = u[:, ls]
            for s in range(1, win):
                acc = acc + delayed(u, phist_ref, POOL_HIST, s, ls, ls)
            out.append(acc)
        return out

    def store(ref, val, ls):
        val = val.reshape(steps, bb, val.shape[-1]).astype(ref.dtype)
        if ls is None:
            ref[...] = val
        else:
            ref[:, :, ls] = val

    pos = pos0 + lax.broadcasted_iota(jnp.int32, (rows, 1), 0) // bb
    outs = (q_ref, k_ref, v_ref, g_ref, b_ref, zs_ref, ypg_ref, sga_ref, sgb_ref)
    xcs, u = _mixin_core(h, w_ref, cw_ref, alog_ref, dtb_ref, pw_ref, ps_ref, outs, pos, conv_tap, pool_sums, store)
    for i in range(CONV_W - 1):
        src = steps + i - (CONV_W - 1)
        for seg in range(3):
            ls = slice(seg * D_MODEL, (seg + 1) * D_MODEL)
            cnew_ref[i, :, ls] = (xcs[seg][src * bb:(src + 1) * bb, :] if src >= 0
                                  else chist_ref[CONV_W - 1 + src, :, ls])
    for i in range(POOL_HIST):
        src = steps + i - POOL_HIST
        pnew_ref[i] = u[src * bb:(src + 1) * bb, :] if src >= 0 else phist_ref[POOL_HIST + src]


def _mixin_sample(li, x, nm, w_big, cw, alog, dtb, pw, ps, chist, phist, *, bb, pos0):
    steps, nb, _ = x.shape
    slab_spec = lambda n, c: pl.BlockSpec((n, bb, c), lambda i: (0, i, 0))
    in_specs = ([slab_spec(steps, D_MODEL)] + _mixin_weight_specs(li)
                + [slab_spec(CONV_W - 1, CONV_CH), slab_spec(POOL_HIST, D_MODEL)])
    big = jax.ShapeDtypeStruct((steps, nb, D_MODEL), F32)
    small = jax.ShapeDtypeStruct((steps, nb, 128), F32)
    gate = jax.ShapeDtypeStruct((steps, nb, D_MODEL), BF16)
    out_shape = [big, big, big, small, small, gate, gate, gate, gate,
                 jax.ShapeDtypeStruct(chist.shape, F32), jax.ShapeDtypeStruct(phist.shape, F32)]
    out_specs = ([slab_spec(steps, D_MODEL)] * 3 + [slab_spec(steps, 128)] * 2 + [slab_spec(steps, D_MODEL)] * 4
                 + [slab_spec(CONV_W - 1, CONV_CH), slab_spec(POOL_HIST, D_MODEL)])
    return pl.pallas_call(
        functools.partial(_mixin_sample_body, steps=steps, bb=bb, pos0=pos0),
        grid=(nb // bb,), in_specs=in_specs, out_specs=out_specs, out_shape=out_shape,
        compiler_params=pltpu.CompilerParams(dimension_semantics=("arbitrary",), vmem_limit_bytes=VMEM_LIMIT),
        name="mixin_sample",
    )(x, nm, w_big, cw, alog, dtb, pw, ps, chist, phist)


def _delta_body(q_ref, k_ref, v_ref, g_ref, b_ref, o_ref, sout_ref, s_ref, *, tile, nseq):
    t = pl.program_id(1)

    @pl.when(t == 0)
    def _():
        s_ref[...] = jnp.zeros(s_ref.shape, F32)

    r = lax.broadcasted_iota(jnp.int32, (BLOCK, BLOCK), 0)
    c = lax.broadcasted_iota(jnp.int32, (BLOCK, BLOCK), 1)
    same = (r // CHUNK) == (c // CHUNK)
    low = same & (r >= c)
    strict = same & (r > c)
    eye = jnp.where(r == c, 1.0, 0.0).astype(F32)
    l_blk = jnp.where(low, 1.0, 0.0).astype(BF16)
    ones0 = jnp.where(c < CHUNK, 1.0, 0.0).astype(BF16)
    ones1 = jnp.where(c >= CHUNK, 1.0, 0.0).astype(BF16)
    first = lax.broadcasted_iota(jnp.int32, (BLOCK, 1), 0) < CHUNK
    zeros_half = jnp.zeros((CHUNK, HEAD_DIM), F32)
    cat = jnp.concatenate
    col = lambda a, hh: a[:, hh:hh + 1]
    units = [(sq, hh) for sq in range(nseq) for hh in range(N_HEADS)]
    n_units = range(len(units))
    ls = [slice(hh * HEAD_DIM, (hh + 1) * HEAD_DIM) for hh in range(N_HEADS)]

    def block(i, carry):
        rows = pl.ds(pl.multiple_of(i * BLOCK, BLOCK), BLOCK)
        gp = [g_ref[sq, rows, :] for sq in range(nseq)]
        bt = [b_ref[sq, rows, :] for sq in range(nseq)]
        g_cum = [_mm_exact_rhs(l_blk, x) for x in gp]
        tot0 = [_mm_exact_rhs(ones0, x) for x in gp]
        tot1 = [_mm_exact_rhs(ones1, x) for x in gp]
        g_t = [x.T for x in g_cum]
        e_g = [jnp.exp(x) for x in g_cum]
        e_tail = [jnp.exp(jnp.where(first, tot0[sq], tot1[sq]) - g_cum[sq]) for sq in range(nseq)]
        e_tot = ([jnp.exp(x) for x in tot0], [jnp.exp(x) for x in tot1])

        kh = [k_ref[sq, rows, ls[hh]] for sq, hh in units]
        qh = [q_ref[sq, rows, ls[hh]] for sq, hh in units]
        beta = [col(bt[sq], hh) for sq, hh in units]
        eg = [col(e_g[sq], hh) for sq, hh in units]
        decay = [jnp.exp(jnp.where(low, col(g_cum[sq], hh) - g_t[sq][hh:hh + 1, :], -jnp.inf)) for sq, hh in units]
        kq = [_mm(cat([kh[n], qh[n]], axis=0), kh[n], nt=True) for n in n_units]
        p = [-jnp.where(strict, kq[n][:BLOCK] * decay[n] * beta[n], 0.0) for n in n_units]
        qk = [kq[n][BLOCK:] * decay[n] for n in n_units]
        tinv = [eye + p[n] for n in n_units]
        p = [_mm(p[n], p[n]) for n in n_units]
        for _ in range(4):
            pp = [_mm(p[n], cat([p[n], tinv[n]], axis=1)) for n in n_units]
            p = [pp[n][:, :BLOCK] for n in n_units]
            tinv = [tinv[n] + pp[n][:, BLOCK:] for n in n_units]
        tinv = [tinv[n] + _mm(p[n], tinv[n]) for n in n_units]
        sol = [_mm(tinv[n], cat([v_ref[sq, rows, ls[hh]] * beta[n], kh[n] * (beta[n] * eg[n])], axis=1))
               for n, (sq, hh) in enumerate(units)]
        wv = [sol[n][:, :HEAD_DIM] for n in n_units]
        wk = [sol[n][:, HEAD_DIM:] for n in n_units]
        qd = [qh[n] * eg[n] for n in n_units]
        kt_t = [(kh[n] * col(e_tail[sq], hh)).T for n, (sq, hh) in enumerate(units)]
        s_cur = [s_ref[sq, hh] for sq, hh in units]
        o_parts = []
        for half in range(2):
            hs = slice(half * CHUNK, (half + 1) * CHUNK)
            res = [_mm(cat([wk[n][hs], qd[n][hs]], axis=0), s_cur[n]) for n in n_units]
            u_new = [wv[n][hs] - res[n][:CHUNK] for n in n_units]
            u_pad = [cat([u_new[n], zeros_half] if half == 0 else [zeros_half, u_new[n]], axis=0) for n in n_units]
            upd = [_mm(cat([qk[n][hs], kt_t[n]], axis=0), u_pad[n]) for n in n_units]
            o_parts.append([res[n][CHUNK:] + upd[n][:CHUNK] for n in n_units])
            s_cur = [s_cur[n] * col(e_tot[half][sq], hh) + upd[n][CHUNK:] for n, (sq, hh) in enumerate(units)]
        for n, (sq, hh) in enumerate(units):
            s_ref[sq, hh] = s_cur[n]
            o_ref[sq, rows, ls[hh]] = cat([o_parts[0][n], o_parts[1][n]], axis=0)
        return carry

    lax.fori_loop(0, tile // BLOCK, block, 0)
    sout_ref[...] = s_ref[...]


def _delta_prompt(q, k, v, g, beta, *, batch, seq, tile, nseq):
    as3d = lambda a: a.reshape(batch, seq, a.shape[-1])
    row_spec = lambda c: pl.BlockSpec((nseq, tile, c), lambda b, t: (b, t, 0))
    s_shape = (N_HEADS, HEAD_DIM, HEAD_DIM)
    o, s_new = pl.pallas_call(
        functools.partial(_delta_body, tile=tile, nseq=nseq),
        grid=(batch // nseq, seq // tile),
        in_specs=[row_spec(D_MODEL)] * 3 + [row_spec(128)] * 2,
        out_specs=[row_spec(D_MODEL), pl.BlockSpec((nseq,) + s_shape, lambda b, t: (b, 0, 0, 0))],
        out_shape=[jax.ShapeDtypeStruct((batch, seq, D_MODEL), F32), jax.ShapeDtypeStruct((batch,) + s_shape, F32)],
        scratch_shapes=[pltpu.VMEM((nseq,) + s_shape, F32)],
        compiler_params=pltpu.CompilerParams(dimension_semantics=("arbitrary", "arbitrary"),
                                             vmem_limit_bytes=VMEM_LIMIT),
        name="delta_prompt",
    )(as3d(q), as3d(k), as3d(v), as3d(g), as3d(beta))
    return o.reshape(batch * seq, D_MODEL), s_new


def _head_indicators():
    d = lax.broadcasted_iota(jnp.int32, (D_MODEL, 128), 0) // HEAD_DIM
    hcol = lax.broadcasted_iota(jnp.int32, (D_MODEL, 128), 1)
    e_sum = jnp.where(d == hcol, 1.0, 0.0).astype(BF16)
    hrow = lax.broadcasted_iota(jnp.int32, (128, D_MODEL), 0)
    d2 = lax.broadcasted_iota(jnp.int32, (128, D_MODEL), 1) // HEAD_DIM
    e_bc = jnp.where(hrow == d2, 1.0, 0.0).astype(BF16)
    return e_sum, e_bc


def _sample_prep_body(q_ref, k_ref, v_ref, g_ref, b_ref, wkqd_ref, wv_ref, kt_ref, qkd_ref, glx_ref,
                      *, steps, nb):
    e_sum, e_bc = _head_indicators()
    expand = lambda x: _mm_exact_lhs(x, e_bc)
    hsum = lambda y: _mm_exact_lhs(y, e_sum)
    sl = lambda ref, i: ref[i * nb:(i + 1) * nb, :]
    q = [sl(q_ref, i) for i in range(steps)]
    k = [sl(k_ref, i) for i in range(steps)]
    v = [sl(v_ref, i) for i in range(steps)]
    beta = [sl(b_ref, i) for i in range(steps)]
    g_cum = []
    for i in range(steps):
        gi = sl(g_ref, i)
        g_cum.append(gi if i == 0 else g_cum[-1] + gi)
    wv, wk = [], []
    for i in range(steps):
        acc_v = v[i] * expand(beta[i])
        acc_k = k[i] * expand(beta[i] * jnp.exp(g_cum[i]))
        for j in range(i):
            a_ij = expand(hsum(k[i] * k[j]) * jnp.exp(g_cum[i] - g_cum[j]) * beta[i])
            acc_v = acc_v - a_ij * wv[j]
            acc_k = acc_k - a_ij * wk[j]
        wv.append(acc_v)
        wk.append(acc_k)
    zeros = jnp.zeros((nb, D_MODEL), F32)
    for i in range(steps):
        wkqd_ref[i * nb:(i + 1) * nb, :] = wk[i]
        wkqd_ref[(steps + i) * nb:(steps + i + 1) * nb, :] = q[i] * expand(jnp.exp(g_cum[i]))
        wv_ref[i * nb:(i + 1) * nb, :] = wv[i]
        wv_ref[(steps + i) * nb:(steps + i + 1) * nb, :] = zeros
        kt_ref[i * nb:(i + 1) * nb, :] = k[i] * expand(jnp.exp(g_cum[steps - 1] - g_cum[i]))
        kt_ref[(steps + i) * nb:(steps + i + 1) * nb, :] = zeros
        for j in range(steps):
            idx = i * steps + j
            if j <= i:
                qkd_ref[idx * nb:(idx + 1) * nb, :] = hsum(q[i] * k[j]) * jnp.exp(g_cum[i] - g_cum[j])
            else:
                qkd_ref[idx * nb:(idx + 1) * nb, :] = jnp.zeros((nb, 128), F32)
    glx_ref[...] = expand(jnp.exp(g_cum[steps - 1]))


def _sample_prep(q, k, v, g, beta, *, steps, nb):
    m = steps * nb
    out_shape = [jax.ShapeDtypeStruct((2 * m, D_MODEL), F32)] * 3 + [
        jax.ShapeDtypeStruct((steps * steps * nb, 128), F32), jax.ShapeDtypeStruct((nb, D_MODEL), F32)]
    return pl.pallas_call(
        functools.partial(_sample_prep_body, steps=steps, nb=nb),
        grid=(1,),
        in_specs=[_full((m, D_MODEL))] * 3 + [_full((m, 128))] * 2,
        out_specs=[_full(s.shape) for s in out_shape],
        out_shape=out_shape,
        compiler_params=pltpu.CompilerParams(dimension_semantics=("arbitrary",), vmem_limit_bytes=VMEM_LIMIT),
        name="sample_prep",
    )(q, k, v, g, beta)


def _sample_state_body(*refs, bb, group):
    s_ref, wkqd_ref, wv_ref, kt_ref, glx_ref = refs[:5]
    r_ref, snew_ref = refs[-2:]
    zeros_pad = jnp.zeros((HEAD_DIM - 8, HEAD_DIM), F32)

    def per_group(gi, carry):
        units = [(gi * group + j, hh) for j in range(group) for hh in range(N_HEADS)]
        ls = [slice(hh * HEAD_DIM, (hh + 1) * HEAD_DIM) for hh in range(N_HEADS)]
        pad = lambda a, n: jnp.concatenate([a, zeros_pad[:n - 8]], axis=0)
        s0 = [s_ref[b, hh] for b, hh in units]
        res = [_mm(pad(wkqd_ref[b, :, ls[hh]], 16), s0[n])[:8] for n, (b, hh) in enumerate(units)]
        kt_t = [pad(kt_ref[b, :, ls[hh]], HEAD_DIM).T for b, hh in units]
        upd = [_mm(kt_t[n], pad(wv_ref[b, :, ls[hh]] - res[n], HEAD_DIM)) for n, (b, hh) in enumerate(units)]
        for n, (b, hh) in enumerate(units):
            r_ref[b, :, ls[hh]] = res[n]
            snew_ref[b, hh] = s0[n] * glx_ref[b, :, ls[hh]] + upd[n]
        return carry

    lax.fori_loop(0, bb // group, per_group, 0)


def _sample_state(li, state, wkqd, wv, kt, glx, prev, *, bb):
    nb = state.shape[1]
    s_spec = pl.BlockSpec((None, bb, N_HEADS, HEAD_DIM, HEAD_DIM), lambda i: (li, i, 0, 0, 0))
    slot_spec = pl.BlockSpec((bb, 8, D_MODEL), lambda i: (i, 0, 0))
    in_specs = [s_spec, slot_spec, slot_spec, slot_spec, pl.BlockSpec((bb, 1, D_MODEL), lambda i: (i, 0, 0))]
    args = [state, wkqd, wv, kt, glx.reshape(nb, 1, D_MODEL)]
    aliases = {}
    if prev is not None:
        in_specs.append(pl.BlockSpec(memory_space=pl.ANY))
        args.append(prev)
        aliases = {len(args) - 1: 1}
    return pl.pallas_call(
        functools.partial(_sample_state_body, bb=bb, group=2),
        grid=(nb // bb,),
        in_specs=in_specs,
        out_specs=[slot_spec, s_spec],
        out_shape=[jax.ShapeDtypeStruct((nb, 8, D_MODEL), F32), jax.ShapeDtypeStruct(state.shape, F32)],
        input_output_aliases=aliases,
        compiler_params=pltpu.CompilerParams(dimension_semantics=("arbitrary",), vmem_limit_bytes=VMEM_LIMIT),
        name="sample_state",
    )(*args)


def _sample_out_body(r_ref, wv_ref, qkd_ref, o_ref, *, steps, nb):
    _, e_bc = _head_indicators()
    sl = lambda ref, i: ref[i * nb:(i + 1) * nb, :]
    u = [sl(wv_ref, j) - sl(r_ref, j) for j in range(steps)]
    for i in range(steps):
        acc = sl(r_ref, steps + i)
        for j in range(i + 1):
            acc = acc + _mm_exact_lhs(sl(qkd_ref, i * steps + j), e_bc) * u[j]
        o_ref[i * nb:(i + 1) * nb, :] = acc


def _sample_out(r, wv, qkd, *, steps, nb):
    m = steps * nb
    return pl.pallas_call(
        functools.partial(_sample_out_body, steps=steps, nb=nb),
        grid=(1,),
        in_specs=[_full(r.shape), _full(wv.shape), _full(qkd.shape)],
        out_specs=_full((m, D_MODEL)),
        out_shape=jax.ShapeDtypeStruct((m, D_MODEL), F32),
        compiler_params=pltpu.CompilerParams(dimension_semantics=("arbitrary",), vmem_limit_bytes=VMEM_LIMIT),
        name="sample_out",
    )(r, wv, qkd)


def _merge_body(o_ref, zs_ref, ypg_ref, sga_ref, sgb_ref, x_ref, p_ref, gn_ref, wpa_ref, wpb_ref, wout_ref,
                npl_ref, wpg_ref, wpp_ref, fn_ref, y_ref, *, final, parts):
    step = o_ref.shape[0] // parts
    groups = [slice(i * step, (i + 1) * step) for i in range(parts)]
    gn = gn_ref[...]
    y_a = []
    for rs in groups:
        gated = []
        for hh in range(N_HEADS):
            ls = slice(hh * HEAD_DIM, (hh + 1) * HEAD_DIM)
            oh = o_ref[rs, ls]
            on = oh * lax.rsqrt(jnp.mean(oh * oh, axis=-1, keepdims=True) + EPS) * gn
            gated.append((on * zs_ref[rs, ls]).astype(BF16))
        y_a.append(_dot(jnp.concatenate(gated, axis=1), wpa_ref[...]))
    y_b = [_mm(ypg_ref[rs, :], wpb_ref[...]) for rs in groups]
    m = [sga_ref[rs, :] * y_a[i] + sgb_ref[rs, :] * y_b[i] for i, rs in enumerate(groups)]
    x1 = [x_ref[rs, :] + _mm(m[i], wout_ref[...]) for i, rs in enumerate(groups)]
    gate = [jax.nn.sigmoid(_mm(_rmsnorm(x1[i], npl_ref[...]), wpg_ref[...])) for i in range(parts)]
    pe = [_mm(p_ref[rs, :], wpp_ref[...]) for rs in groups]
    for i, rs in enumerate(groups):
        x2 = x1[i] + gate[i] * pe[i]
        if final:
            x2 = _rmsnorm(x2, fn_ref[...])
        y_ref[rs, :] = x2


def _merge(li, o, zs, ypg, sga, sgb, x, p, gn, wpa, wpb, wout, npl, wpg, wpp, fn, *, tile, final):
    m = x.shape[0]
    ple = p.shape[-1]
    row_spec = lambda c: pl.BlockSpec((tile, c), lambda i: (i, 0))
    sq = _layer(li, (D_MODEL, D_MODEL))
    return pl.pallas_call(
        functools.partial(_merge_body, final=final, parts=2),
        grid=(m // tile,),
        in_specs=[row_spec(D_MODEL)] * 6 + [pl.BlockSpec((None, tile, ple), lambda i: (li, i, 0)),
                                            _layer(li, (1, HEAD_DIM)), sq, sq, sq, _layer(li, (1, D_MODEL)),
                                            sq, _layer(li, (ple, D_MODEL)), _full((1, D_MODEL))],
        out_specs=row_spec(D_MODEL),
        out_shape=jax.ShapeDtypeStruct((m, D_MODEL), F32),
        compiler_params=pltpu.CompilerParams(dimension_semantics=("arbitrary",), vmem_limit_bytes=VMEM_LIMIT),
        name="merge",
    )(o, zs, ypg, sga, sgb, x, p, gn, wpa, wpb, wout, npl, wpg, wpp, fn)


def kernel(x_prompt, x_sample, p_prompt, p_sample, state_conv, state_delta, state_pool, norm_mix, w_in, conv_w,
           a_log, dt_bias, gdn_norm, w_proj_a, pool_w, pool_scale, w_proj_b, w_out, norm_ple, w_ple_gate,
           w_ple_proj, final_norm):
    depth = w_in.shape[0]
    batch, seq, _ = x_prompt.shape
    nb, steps, _ = x_sample.shape
    rowvec = lambda a: a.reshape(depth, 1, -1)
    pad128 = lambda a: jnp.pad(a, ((0, 0), (0, 128 - a.shape[1]))).reshape(depth, 1, 128)
    mix_w = (rowvec(norm_mix), _wprep(w_in, tk=128), conv_w, pad128(a_log), pad128(dt_bias),
             pool_w.astype(BF16), rowvec(pool_scale))
    merge_w = (rowvec(gdn_norm), w_proj_a.astype(BF16), w_proj_b.astype(BF16), w_out.astype(BF16),
               rowvec(norm_ple), w_ple_gate.astype(BF16), w_ple_proj.astype(BF16), final_norm.reshape(1, -1))

    xp = x_prompt.reshape(batch * seq, D_MODEL)
    pp = p_prompt.reshape(depth, batch * seq, -1)
    conv_p, delta_p, pool_p = [], [], []
    for li in range(depth):
        q, k, v, g, beta, zs, ypg, sga, sgb, cnew, pnew = _mixin_prompt(li, xp, *mix_w, batch=batch, seq=seq, tile=256)
        o, s_new = _delta_prompt(q, k, v, g, beta, batch=batch, seq=seq, tile=256, nseq=4)
        xp = _merge(li, o, zs, ypg, sga, sgb, xp, pp, *merge_w, tile=512, final=(li == depth - 1))
        conv_p.append(cnew)
        delta_p.append(s_new)
        pool_p.append(pnew)
    y_prompt = xp.reshape(batch, seq, D_MODEL)

    swap = lambda a: jnp.swapaxes(a, 0, 1)
    flat = lambda a: a.reshape(-1, a.shape[-1])
    bmajor = lambda a: swap(a.reshape(-1, nb, a.shape[-1]))
    xs = swap(x_sample)
    ps = jnp.swapaxes(p_sample, 1, 2).reshape(depth, steps * nb, -1)
    conv_s, delta_s, pool_s = [], None, []
    for li in range(depth):
        q, k, v, g, beta, zs, ypg, sga, sgb, cnew, pnew = _mixin_sample(
            li, xs, *mix_w, swap(state_conv[li]), swap(state_pool[li]), bb=32, pos0=PAST_LEN)
        wkqd, wv, kt, qkd, glx = _sample_prep(flat(q), flat(k), flat(v), flat(g), flat(beta), steps=steps, nb=nb)
        r, delta_s = _sample_state(li, state_delta, bmajor(wkqd), bmajor(wv), bmajor(kt), glx, delta_s, bb=8)
        o = _sample_out(flat(swap(r)), wv, qkd, steps=steps, nb=nb)
        xs = _merge(li, o, flat(zs), flat(ypg), flat(sga), flat(sgb), flat(xs), ps, *merge_w, tile=steps * nb,
                    final=(li == depth - 1)).reshape(steps, nb, D_MODEL)
        conv_s.append(swap(cnew))
        pool_s.append(swap(pnew))
    y_sample = swap(xs)

    return (y_prompt, y_sample, jnp.stack(conv_p), jnp.stack(delta_p), jnp.stack(pool_p),
            jnp.stack(conv_s), delta_s, jnp.stack(pool_s))
```

```python
import functools

import jax
import jax.numpy as jnp
from jax import lax
from jax.experimental import pallas as pl
from jax.experimental.pallas import tpu as pltpu

F32 = jnp.float32
BF16 = jnp.bfloat16

D_MODEL = 1024
N_HEADS = 8
HEAD_DIM = 128
CONV_W = 4
CONV_CH = 3 * D_MODEL
POOL_WINDOWS = (2, 4, 8, 16)
POOL_HIST = 15
POOL_GC = 256
EPS = 1e-6
PAST_LEN = 16384
CHUNK = 64
BLOCK = 2 * CHUNK
VMEM_LIMIT = 56 * 1024 * 1024

_SPLIT = 4096
_C_QKV, _C_Z, _C_U, _C_GP, _C_GA, _C_GB, _C_A, _C_B = 0, 3072, 4096, 5120, 6144, 7168, 8192, 8320
_W_COLS = 8448


def _dot(a, b, nt=False):
    dims = (((1,), (1,)), ((), ())) if nt else (((1,), (0,)), ((), ()))
    return lax.dot_general(a, b, dims, preferred_element_type=F32)


def _mm(a, b, nt=False):
    return _dot(a.astype(BF16), b.astype(BF16), nt)


def _split3(x):
    x0 = x.astype(BF16)
    r = x - x0.astype(F32)
    x1 = r.astype(BF16)
    x2 = (r - x1.astype(F32)).astype(BF16)
    return x0, x1, x2


def _mm_exact_rhs(a_bf, b):
    b0, b1, b2 = _split3(b)
    return _dot(a_bf, b0) + _dot(a_bf, b1) + _dot(a_bf, b2)


def _mm_exact_lhs(a, b_bf):
    a0, a1, a2 = _split3(a)
    return _dot(a0, b_bf) + _dot(a1, b_bf) + _dot(a2, b_bf)


def _silu(x):
    return x * jax.nn.sigmoid(x)


def _rmsnorm(x, w):
    return x * lax.rsqrt(jnp.mean(x * x, axis=-1, keepdims=True) + EPS) * w


def _full(shape):
    n = len(shape)
    return pl.BlockSpec(shape, lambda *_: (0,) * n, pipeline_mode=pl.Buffered(1))


def _layer(li, shape):
    n = len(shape)
    return pl.BlockSpec((None,) + tuple(shape), lambda *_: (li,) + (0,) * n, pipeline_mode=pl.Buffered(1))


def _wprep_body(wt_ref, o_ref):
    n_tail = wt_ref.shape[0] - _SPLIT - 2 * N_HEADS
    for g in range(_SPLIT // 128):
        o_ref[:, g * 128:(g + 1) * 128] = wt_ref[g * 128:(g + 1) * 128, :].T.astype(BF16)
    for g in range(n_tail // 128):
        r0 = _SPLIT + 2 * N_HEADS + g * 128
        o_ref[:, _C_U + g * 128:_C_U + (g + 1) * 128] = wt_ref[r0:r0 + 128, :].T.astype(BF16)
    ab = wt_ref[_SPLIT:_SPLIT + 128, :].T
    lane = lax.broadcasted_iota(jnp.int32, ab.shape, 1)
    o_ref[:, _C_A:_C_A + 128] = jnp.where(lane < N_HEADS, ab, 0.0).astype(BF16)
    o_ref[:, _C_B:_C_B + 128] = jnp.where(lane < N_HEADS, pltpu.roll(ab, 128 - N_HEADS, 1), 0.0).astype(BF16)


def _wprep(w_in, *, tk):
    depth, k, n = w_in.shape
    return pl.pallas_call(
        _wprep_body, grid=(depth, k // tk),
        in_specs=[pl.BlockSpec((None, n, tk), lambda l, r: (l, 0, r))],
        out_specs=pl.BlockSpec((None, tk, _W_COLS), lambda l, r: (l, r, 0)),
        out_shape=jax.ShapeDtypeStruct((depth, k, _W_COLS), BF16),
        compiler_params=pltpu.CompilerParams(dimension_semantics=("arbitrary", "arbitrary"),
                                             vmem_limit_bytes=VMEM_LIMIT),
        name="wprep",
    )(jnp.swapaxes(w_in, 1, 2))


def _mixin_core(h, w_ref, cw_ref, alog_ref, dtb_ref, pw_ref, ps_ref, outs, pos, conv_tap, pool_sums, store):
    q_ref, k_ref, v_ref, g_ref, b_ref, zs_ref, ypg_ref, sga_ref, sgb_ref = outs
    proj = lambda c0, n=D_MODEL: _dot(h, w_ref[:, c0:c0 + n])
    xcs = [proj(_C_QKV + seg * D_MODEL) for seg in range(3)]
    gates = [proj(_C_Z), proj(_C_GA), proj(_C_GB)]
    a_r = proj(_C_A, 128)
    b_r = proj(_C_B, 128)
    u = proj(_C_U)
    gp = proj(_C_GP)

    for seg, out_ref in enumerate((q_ref, k_ref, v_ref)):
        c0 = seg * D_MODEL
        xc = xcs[seg]
        acc = xc * cw_ref[CONV_W - 1:CONV_W, c0:c0 + D_MODEL]
        for j in range(CONV_W - 1):
            acc = acc + conv_tap(j, c0, xc) * cw_ref[j:j + 1, c0:c0 + D_MODEL]
        y = _silu(acc)
        if seg < 2:
            for hh in range(N_HEADS):
                ls = slice(hh * HEAD_DIM, (hh + 1) * HEAD_DIM)
                yh = y[:, ls]
                inv = lax.rsqrt(jnp.sum(yh * yh, axis=-1, keepdims=True) + EPS)
                if seg == 0:
                    inv = inv * (HEAD_DIM ** -0.5)
                store(out_ref, yh * inv, ls)
        else:
            store(out_ref, y, None)

    store(zs_ref, _silu(gates[0]), None)
    store(sga_ref, jax.nn.sigmoid(gates[1]), None)
    store(sgb_ref, jax.nn.sigmoid(gates[2]), None)
    xs = a_r + dtb_ref[...]
    softplus = jnp.maximum(xs, 0.0) + jnp.log1p(jnp.exp(-jnp.abs(xs)))
    store(g_ref, -jnp.exp(alog_ref[...]) * softplus, None)
    store(b_ref, jax.nn.sigmoid(b_r), None)

    sgp = _silu(gp)
    sums = pool_sums(u)
    for gi, win in enumerate(POOL_WINDOWS):
        l0 = gi * POOL_GC
        ls = slice(l0, l0 + POOL_GC)
        count = jnp.minimum(pos + 1, win).astype(F32)
        y = sums[gi] / count - u[:, ls]
        yp = _mm(y, pw_ref[gi]) * ps_ref[:, ls]
        store(ypg_ref, yp * sgp[:, ls], ls)
    return xcs, u


def _mixin_weight_specs(li):
    return [_layer(li, (1, D_MODEL)), _layer(li, (D_MODEL, _W_COLS)), _layer(li, (CONV_W, CONV_CH)),
            _layer(li, (1, 128)), _layer(li, (1, 128)), _layer(li, (4, POOL_GC, POOL_GC)), _layer(li, (1, D_MODEL))]


def _mixin_prompt_body(x_ref, nm_ref, w_ref, cw_ref, alog_ref, dtb_ref, pw_ref, ps_ref,
                       q_ref, k_ref, v_ref, g_ref, b_ref, zs_ref, ypg_ref, sga_ref, sgb_ref, cnew_ref, pnew_ref,
                       cext, pext, *, rows):
    hc, hp = 8, 16
    t = pl.program_id(1)

    @pl.when(t == 0)
    def _():
        cext[0:hc, :] = jnp.zeros((hc, CONV_CH), F32)
        pext[0:hp, :] = jnp.zeros((hp, D_MODEL), F32)

    h = _rmsnorm(x_ref[...], nm_ref[...]).astype(BF16)

    def conv_tap(j, c0, xc):
        if j == 0:
            cext[hc:hc + rows, c0:c0 + D_MODEL] = xc
        off = hc - (CONV_W - 1 - j)
        return cext[off:off + rows, c0:c0 + D_MODEL]

    def pool_sums(u):
        pext[hp:hp + rows, :] = u
        lvl = pext[...]
        out = []
        for gi, win in enumerate(POOL_WINDOWS):
            lvl = lvl + pltpu.roll(lvl, win // 2, 0)
            out.append(lvl[hp:, :POOL_GC])
            lvl = lvl[:, POOL_GC:]
        return out

    def store(ref, val, ls):
        if ls is None:
            ref[...] = val.astype(ref.dtype)
        else:
            ref[:, ls] = val.astype(ref.dtype)

    pos = t * rows + lax.broadcasted_iota(jnp.int32, (rows, 1), 0)
    outs = (q_ref, k_ref, v_ref, g_ref, b_ref, zs_ref, ypg_ref, sga_ref, sgb_ref)
    _mixin_core(h, w_ref, cw_ref, alog_ref, dtb_ref, pw_ref, ps_ref, outs, pos, conv_tap, pool_sums, store)
    cnew_ref[...] = cext[hc + rows - (CONV_W - 1):hc + rows, :]
    pnew_ref[...] = pext[hp + rows - POOL_HIST:hp + rows, :]
    cext[0:hc, :] = cext[rows:rows + hc, :]
    pext[0:hp, :] = pext[rows:rows + hp, :]


def _mixin_prompt(li, x, nm, w_big, cw, alog, dtb, pw, ps, *, batch, seq, tile):
    m = x.shape[0]
    nt = seq // tile
    row_spec = lambda c: pl.BlockSpec((tile, c), lambda b, t: (b * nt + t, 0))
    in_specs = [row_spec(D_MODEL)] + _mixin_weight_specs(li)
    cnew_spec = pl.BlockSpec((None, CONV_W - 1, CONV_CH), lambda b, t: (b, 0, 0))
    pnew_spec = pl.BlockSpec((None, POOL_HIST, D_MODEL), lambda b, t: (b, 0, 0))
    big = jax.ShapeDtypeStruct((m, D_MODEL), F32)
    small = jax.ShapeDtypeStruct((m, 128), F32)
    gate = jax.ShapeDtypeStruct((m, D_MODEL), BF16)
    out_shape = [big, big, big, small, small, gate, gate, gate, gate,
                 jax.ShapeDtypeStruct((batch, CONV_W - 1, CONV_CH), F32),
                 jax.ShapeDtypeStruct((batch, POOL_HIST, D_MODEL), F32)]
    out_specs = [row_spec(D_MODEL)] * 3 + [row_spec(128)] * 2 + [row_spec(D_MODEL)] * 4 + [cnew_spec, pnew_spec]
    return pl.pallas_call(
        functools.partial(_mixin_prompt_body, rows=tile),
        grid=(batch, nt), in_specs=in_specs, out_specs=out_specs, out_shape=out_shape,
        scratch_shapes=[pltpu.VMEM((8 + tile, CONV_CH), F32), pltpu.VMEM((16 + tile, D_MODEL), F32)],
        compiler_params=pltpu.CompilerParams(dimension_semantics=("arbitrary", "arbitrary"),
                                             vmem_limit_bytes=VMEM_LIMIT),
        name="mixin_prompt",
    )(x, nm, w_big, cw, alog, dtb, pw, ps)


def _mixin_sample_body(x_ref, nm_ref, w_ref, cw_ref, alog_ref, dtb_ref, pw_ref, ps_ref, chist_ref, phist_ref,
                       q_ref, k_ref, v_ref, g_ref, b_ref, zs_ref, ypg_ref, sga_ref, sgb_ref, cnew_ref, pnew_ref,
                       *, steps, bb, pos0):
    rows = steps * bb
    h = _rmsnorm(x_ref[...].reshape(rows, D_MODEL), nm_ref[...]).astype(BF16)

    def delayed(new, hist_ref, nhist, d, ls_new, ls_hist):
        parts = []
        for t in range(steps):
            src = t - d
            parts.append(new[src * bb:(src + 1) * bb, ls_new] if src >= 0 else hist_ref[nhist + src, :, ls_hist])
        return jnp.concatenate(parts, axis=0)

    def conv_tap(j, c0, xc):
        return delayed(xc, chist_ref, CONV_W - 1, CONV_W - 1 - j, slice(None), slice(c0, c0 + D_MODEL))

    def pool_sums(u):
        out = []
        for gi, win in enumerate(POOL_WINDOWS):
            ls = slice(gi * POOL_GC, (gi + 1) * POOL_GC)
            acc = u[:, ls]
            for s in range(1, win):
                acc = acc + delayed(u, phist_ref, POOL_HIST, s, ls, ls)
            out.append(acc)
        return out

    def store(ref, val, ls):
        val = val.reshape(steps, bb, val.shape[-1]).astype(ref.dtype)
        if ls is None:
            ref[...] = val
        else:
            ref[:, :, ls] = val

    pos = pos0 + lax.broadcasted_iota(jnp.int32, (rows, 1), 0) // bb
    outs = (q_ref, k_ref, v_ref, g_ref, b_ref, zs_ref, ypg_ref, sga_ref, sgb_ref)
    xcs, u = _mixin_core(h, w_ref, cw_ref, alog_ref, dtb_ref, pw_ref, ps_ref, outs, pos, conv_tap, pool_sums, store)
    for i in range(CONV_W - 1):
        src = steps + i - (CONV_W - 1)
        for seg in range(3):
            ls = slice(seg * D_MODEL, (seg + 1) * D_MODEL)
            cnew_ref[i, :, ls] = (xcs[seg][src * bb:(src + 1) * bb, :] if src >= 0
                                  else chist_ref[CONV_W - 1 + src, :, ls])
    for i in range(POOL_HIST):
        src = steps + i - POOL_HIST
        pnew_ref[i] = u[src * bb:(src + 1) * bb, :] if src >= 0 else phist_ref[POOL_HIST + src]


def _mixin_sample(li, x, nm, w_big, cw, alog, dtb, pw, ps, chist, phist, *, bb, pos0):
    steps, nb, _ = x.shape
    slab_spec = lambda n, c: pl.BlockSpec((n, bb, c), lambda i: (0, i, 0))
    in_specs = ([slab_spec(steps, D_MODEL)] + _mixin_weight_specs(li)
                + [slab_spec(CONV_W - 1, CONV_CH), slab_spec(POOL_HIST, D_MODEL)])
    big = jax.ShapeDtypeStruct((steps, nb, D_MODEL), F32)
    small = jax.ShapeDtypeStruct((steps, nb, 128), F32)
    gate = jax.ShapeDtypeStruct((steps, nb, D_MODEL), BF16)
    out_shape = [big, big, big, small, small, gate, gate, gate, gate,
                 jax.ShapeDtypeStruct(chist.shape, F32), jax.ShapeDtypeStruct(phist.shape, F32)]
    out_specs = ([slab_spec(steps, D_MODEL)] * 3 + [slab_spec(steps, 128)] * 2 + [slab_spec(steps, D_MODEL)] * 4
                 + [slab_spec(CONV_W - 1, CONV_CH), slab_spec(POOL_HIST, D_MODEL)])
    return pl.pallas_call(
        functools.partial(_mixin_sample_body, steps=steps, bb=bb, pos0=pos0),
        grid=(nb // bb,), in_specs=in_specs, out_specs=out_specs, out_shape=out_shape,
        compiler_params=pltpu.CompilerParams(dimension_semantics=("arbitrary",), vmem_limit_bytes=VMEM_LIMIT),
        name="mixin_sample",
    )(x, nm, w_big, cw, alog, dtb, pw, ps, chist, phist)


def _delta_body(q_ref, k_ref, v_ref, g_ref, b_ref, o_ref, sout_ref, s_ref, *, tile, nseq):
    t = pl.program_id(1)

    @pl.when(t == 0)
    def _():
        s_ref[...] = jnp.zeros(s_ref.shape, F32)

    r = lax.broadcasted_iota(jnp.int32, (BLOCK, BLOCK), 0)
    c = lax.broadcasted_iota(jnp.int32, (BLOCK, BLOCK), 1)
    same = (r // CHUNK) == (c // CHUNK)
    low = same & (r >= c)
    strict = same & (r > c)
    eye = jnp.where(r == c, 1.0, 0.0).astype(F32)
    l_blk = jnp.where(low, 1.0, 0.0).astype(BF16)
    ones0 = jnp.where(c < CHUNK, 1.0, 0.0).astype(BF16)
    ones1 = jnp.where(c >= CHUNK, 1.0, 0.0).astype(BF16)
    first = lax.broadcasted_iota(jnp.int32, (BLOCK, 1), 0) < CHUNK
    zeros_half = jnp.zeros((CHUNK, HEAD_DIM), F32)
    cat = jnp.concatenate
    col = lambda a, hh: a[:, hh:hh + 1]
    units = [(sq, hh) for sq in range(nseq) for hh in range(N_HEADS)]
    n_units = range(len(units))
    ls = [slice(hh * HEAD_DIM, (hh + 1) * HEAD_DIM) for hh in range(N_HEADS)]

    def block(i, carry):
        rows = pl.ds(pl.multiple_of(i * BLOCK, BLOCK), BLOCK)
        gp = [g_ref[sq, rows, :] for sq in range(nseq)]
        bt = [b_ref[sq, rows, :] for sq in range(nseq)]
        g_cum = [_mm_exact_rhs(l_blk, x) for x in gp]
        tot0 = [_mm_exact_rhs(ones0, x) for x in gp]
        tot1 = [_mm_exact_rhs(ones1, x) for x in gp]
        g_t = [x.T for x in g_cum]
        e_g = [jnp.exp(x) for x in g_cum]
        e_tail = [jnp.exp(jnp.where(first, tot0[sq], tot1[sq]) - g_cum[sq]) for sq in range(nseq)]
        e_tot = ([jnp.exp(x) for x in tot0], [jnp.exp(x) for x in tot1])

        kh = [k_ref[sq, rows, ls[hh]] for sq, hh in units]
        qh = [q_ref[sq, rows, ls[hh]] for sq, hh in units]
        beta = [col(bt[sq], hh) for sq, hh in units]
        eg = [col(e_g[sq], hh) for sq, hh in units]
        decay = [jnp.exp(jnp.where(low, col(g_cum[sq], hh) - g_t[sq][hh:hh + 1, :], -jnp.inf)) for sq, hh in units]
        kq = [_mm(cat([kh[n], qh[n]], axis=0), kh[n], nt=True) for n in n_units]
        p = [-jnp.where(strict, kq[n][:BLOCK] * decay[n] * beta[n], 0.0) for n in n_units]
        qk = [kq[n][BLOCK:] * decay[n] for n in n_units]
        tinv = [eye + p[n] for n in n_units]
        p = [_mm(p[n], p[n]) for n in n_units]
        for _ in range(4):
            pp = [_mm(p[n], cat([p[n], tinv[n]], axis=1)) for n in n_units]
            p = [pp[n][:, :BLOCK] for n in n_units]
            tinv = [tinv[n] + pp[n][:, BLOCK:] for n in n_units]
        tinv = [tinv[n] + _mm(p[n], tinv[n]) for n in n_units]
        sol = [_mm(tinv[n], cat([v_ref[sq, rows, ls[hh]] * beta[n], kh[n] * (beta[n] * eg[n])], axis=1))
               for n, (sq, hh) in enumerate(units)]
        wv = [sol[n][:, :HEAD_DIM] for n in n_units]
        wk = [sol[n][:, HEAD_DIM:] for n in n_units]
        qd = [qh[n] * eg[n] for n in n_units]
        kt_t = [(kh[n] * col(e_tail[sq], hh)).T for n, (sq, hh) in enumerate(units)]
        s_cur = [s_ref[sq, hh] for sq, hh in units]
        o_parts = []
        for half in range(2):
            hs = slice(half * CHUNK, (half + 1) * CHUNK)
            res = [_mm(cat([wk[n][hs], qd[n][hs]], axis=0), s_cur[n]) for n in n_units]
            u_new = [wv[n][hs] - res[n][:CHUNK] for n in n_units]
            u_pad = [cat([u_new[n], zeros_half] if half == 0 else [zeros_half, u_new[n]], axis=0) for n in n_units]
            upd = [_mm(cat([qk[n][hs], kt_t[n]], axis=0), u_pad[n]) for n in n_units]
            o_parts.append([res[n][CHUNK:] + upd[n][:CHUNK] for n in n_units])
            s_cur = [s_cur[n] * col(e_tot[half][sq], hh) + upd[n][CHUNK:] for n, (sq, hh) in enumerate(units)]
        for n, (sq, hh) in enumerate(units):
            s_ref[sq, hh] = s_cur[n]
            o_ref[sq, rows, ls[hh]] = cat([o_parts[0][n], o_parts[1][n]], axis=0)
        return carry

    lax.fori_loop(0, tile // BLOCK, block, 0)
    sout_ref[...] = s_ref[...]


def _delta_prompt(q, k, v, g, beta, *, batch, seq, tile, nseq):
    as3d = lambda a: a.reshape(batch, seq, a.shape[-1])
    row_spec = lambda c: pl.BlockSpec((nseq, tile, c), lambda b, t: (b, t, 0))
    s_shape = (N_HEADS, HEAD_DIM, HEAD_DIM)
    o, s_new = pl.pallas_call(
        functools.partial(_delta_body, tile=tile, nseq=nseq),
        grid=(batch // nseq, seq // tile),
        in_specs=[row_spec(D_MODEL)] * 3 + [row_spec(128)] * 2,
        out_specs=[row_spec(D_MODEL), pl.BlockSpec((nseq,) + s_shape, lambda b, t: (b, 0, 0, 0))],
        out_shape=[jax.ShapeDtypeStruct((batch, seq, D_MODEL), F32), jax.ShapeDtypeStruct((batch,) + s_shape, F32)],
        scratch_shapes=[pltpu.VMEM((nseq,) + s_shape, F32)],
        compiler_params=pltpu.CompilerParams(dimension_semantics=("arbitrary", "arbitrary"),
                                             vmem_limit_bytes=VMEM_LIMIT),
        name="delta_prompt",
    )(as3d(q), as3d(k), as3d(v), as3d(g), as3d(beta))
    return o.reshape(batch * seq, D_MODEL), s_new


def _head_indicators():
    d = lax.broadcasted_iota(jnp.int32, (D_MODEL, 128), 0) // HEAD_DIM
    hcol = lax.broadcasted_iota(jnp.int32, (D_MODEL, 128), 1)
    e_sum = jnp.where(d == hcol, 1.0, 0.0).astype(BF16)
    hrow = lax.broadcasted_iota(jnp.int32, (128, D_MODEL), 0)
    d2 = lax.broadcasted_iota(jnp.int32, (128, D_MODEL), 1) // HEAD_DIM
    e_bc = jnp.where(hrow == d2, 1.0, 0.0).astype(BF16)
    return e_sum, e_bc


def _sample_prep_body(q_ref, k_ref, v_ref, g_ref, b_ref, wkqd_ref, wv_ref, kt_ref, qkd_ref, glx_ref,
                      *, steps, nb):
    e_sum, e_bc = _head_indicators()
    expand = lambda x: _mm_exact_lhs(x, e_bc)
    hsum = lambda y: _mm_exact_lhs(y, e_sum)
    sl = lambda ref, i: ref[i * nb:(i + 1) * nb, :]
    q = [sl(q_ref, i) for i in range(steps)]
    k = [sl(k_ref, i) for i in range(steps)]
    v = [sl(v_ref, i) for i in range(steps)]
    beta = [sl(b_ref, i) for i in range(steps)]
    g_cum = []
    for i in range(steps):
        gi = sl(g_ref, i)
        g_cum.append(gi if i == 0 else g_cum[-1] + gi)
    wv, wk = [], []
    for i in range(steps):
        acc_v = v[i] * expand(beta[i])
        acc_k = k[i] * expand(beta[i] * jnp.exp(g_cum[i]))
        for j in range(i):
            a_ij = expand(hsum(k[i] * k[j]) * jnp.exp(g_cum[i] - g_cum[j]) * beta[i])
            acc_v = acc_v - a_ij * wv[j]
            acc_k = acc_k - a_ij * wk[j]
        wv.append(acc_v)
        wk.append(acc_k)
    zeros = jnp.zeros((nb, D_MODEL), F32)
    for i in range(steps):
        wkqd_ref[:, i, :] = wk[i]
        wkqd_ref[:, steps + i, :] = q[i] * expand(jnp.exp(g_cum[i]))
        wv_ref[:, i, :] = wv[i]
        wv_ref[:, steps + i, :] = zeros
        kt_ref[:, i, :] = k[i] * expand(jnp.exp(g_cum[steps - 1] - g_cum[i]))
        kt_ref[:, steps + i, :] = zeros
        for j in range(steps):
            idx = i * steps + j
            if j <= i:
                qkd_ref[idx * nb:(idx + 1) * nb, :] = hsum(q[i] * k[j]) * jnp.exp(g_cum[i] - g_cum[j])
            else:
                qkd_ref[idx * nb:(idx + 1) * nb, :] = jnp.zeros((nb, 128), F32)
    glx_ref[...] = expand(jnp.exp(g_cum[steps - 1]))


def _sample_prep(q, k, v, g, beta, *, steps, nb):
    m = steps * nb
    out_shape = [jax.ShapeDtypeStruct((nb, 2 * steps, D_MODEL), F32)] * 3 + [
        jax.ShapeDtypeStruct((steps * steps * nb, 128), F32), jax.ShapeDtypeStruct((nb, D_MODEL), F32)]
    return pl.pallas_call(
        functools.partial(_sample_prep_body, steps=steps, nb=nb),
        grid=(1,),
        in_specs=[_full((m, D_MODEL))] * 3 + [_full((m, 128))] * 2,
        out_specs=[_full(s.shape) for s in out_shape],
        out_shape=out_shape,
        compiler_params=pltpu.CompilerParams(dimension_semantics=("arbitrary",), vmem_limit_bytes=VMEM_LIMIT),
        name="sample_prep",
    )(q, k, v, g, beta)


def _sample_state_body(*refs, bb, group):
    s_ref, wkqd_ref, wv_ref, kt_ref, glx_ref = refs[:5]
    r_ref, snew_ref = refs[-2:]
    zeros_pad = jnp.zeros((HEAD_DIM - 8, HEAD_DIM), F32)

    def per_group(gi, carry):
        units = [(gi * group + j, hh) for j in range(group) for hh in range(N_HEADS)]
        ls = [slice(hh * HEAD_DIM, (hh + 1) * HEAD_DIM) for hh in range(N_HEADS)]
        pad = lambda a, n: jnp.concatenate([a, zeros_pad[:n - 8]], axis=0)
        s0 = [s_ref[b, hh] for b, hh in units]
        res = [_mm(pad(wkqd_ref[b, :, ls[hh]], 16), s0[n])[:8] for n, (b, hh) in enumerate(units)]
        kt_t = [pad(kt_ref[b, :, ls[hh]], HEAD_DIM).T for b, hh in units]
        upd = [_mm(kt_t[n], pad(wv_ref[b, :, ls[hh]] - res[n], HEAD_DIM)) for n, (b, hh) in enumerate(units)]
        for n, (b, hh) in enumerate(units):
            r_ref[b, :, ls[hh]] = res[n]
            snew_ref[b, hh] = s0[n] * glx_ref[b, :, ls[hh]] + upd[n]
        return carry

    lax.fori_loop(0, bb // group, per_group, 0)


def _sample_state(li, state, wkqd, wv, kt, glx, prev, *, bb):
    nb = state.shape[1]
    s_spec = pl.BlockSpec((None, bb, N_HEADS, HEAD_DIM, HEAD_DIM), lambda i: (li, i, 0, 0, 0))
    slot_spec = pl.BlockSpec((bb, 8, D_MODEL), lambda i: (i, 0, 0))
    in_specs = [s_spec, slot_spec, slot_spec, slot_spec, pl.BlockSpec((bb, 1, D_MODEL), lambda i: (i, 0, 0))]
    args = [state, wkqd, wv, kt, glx.reshape(nb, 1, D_MODEL)]
    aliases = {}
    if prev is not None:
        in_specs.append(pl.BlockSpec(memory_space=pl.ANY))
        args.append(prev)
        aliases = {len(args) - 1: 1}
    return pl.pallas_call(
        functools.partial(_sample_state_body, bb=bb, group=2),
        grid=(nb // bb,),
        in_specs=in_specs,
        out_specs=[slot_spec, s_spec],
        out_shape=[jax.ShapeDtypeStruct((nb, 8, D_MODEL), F32), jax.ShapeDtypeStruct(state.shape, F32)],
        input_output_aliases=aliases,
        compiler_params=pltpu.CompilerParams(dimension_semantics=("arbitrary",), vmem_limit_bytes=VMEM_LIMIT),
        name="sample_state",
    )(*args)


def _sample_out_body(r_ref, wv_ref, qkd_ref, o_ref, *, steps, nb):
    _, e_bc = _head_indicators()
    sl = lambda ref, i: ref[i * nb:(i + 1) * nb, :]
    u = [wv_ref[:, j, :] - r_ref[:, j, :] for j in range(steps)]
    for i in range(steps):
        acc = r_ref[:, steps + i, :]
        for j in range(i + 1):
            acc = acc + _mm_exact_lhs(sl(qkd_ref, i * steps + j), e_bc) * u[j]
        o_ref[i * nb:(i + 1) * nb, :] = acc


def _sample_out(r, wv, qkd, *, steps, nb):
    m = steps * nb
    return pl.pallas_call(
        functools.partial(_sample_out_body, steps=steps, nb=nb),
        grid=(1,),
        in_specs=[_full(r.shape), _full(wv.shape), _full(qkd.shape)],
        out_specs=_full((m, D_MODEL)),
        out_shape=jax.ShapeDtypeStruct((m, D_MODEL), F32),
        compiler_params=pltpu.CompilerParams(dimension_semantics=("arbitrary",), vmem_limit_bytes=VMEM_LIMIT),
        name="sample_out",
    )(r, wv, qkd)


def _merge_body(o_ref, zs_ref, ypg_ref, sga_ref, sgb_ref, x_ref, p_ref, gn_ref, wpa_ref, wpb_ref, wout_ref,
                npl_ref, wpg_ref, wpp_ref, fn_ref, y_ref, *, final, parts):
    step = o_ref.shape[0] // parts
    groups = [slice(i * step, (i + 1) * step) for i in range(parts)]
    gn = gn_ref[...]
    y_a = []
    for rs in groups:
        gated = []
        for hh in range(N_HEADS):
            ls = slice(hh * HEAD_DIM, (hh + 1) * HEAD_DIM)
            oh = o_ref[rs, ls]
            on = oh * lax.rsqrt(jnp.mean(oh * oh, axis=-1, keepdims=True) + EPS) * gn
            gated.append((on * zs_ref[rs, ls]).astype(BF16))
        y_a.append(_dot(jnp.concatenate(gated, axis=1), wpa_ref[...]))
    y_b = [_mm(ypg_ref[rs, :], wpb_ref[...]) for rs in groups]
    m = [sga_ref[rs, :] * y_a[i] + sgb_ref[rs, :] * y_b[i] for i, rs in enumerate(groups)]
    x1 = [x_ref[rs, :] + _mm(m[i], wout_ref[...]) for i, rs in enumerate(groups)]
    gate = [jax.nn.sigmoid(_mm(_rmsnorm(x1[i], npl_ref[...]), wpg_ref[...])) for i in range(parts)]
    pe = [_mm(p_ref[rs, :], wpp_ref[...]) for rs in groups]
    for i, rs in enumerate(groups):
        x2 = x1[i] + gate[i] * pe[i]
        if final:
            x2 = _rmsnorm(x2, fn_ref[...])
        y_ref[rs, :] = x2


def _merge(li, o, zs, ypg, sga, sgb, x, p, gn, wpa, wpb, wout, npl, wpg, wpp, fn, *, tile, final):
    m = x.shape[0]
    ple = p.shape[-1]
    row_spec = lambda c: pl.BlockSpec((tile, c), lambda i: (i, 0))
    sq = _layer(li, (D_MODEL, D_MODEL))
    return pl.pallas_call(
        functools.partial(_merge_body, final=final, parts=2),
        grid=(m // tile,),
        in_specs=[row_spec(D_MODEL)] * 6 + [pl.BlockSpec((None, tile, ple), lambda i: (li, i, 0)),
                                            _layer(li, (1, HEAD_DIM)), sq, sq, sq, _layer(li, (1, D_MODEL)),
                                            sq, _layer(li, (ple, D_MODEL)), _full((1, D_MODEL))],
        out_specs=row_spec(D_MODEL),
        out_shape=jax.ShapeDtypeStruct((m, D_MODEL), F32),
        compiler_params=pltpu.CompilerParams(dimension_semantics=("arbitrary",), vmem_limit_bytes=VMEM_LIMIT),
        name="merge",
    )(o, zs, ypg, sga, sgb, x, p, gn, wpa, wpb, wout, npl, wpg, wpp, fn)


def kernel(x_prompt, x_sample, p_prompt, p_sample, state_conv, state_delta, state_pool, norm_mix, w_in, conv_w,
           a_log, dt_bias, gdn_norm, w_proj_a, pool_w, pool_scale, w_proj_b, w_out, norm_ple, w_ple_gate,
           w_ple_proj, final_norm):
    depth = w_in.shape[0]
    batch, seq, _ = x_prompt.shape
    nb, steps, _ = x_sample.shape
    rowvec = lambda a: a.reshape(depth, 1, -1)
    pad128 = lambda a: jnp.pad(a, ((0, 0), (0, 128 - a.shape[1]))).reshape(depth, 1, 128)
    mix_w = (rowvec(norm_mix), _wprep(w_in, tk=128), conv_w, pad128(a_log), pad128(dt_bias),
             pool_w.astype(BF16), rowvec(pool_scale))
    merge_w = (rowvec(gdn_norm), w_proj_a.astype(BF16), w_proj_b.astype(BF16), w_out.astype(BF16),
               rowvec(norm_ple), w_ple_gate.astype(BF16), w_ple_proj.astype(BF16), final_norm.reshape(1, -1))

    xp = x_prompt.reshape(batch * seq, D_MODEL)
    pp = p_prompt.reshape(depth, batch * seq, -1)
    conv_p, delta_p, pool_p = [], [], []
    for li in range(depth):
        q, k, v, g, beta, zs, ypg, sga, sgb, cnew, pnew = _mixin_prompt(li, xp, *mix_w, batch=batch, seq=seq, tile=256)
        o, s_new = _delta_prompt(q, k, v, g, beta, batch=batch, seq=seq, tile=256, nseq=4)
        xp = _merge(li, o, zs, ypg, sga, sgb, xp, pp, *merge_w, tile=512, final=(li == depth - 1))
        conv_p.append(cnew)
        delta_p.append(s_new)
        pool_p.append(pnew)
    y_prompt = xp.reshape(batch, seq, D_MODEL)

    swap = lambda a: jnp.swapaxes(a, 0, 1)
    flat = lambda a: a.reshape(-1, a.shape[-1])
    xs = swap(x_sample)
    ps = jnp.swapaxes(p_sample, 1, 2).reshape(depth, steps * nb, -1)
    conv_s, delta_s, pool_s = [], None, []
    for li in range(depth):
        q, k, v, g, beta, zs, ypg, sga, sgb, cnew, pnew = _mixin_sample(
            li, xs, *mix_w, swap(state_conv[li]), swap(state_pool[li]), bb=32, pos0=PAST_LEN)
        wkqd, wv, kt, qkd, glx = _sample_prep(flat(q), flat(k), flat(v), flat(g), flat(beta), steps=steps, nb=nb)
        r, delta_s = _sample_state(li, state_delta, wkqd, wv, kt, glx, delta_s, bb=8)
        o = _sample_out(r, wv, qkd, steps=steps, nb=nb)
        xs = _merge(li, o, flat(zs), flat(ypg), flat(sga), flat(sgb), flat(xs), ps, *merge_w, tile=steps * nb,
                    final=(li == depth - 1)).reshape(steps, nb, D_MODEL)
        conv_s.append(swap(cnew))
        pool_s.append(swap(pnew))
    y_sample = swap(xs)

    return (y_prompt, y_sample, jnp.stack(conv_p), jnp.stack(delta_p), jnp.stack(pool_p),
            jnp.stack(conv_s), delta_s, jnp.stack(pool_s))
```

```python
import functools

import jax
import jax.numpy as jnp
from jax import lax
from jax.experimental import pallas as pl
from jax.experimental.pallas import tpu as pltpu

F32 = jnp.float32
BF16 = jnp.bfloat16

D_MODEL = 1024
N_HEADS = 8
HEAD_DIM = 128
CONV_W = 4
CONV_CH = 3 * D_MODEL
POOL_WINDOWS = (2, 4, 8, 16)
POOL_HIST = 15
POOL_GC = 256
EPS = 1e-6
PAST_LEN = 16384
CHUNK = 64
BLOCK = 2 * CHUNK
VMEM_LIMIT = 56 * 1024 * 1024

_SPLIT = 4096
_C_QKV, _C_Z, _C_U, _C_GP, _C_GA, _C_GB, _C_A, _C_B = 0, 3072, 4096, 5120, 6144, 7168, 8192, 8320
_W_COLS = 8448


def _dot(a, b, nt=False):
    dims = (((1,), (1,)), ((), ())) if nt else (((1,), (0,)), ((), ()))
    return lax.dot_general(a, b, dims, preferred_element_type=F32)


def _mm(a, b, nt=False):
    return _dot(a.astype(BF16), b.astype(BF16), nt)


def _split3(x):
    x0 = x.astype(BF16)
    r = x - x0.astype(F32)
    x1 = r.astype(BF16)
    x2 = (r - x1.astype(F32)).astype(BF16)
    return x0, x1, x2


def _mm_exact_rhs(a_bf, b):
    b0, b1, b2 = _split3(b)
    return _dot(a_bf, b0) + _dot(a_bf, b1) + _dot(a_bf, b2)


def _mm_exact_lhs(a, b_bf):
    a0, a1, a2 = _split3(a)
    return _dot(a0, b_bf) + _dot(a1, b_bf) + _dot(a2, b_bf)


def _silu(x):
    return x * jax.nn.sigmoid(x)


def _rmsnorm(x, w):
    return x * lax.rsqrt(jnp.mean(x * x, axis=-1, keepdims=True) + EPS) * w


def _full(shape):
    n = len(shape)
    return pl.BlockSpec(shape, lambda *_: (0,) * n, pipeline_mode=pl.Buffered(1))


def _layer(li, shape):
    n = len(shape)
    return pl.BlockSpec((None,) + tuple(shape), lambda *_: (li,) + (0,) * n, pipeline_mode=pl.Buffered(1))


def _wprep_body(wt_ref, o_ref):
    n_tail = wt_ref.shape[0] - _SPLIT - 2 * N_HEADS
    for g in range(_SPLIT // 128):
        o_ref[:, g * 128:(g + 1) * 128] = wt_ref[g * 128:(g + 1) * 128, :].T.astype(BF16)
    for g in range(n_tail // 128):
        r0 = _SPLIT + 2 * N_HEADS + g * 128
        o_ref[:, _C_U + g * 128:_C_U + (g + 1) * 128] = wt_ref[r0:r0 + 128, :].T.astype(BF16)
    ab = wt_ref[_SPLIT:_SPLIT + 128, :].T
    lane = lax.broadcasted_iota(jnp.int32, ab.shape, 1)
    o_ref[:, _C_A:_C_A + 128] = jnp.where(lane < N_HEADS, ab, 0.0).astype(BF16)
    o_ref[:, _C_B:_C_B + 128] = jnp.where(lane < N_HEADS, pltpu.roll(ab, 128 - N_HEADS, 1), 0.0).astype(BF16)


def _wprep(w_in, *, tk):
    depth, k, n = w_in.shape
    return pl.pallas_call(
        _wprep_body, grid=(depth, k // tk),
        in_specs=[pl.BlockSpec((None, n, tk), lambda l, r: (l, 0, r))],
        out_specs=pl.BlockSpec((None, tk, _W_COLS), lambda l, r: (l, r, 0)),
        out_shape=jax.ShapeDtypeStruct((depth, k, _W_COLS), BF16),
        compiler_params=pltpu.CompilerParams(dimension_semantics=("arbitrary", "arbitrary"),
                                             vmem_limit_bytes=VMEM_LIMIT),
        name="wprep",
    )(jnp.swapaxes(w_in, 1, 2))


def _mixin_core(h, w_ref, cw_ref, alog_ref, dtb_ref, pw_ref, ps_ref, outs, pos, conv_tap, pool_sums, store):
    q_ref, k_ref, v_ref, g_ref, b_ref, zs_ref, ypg_ref, sga_ref, sgb_ref = outs
    proj = lambda c0, n=D_MODEL: _dot(h, w_ref[:, c0:c0 + n])
    xcs = [proj(_C_QKV + seg * D_MODEL) for seg in range(3)]
    gates = [proj(_C_Z), proj(_C_GA), proj(_C_GB)]
    a_r = proj(_C_A, 128)
    b_r = proj(_C_B, 128)
    u = proj(_C_U)
    gp = proj(_C_GP)

    for seg, out_ref in enumerate((q_ref, k_ref, v_ref)):
        c0 = seg * D_MODEL
        xc = xcs[seg]
        acc = xc * cw_ref[CONV_W - 1:CONV_W, c0:c0 + D_MODEL]
        for j in range(CONV_W - 1):
            acc = acc + conv_tap(j, c0, xc) * cw_ref[j:j + 1, c0:c0 + D_MODEL]
        y = _silu(acc)
        if seg < 2:
            for hh in range(N_HEADS):
                ls = slice(hh * HEAD_DIM, (hh + 1) * HEAD_DIM)
                yh = y[:, ls]
                inv = lax.rsqrt(jnp.sum(yh * yh, axis=-1, keepdims=True) + EPS)
                if seg == 0:
                    inv = inv * (HEAD_DIM ** -0.5)
                store(out_ref, yh * inv, ls)
        else:
            store(out_ref, y, None)

    store(zs_ref, _silu(gates[0]), None)
    store(sga_ref, jax.nn.sigmoid(gates[1]), None)
    store(sgb_ref, jax.nn.sigmoid(gates[2]), None)
    xs = a_r + dtb_ref[...]
    softplus = jnp.maximum(xs, 0.0) + jnp.log1p(jnp.exp(-jnp.abs(xs)))
    store(g_ref, -jnp.exp(alog_ref[...]) * softplus, None)
    store(b_ref, jax.nn.sigmoid(b_r), None)

    sgp = _silu(gp)
    sums = pool_sums(u)
    for gi, win in enumerate(POOL_WINDOWS):
        l0 = gi * POOL_GC
        ls = slice(l0, l0 + POOL_GC)
        count = jnp.minimum(pos + 1, win).astype(F32)
        y = sums[gi] / count - u[:, ls]
        yp = _mm(y, pw_ref[gi]) * ps_ref[:, ls]
        store(ypg_ref, yp * sgp[:, ls], ls)
    return xcs, u


def _mixin_weight_specs(li):
    return [_layer(li, (1, D_MODEL)), _layer(li, (D_MODEL, _W_COLS)), _layer(li, (CONV_W, CONV_CH)),
            _layer(li, (1, 128)), _layer(li, (1, 128)), _layer(li, (4, POOL_GC, POOL_GC)), _layer(li, (1, D_MODEL))]


def _mixin_prompt_body(x_ref, nm_ref, w_ref, cw_ref, alog_ref, dtb_ref, pw_ref, ps_ref,
                       q_ref, k_ref, v_ref, g_ref, b_ref, zs_ref, ypg_ref, sga_ref, sgb_ref, cnew_ref, pnew_ref,
                       cext, pext, *, rows):
    hc, hp = 8, 16
    t = pl.program_id(1)

    @pl.when(t == 0)
    def _():
        cext[0:hc, :] = jnp.zeros((hc, CONV_CH), F32)
        pext[0:hp, :] = jnp.zeros((hp, D_MODEL), F32)

    h = _rmsnorm(x_ref[...], nm_ref[...]).astype(BF16)

    def conv_tap(j, c0, xc):
        if j == 0:
            cext[hc:hc + rows, c0:c0 + D_MODEL] = xc
        off = hc - (CONV_W - 1 - j)
        return cext[off:off + rows, c0:c0 + D_MODEL]

    def pool_sums(u):
        pext[hp:hp + rows, :] = u
        lvl = pext[...]
        out = []
        for gi, win in enumerate(POOL_WINDOWS):
            lvl = lvl + pltpu.roll(lvl, win // 2, 0)
            out.append(lvl[hp:, :POOL_GC])
            lvl = lvl[:, POOL_GC:]
        return out

    def store(ref, val, ls):
        if ls is None:
            ref[...] = val.astype(ref.dtype)
        else:
            ref[:, ls] = val.astype(ref.dtype)

    pos = t * rows + lax.broadcasted_iota(jnp.int32, (rows, 1), 0)
    outs = (q_ref, k_ref, v_ref, g_ref, b_ref, zs_ref, ypg_ref, sga_ref, sgb_ref)
    _mixin_core(h, w_ref, cw_ref, alog_ref, dtb_ref, pw_ref, ps_ref, outs, pos, conv_tap, pool_sums, store)
    cnew_ref[...] = cext[hc + rows - (CONV_W - 1):hc + rows, :]
    pnew_ref[...] = pext[hp + rows - POOL_HIST:hp + rows, :]
    cext[0:hc, :] = cext[rows:rows + hc, :]
    pext[0:hp, :] = pext[rows:rows + hp, :]


def _mixin_prompt(li, x, nm, w_big, cw, alog, dtb, pw, ps, *, batch, seq, tile):
    m = x.shape[0]
    nt = seq // tile
    row_spec = lambda c: pl.BlockSpec((tile, c), lambda b, t: (b * nt + t, 0))
    in_specs = [row_spec(D_MODEL)] + _mixin_weight_specs(li)
    cnew_spec = pl.BlockSpec((None, CONV_W - 1, CONV_CH), lambda b, t: (b, 0, 0))
    pnew_spec = pl.BlockSpec((None, POOL_HIST, D_MODEL), lambda b, t: (b, 0, 0))
    big = jax.ShapeDtypeStruct((m, D_MODEL), F32)
    small = jax.ShapeDtypeStruct((m, 128), F32)
    gate = jax.ShapeDtypeStruct((m, D_MODEL), BF16)
    out_shape = [big, big, big, small, small, gate, gate, gate, gate,
                 jax.ShapeDtypeStruct((batch, CONV_W - 1, CONV_CH), F32),
                 jax.ShapeDtypeStruct((batch, POOL_HIST, D_MODEL), F32)]
    out_specs = [row_spec(D_MODEL)] * 3 + [row_spec(128)] * 2 + [row_spec(D_MODEL)] * 4 + [cnew_spec, pnew_spec]
    return pl.pallas_call(
        functools.partial(_mixin_prompt_body, rows=tile),
        grid=(batch, nt), in_specs=in_specs, out_specs=out_specs, out_shape=out_shape,
        scratch_shapes=[pltpu.VMEM((8 + tile, CONV_CH), F32), pltpu.VMEM((16 + tile, D_MODEL), F32)],
        compiler_params=pltpu.CompilerParams(dimension_semantics=("arbitrary", "arbitrary"),
                                             vmem_limit_bytes=VMEM_LIMIT),
        name="mixin_prompt",
    )(x, nm, w_big, cw, alog, dtb, pw, ps)


def _mixin_sample_body(*refs, steps, bb, pos0):
    x_ref, nm_ref, w_ref, cw_ref, alog_ref, dtb_ref, pw_ref, ps_ref, chist_ref, phist_ref = refs[:10]
    q_ref, k_ref, v_ref, g_ref, b_ref, zs_ref, ypg_ref, sga_ref, sgb_ref, cnew_ref, pnew_ref = refs[-11:]
    rows = steps * bb
    h = _rmsnorm(x_ref[...].reshape(rows, D_MODEL), nm_ref[...]).astype(BF16)

    def delayed(new, hist_ref, nhist, d, ls_new, ls_hist):
        parts = []
        for t in range(steps):
            src = t - d
            parts.append(new[src * bb:(src + 1) * bb, ls_new] if src >= 0 else hist_ref[nhist + src, :, ls_hist])
        return jnp.concatenate(parts, axis=0)

    def conv_tap(j, c0, xc):
        return delayed(xc, chist_ref, CONV_W - 1, CONV_W - 1 - j, slice(None), slice(c0, c0 + D_MODEL))

    def pool_sums(u):
        out = []
        for gi, win in enumerate(POOL_WINDOWS):
            ls = slice(gi * POOL_GC, (gi + 1) * POOL_GC)
            acc = u[:, ls]
            for s in range(1, win):
                acc = acc + delayed(u, phist_ref, POOL_HIST, s, ls, ls)
            out.append(acc)
        return out

    def store(ref, val, ls):
        val = val.reshape(steps, bb, val.shape[-1]).astype(ref.dtype)
        if ls is None:
            ref[...] = val
        else:
            ref[:, :, ls] = val

    pos = pos0 + lax.broadcasted_iota(jnp.int32, (rows, 1), 0) // bb
    outs = (q_ref, k_ref, v_ref, g_ref, b_ref, zs_ref, ypg_ref, sga_ref, sgb_ref)
    xcs, u = _mixin_core(h, w_ref, cw_ref, alog_ref, dtb_ref, pw_ref, ps_ref, outs, pos, conv_tap, pool_sums, store)
    for i in range(CONV_W - 1):
        src = steps + i - (CONV_W - 1)
        for seg in range(3):
            ls = slice(seg * D_MODEL, (seg + 1) * D_MODEL)
            cnew_ref[i, :, ls] = (xcs[seg][src * bb:(src + 1) * bb, :] if src >= 0
                                  else chist_ref[CONV_W - 1 + src, :, ls])
    for i in range(POOL_HIST):
        src = steps + i - POOL_HIST
        pnew_ref[i] = u[src * bb:(src + 1) * bb, :] if src >= 0 else phist_ref[POOL_HIST + src]


def _mixin_sample(li, x, nm, w_big, cw, alog, dtb, pw, ps, chist, phist, prev, *, bb, pos0):
    steps, nb, _ = x.shape
    slab_spec = lambda n, c: pl.BlockSpec((n, bb, c), lambda i: (0, i, 0))
    state_spec = lambda n, c: pl.BlockSpec((None, n, bb, c), lambda i: (li, 0, i, 0))
    in_specs = ([slab_spec(steps, D_MODEL)] + _mixin_weight_specs(li)
                + [state_spec(CONV_W - 1, CONV_CH), state_spec(POOL_HIST, D_MODEL)])
    args = [x, nm, w_big, cw, alog, dtb, pw, ps, chist, phist]
    aliases = {}
    if prev is not None:
        in_specs += [pl.BlockSpec(memory_space=pl.ANY)] * 2
        aliases = {len(args): 9, len(args) + 1: 10}
        args += list(prev)
    big = jax.ShapeDtypeStruct((steps, nb, D_MODEL), F32)
    small = jax.ShapeDtypeStruct((steps, nb, 128), F32)
    gate = jax.ShapeDtypeStruct((steps, nb, D_MODEL), BF16)
    out_shape = [big, big, big, small, small, gate, gate, gate, gate,
                 jax.ShapeDtypeStruct(chist.shape, F32), jax.ShapeDtypeStruct(phist.shape, F32)]
    out_specs = ([slab_spec(steps, D_MODEL)] * 3 + [slab_spec(steps, 128)] * 2 + [slab_spec(steps, D_MODEL)] * 4
                 + [state_spec(CONV_W - 1, CONV_CH), state_spec(POOL_HIST, D_MODEL)])
    return pl.pallas_call(
        functools.partial(_mixin_sample_body, steps=steps, bb=bb, pos0=pos0),
        grid=(nb // bb,), in_specs=in_specs, out_specs=out_specs, out_shape=out_shape,
        input_output_aliases=aliases,
        compiler_params=pltpu.CompilerParams(dimension_semantics=("arbitrary",), vmem_limit_bytes=VMEM_LIMIT),
        name="mixin_sample",
    )(*args)


def _delta_body(q_ref, k_ref, v_ref, g_ref, b_ref, o_ref, sout_ref, s_ref, *, tile, nseq):
    t = pl.program_id(1)

    @pl.when(t == 0)
    def _():
        s_ref[...] = jnp.zeros(s_ref.shape, F32)

    r = lax.broadcasted_iota(jnp.int32, (BLOCK, BLOCK), 0)
    c = lax.broadcasted_iota(jnp.int32, (BLOCK, BLOCK), 1)
    same = (r // CHUNK) == (c // CHUNK)
    low = same & (r >= c)
    strict = same & (r > c)
    eye = jnp.where(r == c, 1.0, 0.0).astype(F32)
    l_blk = jnp.where(low, 1.0, 0.0).astype(BF16)
    ones0 = jnp.where(c < CHUNK, 1.0, 0.0).astype(BF16)
    ones1 = jnp.where(c >= CHUNK, 1.0, 0.0).astype(BF16)
    first = lax.broadcasted_iota(jnp.int32, (BLOCK, 1), 0) < CHUNK
    zeros_half = jnp.zeros((CHUNK, HEAD_DIM), F32)
    cat = jnp.concatenate
    col = lambda a, hh: a[:, hh:hh + 1]
    units = [(sq, hh) for sq in range(nseq) for hh in range(N_HEADS)]
    n_units = range(len(units))
    ls = [slice(hh * HEAD_DIM, (hh + 1) * HEAD_DIM) for hh in range(N_HEADS)]

    def block(i, carry):
        rows = pl.ds(pl.multiple_of(i * BLOCK, BLOCK), BLOCK)
        gp = [g_ref[sq, rows, :] for sq in range(nseq)]
        bt = [b_ref[sq, rows, :] for sq in range(nseq)]
        g_cum = [_mm_exact_rhs(l_blk, x) for x in gp]
        tot0 = [_mm_exact_rhs(ones0, x) for x in gp]
        tot1 = [_mm_exact_rhs(ones1, x) for x in gp]
        g_t = [x.T for x in g_cum]
        e_g = [jnp.exp(x) for x in g_cum]
        e_tail = [jnp.exp(jnp.where(first, tot0[sq], tot1[sq]) - g_cum[sq]) for sq in range(nseq)]
        e_tot = ([jnp.exp(x) for x in tot0], [jnp.exp(x) for x in tot1])

        kh = [k_ref[sq, rows, ls[hh]] for sq, hh in units]
        qh = [q_ref[sq, rows, ls[hh]] for sq, hh in units]
        beta = [col(bt[sq], hh) for sq, hh in units]
        eg = [col(e_g[sq], hh) for sq, hh in units]
        decay = [jnp.exp(jnp.where(low, col(g_cum[sq], hh) - g_t[sq][hh:hh + 1, :], -jnp.inf)) for sq, hh in units]
        kq = [_mm(cat([kh[n], qh[n]], axis=0), kh[n], nt=True) for n in n_units]
        p = [-jnp.where(strict, kq[n][:BLOCK] * decay[n] * beta[n], 0.0) for n in n_units]
        qk = [kq[n][BLOCK:] * decay[n] for n in n_units]
        tinv = [eye + p[n] for n in n_units]
        p = [_mm(p[n], p[n]) for n in n_units]
        for _ in range(4):
            pp = [_mm(p[n], cat([p[n], tinv[n]], axis=1)) for n in n_units]
            p = [pp[n][:, :BLOCK] for n in n_units]
            tinv = [tinv[n] + pp[n][:, BLOCK:] for n in n_units]
        tinv = [tinv[n] + _mm(p[n], tinv[n]) for n in n_units]
        sol = [_mm(tinv[n], cat([v_ref[sq, rows, ls[hh]] * beta[n], kh[n] * (beta[n] * eg[n])], axis=1))
               for n, (sq, hh) in enumerate(units)]
        wv = [sol[n][:, :HEAD_DIM] for n in n_units]
        wk = [sol[n][:, HEAD_DIM:] for n in n_units]
        qd = [qh[n] * eg[n] for n in n_units]
        kt_t = [(kh[n] * col(e_tail[sq], hh)).T for n, (sq, hh) in enumerate(units)]
        s_cur = [s_ref[sq, hh] for sq, hh in units]
        o_parts = []
        for half in range(2):
            hs = slice(half * CHUNK, (half + 1) * CHUNK)
            res = [_mm(cat([wk[n][hs], qd[n][hs]], axis=0), s_cur[n]) for n in n_units]
            u_new = [wv[n][hs] - res[n][:CHUNK] for n in n_units]
            u_pad = [cat([u_new[n], zeros_half] if half == 0 else [zeros_half, u_new[n]], axis=0) for n in n_units]
            upd = [_mm(cat([qk[n][hs], kt_t[n]], axis=0), u_pad[n]) for n in n_units]
            o_parts.append([res[n][CHUNK:] + upd[n][:CHUNK] for n in n_units])
            s_cur = [s_cur[n] * col(e_tot[half][sq], hh) + upd[n][CHUNK:] for n, (sq, hh) in enumerate(units)]
        for n, (sq, hh) in enumerate(units):
            s_ref[sq, hh] = s_cur[n]
            o_ref[sq, rows, ls[hh]] = cat([o_parts[0][n], o_parts[1][n]], axis=0)
        return carry

    lax.fori_loop(0, tile // BLOCK, block, 0)
    sout_ref[...] = s_ref[...]


def _delta_prompt(q, k, v, g, beta, *, batch, seq, tile, nseq):
    as3d = lambda a: a.reshape(batch, seq, a.shape[-1])
    row_spec = lambda c: pl.BlockSpec((nseq, tile, c), lambda b, t: (b, t, 0))
    s_shape = (N_HEADS, HEAD_DIM, HEAD_DIM)
    o, s_new = pl.pallas_call(
        functools.partial(_delta_body, tile=tile, nseq=nseq),
        grid=(batch // nseq, seq // tile),
        in_specs=[row_spec(D_MODEL)] * 3 + [row_spec(128)] * 2,
        out_specs=[row_spec(D_MODEL), pl.BlockSpec((nseq,) + s_shape, lambda b, t: (b, 0, 0, 0))],
        out_shape=[jax.ShapeDtypeStruct((batch, seq, D_MODEL), F32), jax.ShapeDtypeStruct((batch,) + s_shape, F32)],
        scratch_shapes=[pltpu.VMEM((nseq,) + s_shape, F32)],
        compiler_params=pltpu.CompilerParams(dimension_semantics=("arbitrary", "arbitrary"),
                                             vmem_limit_bytes=VMEM_LIMIT),
        name="delta_prompt",
    )(as3d(q), as3d(k), as3d(v), as3d(g), as3d(beta))
    return o.reshape(batch * seq, D_MODEL), s_new


def _head_indicators():
    d = lax.broadcasted_iota(jnp.int32, (D_MODEL, 128), 0) // HEAD_DIM
    hcol = lax.broadcasted_iota(jnp.int32, (D_MODEL, 128), 1)
    e_sum = jnp.where(d == hcol, 1.0, 0.0).astype(BF16)
    hrow = lax.broadcasted_iota(jnp.int32, (128, D_MODEL), 0)
    d2 = lax.broadcasted_iota(jnp.int32, (128, D_MODEL), 1) // HEAD_DIM
    e_bc = jnp.where(hrow == d2, 1.0, 0.0).astype(BF16)
    return e_sum, e_bc


def _sample_prep_body(q_ref, k_ref, v_ref, g_ref, b_ref, wkqd_ref, wv_ref, kt_ref, qkd_ref, glx_ref,
                      *, steps, nb):
    e_sum, e_bc = _head_indicators()
    expand = lambda x: _mm_exact_lhs(x, e_bc)
    hsum = lambda y: _mm_exact_lhs(y, e_sum)
    sl = lambda ref, i: ref[i * nb:(i + 1) * nb, :]
    q = [sl(q_ref, i) for i in range(steps)]
    k = [sl(k_ref, i) for i in range(steps)]
    v = [sl(v_ref, i) for i in range(steps)]
    beta = [sl(b_ref, i) for i in range(steps)]
    g_cum = []
    for i in range(steps):
        gi = sl(g_ref, i)
        g_cum.append(gi if i == 0 else g_cum[-1] + gi)
    wv, wk = [], []
    for i in range(steps):
        acc_v = v[i] * expand(beta[i])
        acc_k = k[i] * expand(beta[i] * jnp.exp(g_cum[i]))
        for j in range(i):
            a_ij = expand(hsum(k[i] * k[j]) * jnp.exp(g_cum[i] - g_cum[j]) * beta[i])
            acc_v = acc_v - a_ij * wv[j]
            acc_k = acc_k - a_ij * wk[j]
        wv.append(acc_v)
        wk.append(acc_k)
    zeros = jnp.zeros((nb, D_MODEL), F32)
    for i in range(steps):
        wkqd_ref[:, i, :] = wk[i]
        wkqd_ref[:, steps + i, :] = q[i] * expand(jnp.exp(g_cum[i]))
        wv_ref[:, i, :] = wv[i]
        wv_ref[:, steps + i, :] = zeros
        kt_ref[:, i, :] = k[i] * expand(jnp.exp(g_cum[steps - 1] - g_cum[i]))
        kt_ref[:, steps + i, :] = zeros
        for j in range(steps):
            idx = i * steps + j
            if j <= i:
                qkd_ref[idx * nb:(idx + 1) * nb, :] = hsum(q[i] * k[j]) * jnp.exp(g_cum[i] - g_cum[j])
            else:
                qkd_ref[idx * nb:(idx + 1) * nb, :] = jnp.zeros((nb, 128), F32)
    glx_ref[...] = expand(jnp.exp(g_cum[steps - 1]))


def _sample_prep(q, k, v, g, beta, *, steps, nb):
    m = steps * nb
    out_shape = [jax.ShapeDtypeStruct((nb, 2 * steps, D_MODEL), F32)] * 3 + [
        jax.ShapeDtypeStruct((steps * steps * nb, 128), F32), jax.ShapeDtypeStruct((nb, D_MODEL), F32)]
    return pl.pallas_call(
        functools.partial(_sample_prep_body, steps=steps, nb=nb),
        grid=(1,),
        in_specs=[_full((m, D_MODEL))] * 3 + [_full((m, 128))] * 2,
        out_specs=[_full(s.shape) for s in out_shape],
        out_shape=out_shape,
        compiler_params=pltpu.CompilerParams(dimension_semantics=("arbitrary",), vmem_limit_bytes=VMEM_LIMIT),
        name="sample_prep",
    )(q, k, v, g, beta)


def _sample_state_body(*refs, bb, group):
    s_ref, wkqd_ref, wv_ref, kt_ref, glx_ref = refs[:5]
    r_ref, snew_ref = refs[-2:]
    zeros_pad = jnp.zeros((HEAD_DIM - 8, HEAD_DIM), F32)

    def per_group(gi, carry):
        units = [(gi * group + j, hh) for j in range(group) for hh in range(N_HEADS)]
        ls = [slice(hh * HEAD_DIM, (hh + 1) * HEAD_DIM) for hh in range(N_HEADS)]
        pad = lambda a, n: jnp.concatenate([a, zeros_pad[:n - 8]], axis=0)
        s0 = [s_ref[b, hh] for b, hh in units]
        res = [_mm(pad(wkqd_ref[b, :, ls[hh]], 16), s0[n])[:8] for n, (b, hh) in enumerate(units)]
        kt_t = [pad(kt_ref[b, :, ls[hh]], HEAD_DIM).T for b, hh in units]
        upd = [_mm(kt_t[n], pad(wv_ref[b, :, ls[hh]] - res[n], HEAD_DIM)) for n, (b, hh) in enumerate(units)]
        for n, (b, hh) in enumerate(units):
            r_ref[b, :, ls[hh]] = res[n]
            snew_ref[b, hh] = s0[n] * glx_ref[b, :, ls[hh]] + upd[n]
        return carry

    lax.fori_loop(0, bb // group, per_group, 0)


def _sample_state(li, state, wkqd, wv, kt, glx, prev, *, bb):
    nb = state.shape[1]
    s_spec = pl.BlockSpec((None, bb, N_HEADS, HEAD_DIM, HEAD_DIM), lambda i: (li, i, 0, 0, 0))
    slot_spec = pl.BlockSpec((bb, 8, D_MODEL), lambda i: (i, 0, 0))
    in_specs = [s_spec, slot_spec, slot_spec, slot_spec, pl.BlockSpec((bb, 1, D_MODEL), lambda i: (i, 0, 0))]
    args = [state, wkqd, wv, kt, glx.reshape(nb, 1, D_MODEL)]
    aliases = {}
    if prev is not None:
        in_specs.append(pl.BlockSpec(memory_space=pl.ANY))
        args.append(prev)
        aliases = {len(args) - 1: 1}
    return pl.pallas_call(
        functools.partial(_sample_state_body, bb=bb, group=2),
        grid=(nb // bb,),
        in_specs=in_specs,
        out_specs=[slot_spec, s_spec],
        out_shape=[jax.ShapeDtypeStruct((nb, 8, D_MODEL), F32), jax.ShapeDtypeStruct(state.shape, F32)],
        input_output_aliases=aliases,
        compiler_params=pltpu.CompilerParams(dimension_semantics=("arbitrary",), vmem_limit_bytes=VMEM_LIMIT),
        name="sample_state",
    )(*args)


def _sample_out_body(r_ref, wv_ref, qkd_ref, o_ref, *, steps, nb):
    _, e_bc = _head_indicators()
    sl = lambda ref, i: ref[i * nb:(i + 1) * nb, :]
    u = [wv_ref[:, j, :] - r_ref[:, j, :] for j in range(steps)]
    for i in range(steps):
        acc = r_ref[:, steps + i, :]
        for j in range(i + 1):
            acc = acc + _mm_exact_lhs(sl(qkd_ref, i * steps + j), e_bc) * u[j]
        o_ref[i * nb:(i + 1) * nb, :] = acc


def _sample_out(r, wv, qkd, *, steps, nb):
    m = steps * nb
    return pl.pallas_call(
        functools.partial(_sample_out_body, steps=steps, nb=nb),
        grid=(1,),
        in_specs=[_full(r.shape), _full(wv.shape), _full(qkd.shape)],
        out_specs=_full((m, D_MODEL)),
        out_shape=jax.ShapeDtypeStruct((m, D_MODEL), F32),
        compiler_params=pltpu.CompilerParams(dimension_semantics=("arbitrary",), vmem_limit_bytes=VMEM_LIMIT),
        name="sample_out",
    )(r, wv, qkd)


def _merge_body(o_ref, zs_ref, ypg_ref, sga_ref, sgb_ref, x_ref, p_ref, gn_ref, wpa_ref, wpb_ref, wout_ref,
                npl_ref, wpg_ref, wpp_ref, fn_ref, y_ref, *, final, parts):
    step = o_ref.shape[0] // parts
    groups = [slice(i * step, (i + 1) * step) for i in range(parts)]
    gn = gn_ref[...]
    y_a = []
    for rs in groups:
        gated = []
        for hh in range(N_HEADS):
            ls = slice(hh * HEAD_DIM, (hh + 1) * HEAD_DIM)
            oh = o_ref[rs, ls]
            on = oh * lax.rsqrt(jnp.mean(oh * oh, axis=-1, keepdims=True) + EPS) * gn
            gated.append((on * zs_ref[rs, ls]).astype(BF16))
        y_a.append(_dot(jnp.concatenate(gated, axis=1), wpa_ref[...]))
    y_b = [_mm(ypg_ref[rs, :], wpb_ref[...]) for rs in groups]
    m = [sga_ref[rs, :] * y_a[i] + sgb_ref[rs, :] * y_b[i] for i, rs in enumerate(groups)]
    x1 = [x_ref[rs, :] + _mm(m[i], wout_ref[...]) for i, rs in enumerate(groups)]
    gate = [jax.nn.sigmoid(_mm(_rmsnorm(x1[i], npl_ref[...]), wpg_ref[...])) for i in range(parts)]
    pe = [_mm(p_ref[rs, :], wpp_ref[...]) for rs in groups]
    for i, rs in enumerate(groups):
        x2 = x1[i] + gate[i] * pe[i]
        if final:
            x2 = _rmsnorm(x2, fn_ref[...])
        y_ref[rs, :] = x2


def _merge(li, o, zs, ypg, sga, sgb, x, p, gn, wpa, wpb, wout, npl, wpg, wpp, fn, *, tile, final):
    m = x.shape[0]
    ple = p.shape[-1]
    row_spec = lambda c: pl.BlockSpec((tile, c), lambda i: (i, 0))
    sq = _layer(li, (D_MODEL, D_MODEL))
    return pl.pallas_call(
        functools.partial(_merge_body, final=final, parts=2),
        grid=(m // tile,),
        in_specs=[row_spec(D_MODEL)] * 6 + [pl.BlockSpec((None, tile, ple), lambda i: (li, i, 0)),
                                            _layer(li, (1, HEAD_DIM)), sq, sq, sq, _layer(li, (1, D_MODEL)),
                                            sq, _layer(li, (ple, D_MODEL)), _full((1, D_MODEL))],
        out_specs=row_spec(D_MODEL),
        out_shape=jax.ShapeDtypeStruct((m, D_MODEL), F32),
        compiler_params=pltpu.CompilerParams(dimension_semantics=("arbitrary",), vmem_limit_bytes=VMEM_LIMIT),
        name="merge",
    )(o, zs, ypg, sga, sgb, x, p, gn, wpa, wpb, wout, npl, wpg, wpp, fn)


def kernel(x_prompt, x_sample, p_prompt, p_sample, state_conv, state_delta, state_pool, norm_mix, w_in, conv_w,
           a_log, dt_bias, gdn_norm, w_proj_a, pool_w, pool_scale, w_proj_b, w_out, norm_ple, w_ple_gate,
           w_ple_proj, final_norm):
    depth = w_in.shape[0]
    batch, seq, _ = x_prompt.shape
    nb, steps, _ = x_sample.shape
    rowvec = lambda a: a.reshape(depth, 1, -1)
    pad128 = lambda a: jnp.pad(a, ((0, 0), (0, 128 - a.shape[1]))).reshape(depth, 1, 128)
    mix_w = (rowvec(norm_mix), _wprep(w_in, tk=128), conv_w, pad128(a_log), pad128(dt_bias),
             pool_w.astype(BF16), rowvec(pool_scale))
    merge_w = (rowvec(gdn_norm), w_proj_a.astype(BF16), w_proj_b.astype(BF16), w_out.astype(BF16),
               rowvec(norm_ple), w_ple_gate.astype(BF16), w_ple_proj.astype(BF16), final_norm.reshape(1, -1))

    xp = x_prompt.reshape(batch * seq, D_MODEL)
    pp = p_prompt.reshape(depth, batch * seq, -1)
    conv_p, delta_p, pool_p = [], [], []
    for li in range(depth):
        q, k, v, g, beta, zs, ypg, sga, sgb, cnew, pnew = _mixin_prompt(li, xp, *mix_w, batch=batch, seq=seq, tile=256)
        o, s_new = _delta_prompt(q, k, v, g, beta, batch=batch, seq=seq, tile=256, nseq=4)
        xp = _merge(li, o, zs, ypg, sga, sgb, xp, pp, *merge_w, tile=512, final=(li == depth - 1))
        conv_p.append(cnew)
        delta_p.append(s_new)
        pool_p.append(pnew)
    y_prompt = xp.reshape(batch, seq, D_MODEL)

    swap = lambda a: jnp.swapaxes(a, 0, 1)
    flat = lambda a: a.reshape(-1, a.shape[-1])
    xs = swap(x_sample)
    ps = jnp.swapaxes(p_sample, 1, 2).reshape(depth, steps * nb, -1)
    conv_t, pool_t = jnp.swapaxes(state_conv, 1, 2), jnp.swapaxes(state_pool, 1, 2)
    new_states, delta_s = None, None
    for li in range(depth):
        q, k, v, g, beta, zs, ypg, sga, sgb, *new_states = _mixin_sample(
            li, xs, *mix_w, conv_t, pool_t, new_states, bb=32, pos0=PAST_LEN)
        wkqd, wv, kt, qkd, glx = _sample_prep(flat(q), flat(k), flat(v), flat(g), flat(beta), steps=steps, nb=nb)
        r, delta_s = _sample_state(li, state_delta, wkqd, wv, kt, glx, delta_s, bb=8)
        o = _sample_out(r, wv, qkd, steps=steps, nb=nb)
        xs = _merge(li, o, flat(zs), flat(ypg), flat(sga), flat(sgb), flat(xs), ps, *merge_w, tile=steps * nb,
                    final=(li == depth - 1)).reshape(steps, nb, D_MODEL)
    y_sample = swap(xs)
    conv_s, pool_s = (jnp.swapaxes(a, 1, 2) for a in new_states)

    return (y_prompt, y_sample, jnp.stack(conv_p), jnp.stack(delta_p), jnp.stack(pool_p),
            conv_s, delta_s, pool_s)
```

```python
import functools

import jax
import jax.numpy as jnp
from jax import lax
from jax.experimental import pallas as pl
from jax.experimental.pallas import tpu as pltpu

F32 = jnp.float32
BF16 = jnp.bfloat16

D_MODEL = 1024
N_HEADS = 8
HEAD_DIM = 128
CONV_W = 4
CONV_CH = 3 * D_MODEL
POOL_WINDOWS = (2, 4, 8, 16)
POOL_HIST = 15
POOL_GC = 256
EPS = 1e-6
PAST_LEN = 16384
CHUNK = 64
BLOCK = 2 * CHUNK
VMEM_LIMIT = 56 * 1024 * 1024

MIXIN_TILE = 256
DELTA_TILE, DELTA_SEQS = 256, 4
MERGE_TILE = 512
SAMPLE_SEQS = 32
STATE_SEQS, STATE_GROUP = 16, 2
WPREP_TK = 128

_SPLIT = 4096
_C_QKV, _C_Z, _C_U, _C_GP, _C_GA, _C_GB, _C_A, _C_B = 0, 3072, 4096, 5120, 6144, 7168, 8192, 8320
_W_COLS = 8448


def _dot(a, b, nt=False):
    dims = (((1,), (1,)), ((), ())) if nt else (((1,), (0,)), ((), ()))
    return lax.dot_general(a, b, dims, preferred_element_type=F32)


def _mm(a, b, nt=False):
    return _dot(a.astype(BF16), b.astype(BF16), nt)


def _split3(x):
    x0 = x.astype(BF16)
    r = x - x0.astype(F32)
    x1 = r.astype(BF16)
    x2 = (r - x1.astype(F32)).astype(BF16)
    return x0, x1, x2


def _mm_exact_rhs(a_bf, b):
    b0, b1, b2 = _split3(b)
    return _dot(a_bf, b0) + _dot(a_bf, b1) + _dot(a_bf, b2)


def _mm_exact_lhs(a, b_bf):
    a0, a1, a2 = _split3(a)
    return _dot(a0, b_bf) + _dot(a1, b_bf) + _dot(a2, b_bf)


def _silu(x):
    return x * jax.nn.sigmoid(x)


def _rmsnorm(x, w):
    return x * lax.rsqrt(jnp.mean(x * x, axis=-1, keepdims=True) + EPS) * w


def _full(shape):
    n = len(shape)
    return pl.BlockSpec(shape, lambda *_: (0,) * n, pipeline_mode=pl.Buffered(1))


def _layer(li, shape):
    n = len(shape)
    return pl.BlockSpec((None,) + tuple(shape), lambda *_: (li,) + (0,) * n, pipeline_mode=pl.Buffered(1))


def _wprep_body(wt_ref, o_ref):
    n_tail = wt_ref.shape[0] - _SPLIT - 2 * N_HEADS
    for g in range(_SPLIT // 128):
        o_ref[:, g * 128:(g + 1) * 128] = wt_ref[g * 128:(g + 1) * 128, :].T.astype(BF16)
    for g in range(n_tail // 128):
        r0 = _SPLIT + 2 * N_HEADS + g * 128
        o_ref[:, _C_U + g * 128:_C_U + (g + 1) * 128] = wt_ref[r0:r0 + 128, :].T.astype(BF16)
    ab = wt_ref[_SPLIT:_SPLIT + 128, :].T
    lane = lax.broadcasted_iota(jnp.int32, ab.shape, 1)
    o_ref[:, _C_A:_C_A + 128] = jnp.where(lane < N_HEADS, ab, 0.0).astype(BF16)
    o_ref[:, _C_B:_C_B + 128] = jnp.where(lane < N_HEADS, pltpu.roll(ab, 128 - N_HEADS, 1), 0.0).astype(BF16)


def _wprep(w_in, *, tk):
    depth, k, n = w_in.shape
    return pl.pallas_call(
        _wprep_body, grid=(depth, k // tk),
        in_specs=[pl.BlockSpec((None, n, tk), lambda l, r: (l, 0, r))],
        out_specs=pl.BlockSpec((None, tk, _W_COLS), lambda l, r: (l, r, 0)),
        out_shape=jax.ShapeDtypeStruct((depth, k, _W_COLS), BF16),
        compiler_params=pltpu.CompilerParams(dimension_semantics=("arbitrary", "arbitrary"),
                                             vmem_limit_bytes=VMEM_LIMIT),
        name="wprep",
    )(jnp.swapaxes(w_in, 1, 2))


def _mixin_core(h, w_ref, cw_ref, alog_ref, dtb_ref, pw_ref, ps_ref, outs, pos, conv_tap, pool_sums, store):
    q_ref, k_ref, v_ref, g_ref, b_ref, zs_ref, ypg_ref, sga_ref, sgb_ref = outs
    proj = lambda c0, n=D_MODEL: _dot(h, w_ref[:, c0:c0 + n])
    xcs = [proj(_C_QKV + seg * D_MODEL) for seg in range(3)]
    gates = [proj(_C_Z), proj(_C_GA), proj(_C_GB)]
    a_r = proj(_C_A, 128)
    b_r = proj(_C_B, 128)
    u = proj(_C_U)
    gp = proj(_C_GP)

    for seg, out_ref in enumerate((q_ref, k_ref, v_ref)):
        c0 = seg * D_MODEL
        xc = xcs[seg]
        acc = xc * cw_ref[CONV_W - 1:CONV_W, c0:c0 + D_MODEL]
        for j in range(CONV_W - 1):
            acc = acc + conv_tap(j, c0, xc) * cw_ref[j:j + 1, c0:c0 + D_MODEL]
        y = _silu(acc)
        if seg < 2:
            for hh in range(N_HEADS):
                ls = slice(hh * HEAD_DIM, (hh + 1) * HEAD_DIM)
                yh = y[:, ls]
                inv = lax.rsqrt(jnp.sum(yh * yh, axis=-1, keepdims=True) + EPS)
                if seg == 0:
                    inv = inv * (HEAD_DIM ** -0.5)
                store(out_ref, yh * inv, ls)
        else:
            store(out_ref, y, None)

    store(zs_ref, _silu(gates[0]), None)
    store(sga_ref, jax.nn.sigmoid(gates[1]), None)
    store(sgb_ref, jax.nn.sigmoid(gates[2]), None)
    xs = a_r + dtb_ref[...]
    softplus = jnp.maximum(xs, 0.0) + jnp.log1p(jnp.exp(-jnp.abs(xs)))
    store(g_ref, -jnp.exp(alog_ref[...]) * softplus, None)
    store(b_ref, jax.nn.sigmoid(b_r), None)

    sgp = _silu(gp)
    sums = pool_sums(u)
    for gi, win in enumerate(POOL_WINDOWS):
        l0 = gi * POOL_GC
        ls = slice(l0, l0 + POOL_GC)
        count = jnp.minimum(pos + 1, win).astype(F32)
        y = sums[gi] / count - u[:, ls]
        yp = _mm(y, pw_ref[gi]) * ps_ref[:, ls]
        store(ypg_ref, yp * sgp[:, ls], ls)
    return xcs, u


def _mixin_weight_specs(li):
    return [_layer(li, (1, D_MODEL)), _layer(li, (D_MODEL, _W_COLS)), _layer(li, (CONV_W, CONV_CH)),
            _layer(li, (1, 128)), _layer(li, (1, 128)), _layer(li, (4, POOL_GC, POOL_GC)), _layer(li, (1, D_MODEL))]


def _mixin_prompt_body(x_ref, nm_ref, w_ref, cw_ref, alog_ref, dtb_ref, pw_ref, ps_ref,
                       q_ref, k_ref, v_ref, g_ref, b_ref, zs_ref, ypg_ref, sga_ref, sgb_ref, cnew_ref, pnew_ref,
                       cext, pext, *, rows):
    hc, hp = 8, 16
    t = pl.program_id(1)

    @pl.when(t == 0)
    def _():
        cext[0:hc, :] = jnp.zeros((hc, CONV_CH), F32)
        pext[0:hp, :] = jnp.zeros((hp, D_MODEL), F32)

    h = _rmsnorm(x_ref[...], nm_ref[...]).astype(BF16)

    def conv_tap(j, c0, xc):
        if j == 0:
            cext[hc:hc + rows, c0:c0 + D_MODEL] = xc
        off = hc - (CONV_W - 1 - j)
        return cext[off:off + rows, c0:c0 + D_MODEL]

    def pool_sums(u):
        pext[hp:hp + rows, :] = u
        lvl = pext[...]
        out = []
        for gi, win in enumerate(POOL_WINDOWS):
            lvl = lvl + pltpu.roll(lvl, win // 2, 0)
            out.append(lvl[hp:, :POOL_GC])
            lvl = lvl[:, POOL_GC:]
        return out

    def store(ref, val, ls):
        if ls is None:
            ref[...] = val.astype(ref.dtype)
        else:
            ref[:, ls] = val.astype(ref.dtype)

    pos = t * rows + lax.broadcasted_iota(jnp.int32, (rows, 1), 0)
    outs = (q_ref, k_ref, v_ref, g_ref, b_ref, zs_ref, ypg_ref, sga_ref, sgb_ref)
    _mixin_core(h, w_ref, cw_ref, alog_ref, dtb_ref, pw_ref, ps_ref, outs, pos, conv_tap, pool_sums, store)
    cnew_ref[...] = cext[hc + rows - (CONV_W - 1):hc + rows, :]
    pnew_ref[...] = pext[hp + rows - POOL_HIST:hp + rows, :]
    cext[0:hc, :] = cext[rows:rows + hc, :]
    pext[0:hp, :] = pext[rows:rows + hp, :]


def _mixin_prompt(li, x, nm, w_big, cw, alog, dtb, pw, ps, *, batch, seq, tile):
    m = x.shape[0]
    nt = seq // tile
    row_spec = lambda c: pl.BlockSpec((tile, c), lambda b, t: (b * nt + t, 0))
    in_specs = [row_spec(D_MODEL)] + _mixin_weight_specs(li)
    cnew_spec = pl.BlockSpec((None, CONV_W - 1, CONV_CH), lambda b, t: (b, 0, 0))
    pnew_spec = pl.BlockSpec((None, POOL_HIST, D_MODEL), lambda b, t: (b, 0, 0))
    big = jax.ShapeDtypeStruct((m, D_MODEL), F32)
    small = jax.ShapeDtypeStruct((m, 128), F32)
    gate = jax.ShapeDtypeStruct((m, D_MODEL), BF16)
    out_shape = [big, big, big, small, small, gate, gate, gate, gate,
                 jax.ShapeDtypeStruct((batch, CONV_W - 1, CONV_CH), F32),
                 jax.ShapeDtypeStruct((batch, POOL_HIST, D_MODEL), F32)]
    out_specs = [row_spec(D_MODEL)] * 3 + [row_spec(128)] * 2 + [row_spec(D_MODEL)] * 4 + [cnew_spec, pnew_spec]
    return pl.pallas_call(
        functools.partial(_mixin_prompt_body, rows=tile),
        grid=(batch, nt), in_specs=in_specs, out_specs=out_specs, out_shape=out_shape,
        scratch_shapes=[pltpu.VMEM((8 + tile, CONV_CH), F32), pltpu.VMEM((16 + tile, D_MODEL), F32)],
        compiler_params=pltpu.CompilerParams(dimension_semantics=("arbitrary", "arbitrary"),
                                             vmem_limit_bytes=VMEM_LIMIT),
        name="mixin_prompt",
    )(x, nm, w_big, cw, alog, dtb, pw, ps)


def _mixin_sample_body(*refs, steps, bb, pos0):
    x_ref, nm_ref, w_ref, cw_ref, alog_ref, dtb_ref, pw_ref, ps_ref, chist_ref, phist_ref = refs[:10]
    q_ref, k_ref, v_ref, g_ref, b_ref, zs_ref, ypg_ref, sga_ref, sgb_ref, cnew_ref, pnew_ref = refs[-11:]
    rows = steps * bb
    h = _rmsnorm(x_ref[...].reshape(rows, D_MODEL), nm_ref[...]).astype(BF16)

    def delayed(new, hist_ref, nhist, d, ls_new, ls_hist):
        parts = []
        for t in range(steps):
            src = t - d
            parts.append(new[src * bb:(src + 1) * bb, ls_new] if src >= 0 else hist_ref[nhist + src, :, ls_hist])
        return jnp.concatenate(parts, axis=0)

    def conv_tap(j, c0, xc):
        return delayed(xc, chist_ref, CONV_W - 1, CONV_W - 1 - j, slice(None), slice(c0, c0 + D_MODEL))

    def pool_sums(u):
        out = []
        for gi, win in enumerate(POOL_WINDOWS):
            ls = slice(gi * POOL_GC, (gi + 1) * POOL_GC)
            acc = u[:, ls]
            for s in range(1, win):
                acc = acc + delayed(u, phist_ref, POOL_HIST, s, ls, ls)
            out.append(acc)
        return out

    def store(ref, val, ls):
        val = val.reshape(steps, bb, val.shape[-1]).astype(ref.dtype)
        if ls is None:
            ref[...] = val
        else:
            ref[:, :, ls] = val

    pos = pos0 + lax.broadcasted_iota(jnp.int32, (rows, 1), 0) // bb
    outs = (q_ref, k_ref, v_ref, g_ref, b_ref, zs_ref, ypg_ref, sga_ref, sgb_ref)
    xcs, u = _mixin_core(h, w_ref, cw_ref, alog_ref, dtb_ref, pw_ref, ps_ref, outs, pos, conv_tap, pool_sums, store)
    for i in range(CONV_W - 1):
        src = steps + i - (CONV_W - 1)
        for seg in range(3):
            ls = slice(seg * D_MODEL, (seg + 1) * D_MODEL)
            cnew_ref[i, :, ls] = (xcs[seg][src * bb:(src + 1) * bb, :] if src >= 0
                                  else chist_ref[CONV_W - 1 + src, :, ls])
    for i in range(POOL_HIST):
        src = steps + i - POOL_HIST
        pnew_ref[i] = u[src * bb:(src + 1) * bb, :] if src >= 0 else phist_ref[POOL_HIST + src]


def _mixin_sample(li, x, nm, w_big, cw, alog, dtb, pw, ps, chist, phist, prev, *, bb, pos0):
    steps, nb, _ = x.shape
    slab_spec = lambda n, c: pl.BlockSpec((n, bb, c), lambda i: (0, i, 0))
    state_spec = lambda n, c: pl.BlockSpec((None, n, bb, c), lambda i: (li, 0, i, 0))
    in_specs = ([slab_spec(steps, D_MODEL)] + _mixin_weight_specs(li)
                + [state_spec(CONV_W - 1, CONV_CH), state_spec(POOL_HIST, D_MODEL)])
    args = [x, nm, w_big, cw, alog, dtb, pw, ps, chist, phist]
    aliases = {}
    if prev is not None:
        in_specs += [pl.BlockSpec(memory_space=pl.ANY)] * 2
        aliases = {len(args): 9, len(args) + 1: 10}
        args += list(prev)
    big = jax.ShapeDtypeStruct((steps, nb, D_MODEL), F32)
    small = jax.ShapeDtypeStruct((steps, nb, 128), F32)
    gate = jax.ShapeDtypeStruct((steps, nb, D_MODEL), BF16)
    out_shape = [big, big, big, small, small, gate, gate, gate, gate,
                 jax.ShapeDtypeStruct(chist.shape, F32), jax.ShapeDtypeStruct(phist.shape, F32)]
    out_specs = ([slab_spec(steps, D_MODEL)] * 3 + [slab_spec(steps, 128)] * 2 + [slab_spec(steps, D_MODEL)] * 4
                 + [state_spec(CONV_W - 1, CONV_CH), state_spec(POOL_HIST, D_MODEL)])
    return pl.pallas_call(
        functools.partial(_mixin_sample_body, steps=steps, bb=bb, pos0=pos0),
        grid=(nb // bb,), in_specs=in_specs, out_specs=out_specs, out_shape=out_shape,
        input_output_aliases=aliases,
        compiler_params=pltpu.CompilerParams(dimension_semantics=("arbitrary",), vmem_limit_bytes=VMEM_LIMIT),
        name="mixin_sample",
    )(*args)


def _delta_body(q_ref, k_ref, v_ref, g_ref, b_ref, o_ref, sout_ref, s_ref, *, tile, nseq):
    t = pl.program_id(1)

    @pl.when(t == 0)
    def _():
        s_ref[...] = jnp.zeros(s_ref.shape, F32)

    r = lax.broadcasted_iota(jnp.int32, (BLOCK, BLOCK), 0)
    c = lax.broadcasted_iota(jnp.int32, (BLOCK, BLOCK), 1)
    same = (r // CHUNK) == (c // CHUNK)
    low = same & (r >= c)
    strict = same & (r > c)
    eye = jnp.where(r == c, 1.0, 0.0).astype(F32)
    l_blk = jnp.where(low, 1.0, 0.0).astype(BF16)
    ones0 = jnp.where(c < CHUNK, 1.0, 0.0).astype(BF16)
    ones1 = jnp.where(c >= CHUNK, 1.0, 0.0).astype(BF16)
    first = lax.broadcasted_iota(jnp.int32, (BLOCK, 1), 0) < CHUNK
    zeros_half = jnp.zeros((CHUNK, HEAD_DIM), F32)
    cat = jnp.concatenate
    col = lambda a, hh: a[:, hh:hh + 1]
    units = [(sq, hh) for sq in range(nseq) for hh in range(N_HEADS)]
    n_units = range(len(units))
    ls = [slice(hh * HEAD_DIM, (hh + 1) * HEAD_DIM) for hh in range(N_HEADS)]

    def block(i, carry):
        rows = pl.ds(pl.multiple_of(i * BLOCK, BLOCK), BLOCK)
        gp = [g_ref[sq, rows, :] for sq in range(nseq)]
        bt = [b_ref[sq, rows, :] for sq in range(nseq)]
        g_cum = [_mm_exact_rhs(l_blk, x) for x in gp]
        tot0 = [_mm_exact_rhs(ones0, x) for x in gp]
        tot1 = [_mm_exact_rhs(ones1, x) for x in gp]
        g_t = [x.T for x in g_cum]
        e_g = [jnp.exp(x) for x in g_cum]
        e_tail = [jnp.exp(jnp.where(first, tot0[sq], tot1[sq]) - g_cum[sq]) for sq in range(nseq)]
        e_tot = ([jnp.exp(x) for x in tot0], [jnp.exp(x) for x in tot1])

        kh = [k_ref[sq, rows, ls[hh]] for sq, hh in units]
        qh = [q_ref[sq, rows, ls[hh]] for sq, hh in units]
        beta = [col(bt[sq], hh) for sq, hh in units]
        eg = [col(e_g[sq], hh) for sq, hh in units]
        decay = [jnp.exp(jnp.where(low, col(g_cum[sq], hh) - g_t[sq][hh:hh + 1, :], -jnp.inf)) for sq, hh in units]
        kq = [_mm(cat([kh[n], qh[n]], axis=0), kh[n], nt=True) for n in n_units]
        p = [-jnp.where(strict, kq[n][:BLOCK] * decay[n] * beta[n], 0.0) for n in n_units]
        qk = [kq[n][BLOCK:] * decay[n] for n in n_units]
        tinv = [eye + p[n] for n in n_units]
        p = [_mm(p[n], p[n]) for n in n_units]
        for _ in range(4):
            pp = [_mm(p[n], cat([p[n], tinv[n]], axis=1)) for n in n_units]
            p = [pp[n][:, :BLOCK] for n in n_units]
            tinv = [tinv[n] + pp[n][:, BLOCK:] for n in n_units]
        tinv = [tinv[n] + _mm(p[n], tinv[n]) for n in n_units]
        sol = [_mm(tinv[n], cat([v_ref[sq, rows, ls[hh]] * beta[n], kh[n] * (beta[n] * eg[n])], axis=1))
               for n, (sq, hh) in enumerate(units)]
        wv = [sol[n][:, :HEAD_DIM] for n in n_units]
        wk = [sol[n][:, HEAD_DIM:] for n in n_units]
        qd = [qh[n] * eg[n] for n in n_units]
        kt_t = [(kh[n] * col(e_tail[sq], hh)).T for n, (sq, hh) in enumerate(units)]
        s_cur = [s_ref[sq, hh] for sq, hh in units]
        o_parts = []
        for half in range(2):
            hs = slice(half * CHUNK, (half + 1) * CHUNK)
            res = [_mm(cat([wk[n][hs], qd[n][hs]], axis=0), s_cur[n]) for n in n_units]
            u_new = [wv[n][hs] - res[n][:CHUNK] for n in n_units]
            u_pad = [cat([u_new[n], zeros_half] if half == 0 else [zeros_half, u_new[n]], axis=0) for n in n_units]
            upd = [_mm(cat([qk[n][hs], kt_t[n]], axis=0), u_pad[n]) for n in n_units]
            o_parts.append([res[n][CHUNK:] + upd[n][:CHUNK] for n in n_units])
            s_cur = [s_cur[n] * col(e_tot[half][sq], hh) + upd[n][CHUNK:] for n, (sq, hh) in enumerate(units)]
        for n, (sq, hh) in enumerate(units):
            s_ref[sq, hh] = s_cur[n]
            o_ref[sq, rows, ls[hh]] = cat([o_parts[0][n], o_parts[1][n]], axis=0)
        return carry

    lax.fori_loop(0, tile // BLOCK, block, 0)
    sout_ref[...] = s_ref[...]


def _delta_prompt(q, k, v, g, beta, *, batch, seq, tile, nseq):
    as3d = lambda a: a.reshape(batch, seq, a.shape[-1])
    row_spec = lambda c: pl.BlockSpec((nseq, tile, c), lambda b, t: (b, t, 0))
    s_shape = (N_HEADS, HEAD_DIM, HEAD_DIM)
    o, s_new = pl.pallas_call(
        functools.partial(_delta_body, tile=tile, nseq=nseq),
        grid=(batch // nseq, seq // tile),
        in_specs=[row_spec(D_MODEL)] * 3 + [row_spec(128)] * 2,
        out_specs=[row_spec(D_MODEL), pl.BlockSpec((nseq,) + s_shape, lambda b, t: (b, 0, 0, 0))],
        out_shape=[jax.ShapeDtypeStruct((batch, seq, D_MODEL), F32), jax.ShapeDtypeStruct((batch,) + s_shape, F32)],
        scratch_shapes=[pltpu.VMEM((nseq,) + s_shape, F32)],
        compiler_params=pltpu.CompilerParams(dimension_semantics=("arbitrary", "arbitrary"),
                                             vmem_limit_bytes=VMEM_LIMIT),
        name="delta_prompt",
    )(as3d(q), as3d(k), as3d(v), as3d(g), as3d(beta))
    return o.reshape(batch * seq, D_MODEL), s_new


def _head_indicators():
    d = lax.broadcasted_iota(jnp.int32, (D_MODEL, 128), 0) // HEAD_DIM
    hcol = lax.broadcasted_iota(jnp.int32, (D_MODEL, 128), 1)
    e_sum = jnp.where(d == hcol, 1.0, 0.0).astype(BF16)
    hrow = lax.broadcasted_iota(jnp.int32, (128, D_MODEL), 0)
    d2 = lax.broadcasted_iota(jnp.int32, (128, D_MODEL), 1) // HEAD_DIM
    e_bc = jnp.where(hrow == d2, 1.0, 0.0).astype(BF16)
    return e_sum, e_bc


def _sample_prep_body(q_ref, k_ref, v_ref, g_ref, b_ref, wkqd_ref, wv_ref, kt_ref, qkd_ref, glx_ref,
                      *, steps, nb):
    e_sum, e_bc = _head_indicators()
    expand = lambda x: _mm_exact_lhs(x, e_bc)
    hsum = lambda y: _mm_exact_lhs(y, e_sum)
    sl = lambda ref, i: ref[i * nb:(i + 1) * nb, :]
    q = [sl(q_ref, i) for i in range(steps)]
    k = [sl(k_ref, i) for i in range(steps)]
    v = [sl(v_ref, i) for i in range(steps)]
    beta = [sl(b_ref, i) for i in range(steps)]
    g_cum = []
    for i in range(steps):
        gi = sl(g_ref, i)
        g_cum.append(gi if i == 0 else g_cum[-1] + gi)
    wv, wk = [], []
    for i in range(steps):
        acc_v = v[i] * expand(beta[i])
        acc_k = k[i] * expand(beta[i] * jnp.exp(g_cum[i]))
        for j in range(i):
            a_ij = expand(hsum(k[i] * k[j]) * jnp.exp(g_cum[i] - g_cum[j]) * beta[i])
            acc_v = acc_v - a_ij * wv[j]
            acc_k = acc_k - a_ij * wk[j]
        wv.append(acc_v)
        wk.append(acc_k)
    zeros = jnp.zeros((nb, D_MODEL), F32)
    for i in range(steps):
        wkqd_ref[:, i, :] = wk[i]
        wkqd_ref[:, steps + i, :] = q[i] * expand(jnp.exp(g_cum[i]))
        wv_ref[:, i, :] = wv[i]
        wv_ref[:, steps + i, :] = zeros
        kt_ref[:, i, :] = k[i] * expand(jnp.exp(g_cum[steps - 1] - g_cum[i]))
        kt_ref[:, steps + i, :] = zeros
        for j in range(steps):
            idx = i * steps + j
            if j <= i:
                qkd_ref[idx * nb:(idx + 1) * nb, :] = hsum(q[i] * k[j]) * jnp.exp(g_cum[i] - g_cum[j])
            else:
                qkd_ref[idx * nb:(idx + 1) * nb, :] = jnp.zeros((nb, 128), F32)
    glx_ref[...] = expand(jnp.exp(g_cum[steps - 1]))


def _sample_prep(q, k, v, g, beta, *, steps, nb):
    m = steps * nb
    out_shape = [jax.ShapeDtypeStruct((nb, 2 * steps, D_MODEL), F32)] * 3 + [
        jax.ShapeDtypeStruct((steps * steps * nb, 128), F32), jax.ShapeDtypeStruct((nb, D_MODEL), F32)]
    return pl.pallas_call(
        functools.partial(_sample_prep_body, steps=steps, nb=nb),
        grid=(1,),
        in_specs=[_full((m, D_MODEL))] * 3 + [_full((m, 128))] * 2,
        out_specs=[_full(s.shape) for s in out_shape],
        out_shape=out_shape,
        compiler_params=pltpu.CompilerParams(dimension_semantics=("arbitrary",), vmem_limit_bytes=VMEM_LIMIT),
        name="sample_prep",
    )(q, k, v, g, beta)


def _sample_state_body(*refs, bb, group):
    s_ref, wkqd_ref, wv_ref, kt_ref, glx_ref = refs[:5]
    r_ref, snew_ref = refs[-2:]
    slots = wkqd_ref.shape[1]
    zeros_pad = jnp.zeros((HEAD_DIM - slots, HEAD_DIM), F32)

    def per_group(gi, carry):
        units = [(gi * group + j, hh) for j in range(group) for hh in range(N_HEADS)]
        ls = [slice(hh * HEAD_DIM, (hh + 1) * HEAD_DIM) for hh in range(N_HEADS)]
        pad = lambda a, n: jnp.concatenate([a, zeros_pad[:n - slots]], axis=0)
        s0 = [s_ref[b, hh] for b, hh in units]
        res = [_mm(pad(wkqd_ref[b, :, ls[hh]], 2 * slots), s0[n])[:slots] for n, (b, hh) in enumerate(units)]
        kt_t = [pad(kt_ref[b, :, ls[hh]], HEAD_DIM).T for b, hh in units]
        upd = [_mm(kt_t[n], pad(wv_ref[b, :, ls[hh]] - res[n], HEAD_DIM)) for n, (b, hh) in enumerate(units)]
        for n, (b, hh) in enumerate(units):
            r_ref[b, :, ls[hh]] = res[n]
            snew_ref[b, hh] = s0[n] * glx_ref[b, :, ls[hh]] + upd[n]
        return carry

    lax.fori_loop(0, bb // group, per_group, 0)


def _sample_state(li, state, wkqd, wv, kt, glx, prev, *, bb):
    nb = state.shape[1]
    s_spec = pl.BlockSpec((None, bb, N_HEADS, HEAD_DIM, HEAD_DIM), lambda i: (li, i, 0, 0, 0))
    slots = wkqd.shape[1]
    slot_spec = pl.BlockSpec((bb, slots, D_MODEL), lambda i: (i, 0, 0))
    in_specs = [s_spec, slot_spec, slot_spec, slot_spec, pl.BlockSpec((bb, 1, D_MODEL), lambda i: (i, 0, 0))]
    args = [state, wkqd, wv, kt, glx.reshape(nb, 1, D_MODEL)]
    aliases = {}
    if prev is not None:
        in_specs.append(pl.BlockSpec(memory_space=pl.ANY))
        args.append(prev)
        aliases = {len(args) - 1: 1}
    return pl.pallas_call(
        functools.partial(_sample_state_body, bb=bb, group=STATE_GROUP),
        grid=(nb // bb,),
        in_specs=in_specs,
        out_specs=[slot_spec, s_spec],
        out_shape=[jax.ShapeDtypeStruct((nb, slots, D_MODEL), F32), jax.ShapeDtypeStruct(state.shape, F32)],
        input_output_aliases=aliases,
        compiler_params=pltpu.CompilerParams(dimension_semantics=("arbitrary",), vmem_limit_bytes=VMEM_LIMIT),
        name="sample_state",
    )(*args)


def _sample_out_body(r_ref, wv_ref, qkd_ref, o_ref, *, steps, nb):
    _, e_bc = _head_indicators()
    sl = lambda ref, i: ref[i * nb:(i + 1) * nb, :]
    u = [wv_ref[:, j, :] - r_ref[:, j, :] for j in range(steps)]
    for i in range(steps):
        acc = r_ref[:, steps + i, :]
        for j in range(i + 1):
            acc = acc + _mm_exact_lhs(sl(qkd_ref, i * steps + j), e_bc) * u[j]
        o_ref[i * nb:(i + 1) * nb, :] = acc


def _sample_out(r, wv, qkd, *, steps, nb):
    m = steps * nb
    return pl.pallas_call(
        functools.partial(_sample_out_body, steps=steps, nb=nb),
        grid=(1,),
        in_specs=[_full(r.shape), _full(wv.shape), _full(qkd.shape)],
        out_specs=_full((m, D_MODEL)),
        out_shape=jax.ShapeDtypeStruct((m, D_MODEL), F32),
        compiler_params=pltpu.CompilerParams(dimension_semantics=("arbitrary",), vmem_limit_bytes=VMEM_LIMIT),
        name="sample_out",
    )(r, wv, qkd)


def _merge_body(o_ref, zs_ref, ypg_ref, sga_ref, sgb_ref, x_ref, p_ref, gn_ref, wpa_ref, wpb_ref, wout_ref,
                npl_ref, wpg_ref, wpp_ref, fn_ref, y_ref, *, final, parts):
    step = o_ref.shape[0] // parts
    groups = [slice(i * step, (i + 1) * step) for i in range(parts)]
    gn = gn_ref[...]
    y_a = []
    for rs in groups:
        gated = []
        for hh in range(N_HEADS):
            ls = slice(hh * HEAD_DIM, (hh + 1) * HEAD_DIM)
            oh = o_ref[rs, ls]
            on = oh * lax.rsqrt(jnp.mean(oh * oh, axis=-1, keepdims=True) + EPS) * gn
            gated.append((on * zs_ref[rs, ls]).astype(BF16))
        y_a.append(_dot(jnp.concatenate(gated, axis=1), wpa_ref[...]))
    y_b = [_mm(ypg_ref[rs, :], wpb_ref[...]) for rs in groups]
    m = [sga_ref[rs, :] * y_a[i] + sgb_ref[rs, :] * y_b[i] for i, rs in enumerate(groups)]
    x1 = [x_ref[rs, :] + _mm(m[i], wout_ref[...]) for i, rs in enumerate(groups)]
    gate = [jax.nn.sigmoid(_mm(_rmsnorm(x1[i], npl_ref[...]), wpg_ref[...])) for i in range(parts)]
    pe = [_mm(p_ref[rs, :], wpp_ref[...]) for rs in groups]
    for i, rs in enumerate(groups):
        x2 = x1[i] + gate[i] * pe[i]
        if final:
            x2 = _rmsnorm(x2, fn_ref[...])
        y_ref[rs, :] = x2


def _merge(li, o, zs, ypg, sga, sgb, x, p, gn, wpa, wpb, wout, npl, wpg, wpp, fn, *, tile, final):
    m = x.shape[0]
    ple = p.shape[-1]
    row_spec = lambda c: pl.BlockSpec((tile, c), lambda i: (i, 0))
    sq = _layer(li, (D_MODEL, D_MODEL))
    return pl.pallas_call(
        functools.partial(_merge_body, final=final, parts=2),
        grid=(m // tile,),
        in_specs=[row_spec(D_MODEL)] * 6 + [pl.BlockSpec((None, tile, ple), lambda i: (li, i, 0)),
                                            _layer(li, (1, HEAD_DIM)), sq, sq, sq, _layer(li, (1, D_MODEL)),
                                            sq, _layer(li, (ple, D_MODEL)), _full((1, D_MODEL))],
        out_specs=row_spec(D_MODEL),
        out_shape=jax.ShapeDtypeStruct((m, D_MODEL), F32),
        compiler_params=pltpu.CompilerParams(dimension_semantics=("arbitrary",), vmem_limit_bytes=VMEM_LIMIT),
        name="merge",
    )(o, zs, ypg, sga, sgb, x, p, gn, wpa, wpb, wout, npl, wpg, wpp, fn)


def kernel(x_prompt, x_sample, p_prompt, p_sample, state_conv, state_delta, state_pool, norm_mix, w_in, conv_w,
           a_log, dt_bias, gdn_norm, w_proj_a, pool_w, pool_scale, w_proj_b, w_out, norm_ple, w_ple_gate,
           w_ple_proj, final_norm):
    depth = w_in.shape[0]
    batch, seq, _ = x_prompt.shape
    nb, steps, _ = x_sample.shape
    rowvec = lambda a: a.reshape(depth, 1, -1)
    pad128 = lambda a: jnp.pad(a, ((0, 0), (0, 128 - a.shape[1]))).reshape(depth, 1, 128)
    mix_w = (rowvec(norm_mix), _wprep(w_in, tk=WPREP_TK), conv_w, pad128(a_log), pad128(dt_bias),
             pool_w.astype(BF16), rowvec(pool_scale))
    merge_w = (rowvec(gdn_norm), w_proj_a.astype(BF16), w_proj_b.astype(BF16), w_out.astype(BF16),
               rowvec(norm_ple), w_ple_gate.astype(BF16), w_ple_proj.astype(BF16), final_norm.reshape(1, -1))

    xp = x_prompt.reshape(batch * seq, D_MODEL)
    pp = p_prompt.reshape(depth, batch * seq, -1)
    conv_p, delta_p, pool_p = [], [], []
    for li in range(depth):
        q, k, v, g, beta, zs, ypg, sga, sgb, cnew, pnew = _mixin_prompt(
            li, xp, *mix_w, batch=batch, seq=seq, tile=MIXIN_TILE)
        o, s_new = _delta_prompt(q, k, v, g, beta, batch=batch, seq=seq, tile=DELTA_TILE, nseq=DELTA_SEQS)
        xp = _merge(li, o, zs, ypg, sga, sgb, xp, pp, *merge_w, tile=MERGE_TILE, final=(li == depth - 1))
        conv_p.append(cnew)
        delta_p.append(s_new)
        pool_p.append(pnew)
    y_prompt = xp.reshape(batch, seq, D_MODEL)

    swap = lambda a: jnp.swapaxes(a, 0, 1)
    flat = lambda a: a.reshape(-1, a.shape[-1])
    xs = swap(x_sample)
    ps = jnp.swapaxes(p_sample, 1, 2).reshape(depth, steps * nb, -1)
    conv_t, pool_t = jnp.swapaxes(state_conv, 1, 2), jnp.swapaxes(state_pool, 1, 2)
    new_states, delta_s = None, None
    for li in range(depth):
        q, k, v, g, beta, zs, ypg, sga, sgb, *new_states = _mixin_sample(
            li, xs, *mix_w, conv_t, pool_t, new_states, bb=SAMPLE_SEQS, pos0=PAST_LEN)
        wkqd, wv, kt, qkd, glx = _sample_prep(flat(q), flat(k), flat(v), flat(g), flat(beta), steps=steps, nb=nb)
        r, delta_s = _sample_state(li, state_delta, wkqd, wv, kt, glx, delta_s, bb=STATE_SEQS)
        o = _sample_out(r, wv, qkd, steps=steps, nb=nb)
        xs = _merge(li, o, flat(zs), flat(ypg), flat(sga), flat(sgb), flat(xs), ps, *merge_w, tile=steps * nb,
                    final=(li == depth - 1)).reshape(steps, nb, D_MODEL)
    y_sample = swap(xs)
    conv_s, pool_s = (jnp.swapaxes(a, 1, 2) for a in new_states)

    return (y_prompt, y_sample, jnp.stack(conv_p), jnp.stack(delta_p), jnp.stack(pool_p),
            conv_s, delta_s, pool_s)
```

```python
import functools

import jax
import jax.numpy as jnp
from jax import lax
from jax.experimental import pallas as pl
from jax.experimental.pallas import tpu as pltpu

F32 = jnp.float32
BF16 = jnp.bfloat16

D_MODEL = 1024
N_HEADS = 8
HEAD_DIM = 128
CONV_W = 4
CONV_CH = 3 * D_MODEL
POOL_WINDOWS = (2, 4, 8, 16)
POOL_HIST = 15
POOL_GC = 256
EPS = 1e-6
PAST_LEN = 16384
CHUNK = 64
BLOCK = 2 * CHUNK
VMEM_LIMIT = 56 * 1024 * 1024

MIXIN_TILE = 512
DELTA_TILE, DELTA_SEQS = 256, 4
MERGE_TILE = 512
SAMPLE_SEQS = 32
STATE_SEQS, STATE_GROUP = 16, 2
WPREP_TK = 128

_SPLIT = 4096
_C_QKV, _C_Z, _C_U, _C_GP, _C_GA, _C_GB, _C_A, _C_B = 0, 3072, 4096, 5120, 6144, 7168, 8192, 8320
_W_COLS = 8448


def _dot(a, b, nt=False):
    dims = (((1,), (1,)), ((), ())) if nt else (((1,), (0,)), ((), ()))
    return lax.dot_general(a, b, dims, preferred_element_type=F32)


def _mm(a, b, nt=False):
    return _dot(a.astype(BF16), b.astype(BF16), nt)


def _split3(x):
    x0 = x.astype(BF16)
    r = x - x0.astype(F32)
    x1 = r.astype(BF16)
    x2 = (r - x1.astype(F32)).astype(BF16)
    return x0, x1, x2


def _mm_exact_rhs(a_bf, b):
    b0, b1, b2 = _split3(b)
    return _dot(a_bf, b0) + _dot(a_bf, b1) + _dot(a_bf, b2)


def _mm_exact_lhs(a, b_bf):
    a0, a1, a2 = _split3(a)
    return _dot(a0, b_bf) + _dot(a1, b_bf) + _dot(a2, b_bf)


def _silu(x):
    return x * jax.nn.sigmoid(x)


def _rmsnorm(x, w):
    return x * lax.rsqrt(jnp.mean(x * x, axis=-1, keepdims=True) + EPS) * w


def _full(shape):
    n = len(shape)
    return pl.BlockSpec(shape, lambda *_: (0,) * n, pipeline_mode=pl.Buffered(1))


def _layer(li, shape):
    n = len(shape)
    return pl.BlockSpec((None,) + tuple(shape), lambda *_: (li,) + (0,) * n, pipeline_mode=pl.Buffered(1))


def _wprep_body(wt_ref, o_ref):
    n_tail = wt_ref.shape[0] - _SPLIT - 2 * N_HEADS
    for g in range(_SPLIT // 128):
        o_ref[:, g * 128:(g + 1) * 128] = wt_ref[g * 128:(g + 1) * 128, :].T.astype(BF16)
    for g in range(n_tail // 128):
        r0 = _SPLIT + 2 * N_HEADS + g * 128
        o_ref[:, _C_U + g * 128:_C_U + (g + 1) * 128] = wt_ref[r0:r0 + 128, :].T.astype(BF16)
    ab = wt_ref[_SPLIT:_SPLIT + 128, :].T
    lane = lax.broadcasted_iota(jnp.int32, ab.shape, 1)
    o_ref[:, _C_A:_C_A + 128] = jnp.where(lane < N_HEADS, ab, 0.0).astype(BF16)
    o_ref[:, _C_B:_C_B + 128] = jnp.where(lane < N_HEADS, pltpu.roll(ab, 128 - N_HEADS, 1), 0.0).astype(BF16)


def _wprep(w_in, *, tk):
    depth, k, n = w_in.shape
    return pl.pallas_call(
        _wprep_body, grid=(depth, k // tk),
        in_specs=[pl.BlockSpec((None, n, tk), lambda l, r: (l, 0, r))],
        out_specs=pl.BlockSpec((None, tk, _W_COLS), lambda l, r: (l, r, 0)),
        out_shape=jax.ShapeDtypeStruct((depth, k, _W_COLS), BF16),
        compiler_params=pltpu.CompilerParams(dimension_semantics=("arbitrary", "arbitrary"),
                                             vmem_limit_bytes=VMEM_LIMIT),
        name="wprep",
    )(jnp.swapaxes(w_in, 1, 2))


def _mixin_core(h, w_ref, cw_ref, alog_ref, dtb_ref, pw_ref, ps_ref, outs, pos, conv_tap, pool_sums, store):
    q_ref, k_ref, v_ref, g_ref, b_ref, zs_ref, ypg_ref, sga_ref, sgb_ref = outs
    proj = lambda c0, n=D_MODEL: _dot(h, w_ref[:, c0:c0 + n])
    xcs = [proj(_C_QKV + seg * D_MODEL) for seg in range(3)]
    gates = [proj(_C_Z), proj(_C_GA), proj(_C_GB)]
    a_r = proj(_C_A, 128)
    b_r = proj(_C_B, 128)
    u = proj(_C_U)
    gp = proj(_C_GP)

    for seg, out_ref in enumerate((q_ref, k_ref, v_ref)):
        c0 = seg * D_MODEL
        xc = xcs[seg]
        acc = xc * cw_ref[CONV_W - 1:CONV_W, c0:c0 + D_MODEL]
        for j in range(CONV_W - 1):
            acc = acc + conv_tap(j, c0, xc) * cw_ref[j:j + 1, c0:c0 + D_MODEL]
        y = _silu(acc)
        if seg < 2:
            for hh in range(N_HEADS):
                ls = slice(hh * HEAD_DIM, (hh + 1) * HEAD_DIM)
                yh = y[:, ls]
                inv = lax.rsqrt(jnp.sum(yh * yh, axis=-1, keepdims=True) + EPS)
                if seg == 0:
                    inv = inv * (HEAD_DIM ** -0.5)
                store(out_ref, yh * inv, ls)
        else:
            store(out_ref, y, None)

    store(zs_ref, _silu(gates[0]), None)
    store(sga_ref, jax.nn.sigmoid(gates[1]), None)
    store(sgb_ref, jax.nn.sigmoid(gates[2]), None)
    xs = a_r + dtb_ref[...]
    softplus = jnp.maximum(xs, 0.0) + jnp.log1p(jnp.exp(-jnp.abs(xs)))
    store(g_ref, -jnp.exp(alog_ref[...]) * softplus, None)
    store(b_ref, jax.nn.sigmoid(b_r), None)

    sgp = _silu(gp)
    sums = pool_sums(u)
    for gi, win in enumerate(POOL_WINDOWS):
        l0 = gi * POOL_GC
        ls = slice(l0, l0 + POOL_GC)
        count = jnp.minimum(pos + 1, win).astype(F32)
        y = sums[gi] / count - u[:, ls]
        yp = _mm(y, pw_ref[gi]) * ps_ref[:, ls]
        store(ypg_ref, yp * sgp[:, ls], ls)
    return xcs, u


def _mixin_weight_specs(li):
    return [_layer(li, (1, D_MODEL)), _layer(li, (D_MODEL, _W_COLS)), _layer(li, (CONV_W, CONV_CH)),
            _layer(li, (1, 128)), _layer(li, (1, 128)), _layer(li, (4, POOL_GC, POOL_GC)), _layer(li, (1, D_MODEL))]


def _mixin_prompt_body(x_ref, nm_ref, w_ref, cw_ref, alog_ref, dtb_ref, pw_ref, ps_ref,
                       q_ref, k_ref, v_ref, g_ref, b_ref, zs_ref, ypg_ref, sga_ref, sgb_ref, cnew_ref, pnew_ref,
                       cext, pext, *, rows):
    hc, hp = 8, 16
    t = pl.program_id(1)

    @pl.when(t == 0)
    def _():
        cext[0:hc, :] = jnp.zeros((hc, CONV_CH), F32)
        pext[0:hp, :] = jnp.zeros((hp, D_MODEL), F32)

    h = _rmsnorm(x_ref[...], nm_ref[...]).astype(BF16)

    def conv_tap(j, c0, xc):
        if j == 0:
            cext[hc:hc + rows, c0:c0 + D_MODEL] = xc
        off = hc - (CONV_W - 1 - j)
        return cext[off:off + rows, c0:c0 + D_MODEL]

    def pool_sums(u):
        pext[hp:hp + rows, :] = u
        lvl = pext[...]
        out = []
        for gi, win in enumerate(POOL_WINDOWS):
            lvl = lvl + pltpu.roll(lvl, win // 2, 0)
            out.append(lvl[hp:, :POOL_GC])
            lvl = lvl[:, POOL_GC:]
        return out

    def store(ref, val, ls):
        if ls is None:
            ref[...] = val.astype(ref.dtype)
        else:
            ref[:, ls] = val.astype(ref.dtype)

    pos = t * rows + lax.broadcasted_iota(jnp.int32, (rows, 1), 0)
    outs = (q_ref, k_ref, v_ref, g_ref, b_ref, zs_ref, ypg_ref, sga_ref, sgb_ref)
    _mixin_core(h, w_ref, cw_ref, alog_ref, dtb_ref, pw_ref, ps_ref, outs, pos, conv_tap, pool_sums, store)
    cnew_ref[...] = cext[hc + rows - (CONV_W - 1):hc + rows, :]
    pnew_ref[...] = pext[hp + rows - POOL_HIST:hp + rows, :]
    cext[0:hc, :] = cext[rows:rows + hc, :]
    pext[0:hp, :] = pext[rows:rows + hp, :]


def _mixin_prompt(li, x, nm, w_big, cw, alog, dtb, pw, ps, *, batch, seq, tile):
    m = x.shape[0]
    nt = seq // tile
    row_spec = lambda c: pl.BlockSpec((tile, c), lambda b, t: (b * nt + t, 0))
    in_specs = [row_spec(D_MODEL)] + _mixin_weight_specs(li)
    cnew_spec = pl.BlockSpec((None, CONV_W - 1, CONV_CH), lambda b, t: (b, 0, 0))
    pnew_spec = pl.BlockSpec((None, POOL_HIST, D_MODEL), lambda b, t: (b, 0, 0))
    big = jax.ShapeDtypeStruct((m, D_MODEL), F32)
    small = jax.ShapeDtypeStruct((m, 128), F32)
    gate = jax.ShapeDtypeStruct((m, D_MODEL), BF16)
    out_shape = [big, big, big, small, small, gate, gate, gate, gate,
                 jax.ShapeDtypeStruct((batch, CONV_W - 1, CONV_CH), F32),
                 jax.ShapeDtypeStruct((batch, POOL_HIST, D_MODEL), F32)]
    out_specs = [row_spec(D_MODEL)] * 3 + [row_spec(128)] * 2 + [row_spec(D_MODEL)] * 4 + [cnew_spec, pnew_spec]
    return pl.pallas_call(
        functools.partial(_mixin_prompt_body, rows=tile),
        grid=(batch, nt), in_specs=in_specs, out_specs=out_specs, out_shape=out_shape,
        scratch_shapes=[pltpu.VMEM((8 + tile, CONV_CH), F32), pltpu.VMEM((16 + tile, D_MODEL), F32)],
        compiler_params=pltpu.CompilerParams(dimension_semantics=("arbitrary", "arbitrary"),
                                             vmem_limit_bytes=VMEM_LIMIT),
        name="mixin_prompt",
    )(x, nm, w_big, cw, alog, dtb, pw, ps)


def _mixin_sample_body(*refs, steps, bb, pos0):
    x_ref, nm_ref, w_ref, cw_ref, alog_ref, dtb_ref, pw_ref, ps_ref, chist_ref, phist_ref = refs[:10]
    q_ref, k_ref, v_ref, g_ref, b_ref, zs_ref, ypg_ref, sga_ref, sgb_ref, cnew_ref, pnew_ref = refs[-11:]
    rows = steps * bb
    h = _rmsnorm(x_ref[...].reshape(rows, D_MODEL), nm_ref[...]).astype(BF16)

    def delayed(new, hist_ref, nhist, d, ls_new, ls_hist):
        parts = []
        for t in range(steps):
            src = t - d
            parts.append(new[src * bb:(src + 1) * bb, ls_new] if src >= 0 else hist_ref[nhist + src, :, ls_hist])
        return jnp.concatenate(parts, axis=0)

    def conv_tap(j, c0, xc):
        return delayed(xc, chist_ref, CONV_W - 1, CONV_W - 1 - j, slice(None), slice(c0, c0 + D_MODEL))

    def pool_sums(u):
        out = []
        for gi, win in enumerate(POOL_WINDOWS):
            ls = slice(gi * POOL_GC, (gi + 1) * POOL_GC)
            acc = u[:, ls]
            for s in range(1, win):
                acc = acc + delayed(u, phist_ref, POOL_HIST, s, ls, ls)
            out.append(acc)
        return out

    def store(ref, val, ls):
        val = val.reshape(steps, bb, val.shape[-1]).astype(ref.dtype)
        if ls is None:
            ref[...] = val
        else:
            ref[:, :, ls] = val

    pos = pos0 + lax.broadcasted_iota(jnp.int32, (rows, 1), 0) // bb
    outs = (q_ref, k_ref, v_ref, g_ref, b_ref, zs_ref, ypg_ref, sga_ref, sgb_ref)
    xcs, u = _mixin_core(h, w_ref, cw_ref, alog_ref, dtb_ref, pw_ref, ps_ref, outs, pos, conv_tap, pool_sums, store)
    for i in range(CONV_W - 1):
        src = steps + i - (CONV_W - 1)
        for seg in range(3):
            ls = slice(seg * D_MODEL, (seg + 1) * D_MODEL)
            cnew_ref[i, :, ls] = (xcs[seg][src * bb:(src + 1) * bb, :] if src >= 0
                                  else chist_ref[CONV_W - 1 + src, :, ls])
    for i in range(POOL_HIST):
        src = steps + i - POOL_HIST
        pnew_ref[i] = u[src * bb:(src + 1) * bb, :] if src >= 0 else phist_ref[POOL_HIST + src]


def _mixin_sample(li, x, nm, w_big, cw, alog, dtb, pw, ps, chist, phist, prev, *, bb, pos0):
    steps, nb, _ = x.shape
    slab_spec = lambda n, c: pl.BlockSpec((n, bb, c), lambda i: (0, i, 0))
    state_spec = lambda n, c: pl.BlockSpec((None, n, bb, c), lambda i: (li, 0, i, 0))
    in_specs = ([slab_spec(steps, D_MODEL)] + _mixin_weight_specs(li)
                + [state_spec(CONV_W - 1, CONV_CH), state_spec(POOL_HIST, D_MODEL)])
    args = [x, nm, w_big, cw, alog, dtb, pw, ps, chist, phist]
    aliases = {}
    if prev is not None:
        in_specs += [pl.BlockSpec(memory_space=pl.ANY)] * 2
        aliases = {len(args): 9, len(args) + 1: 10}
        args += list(prev)
    big = jax.ShapeDtypeStruct((steps, nb, D_MODEL), F32)
    small = jax.ShapeDtypeStruct((steps, nb, 128), F32)
    gate = jax.ShapeDtypeStruct((steps, nb, D_MODEL), BF16)
    out_shape = [big, big, big, small, small, gate, gate, gate, gate,
                 jax.ShapeDtypeStruct(chist.shape, F32), jax.ShapeDtypeStruct(phist.shape, F32)]
    out_specs = ([slab_spec(steps, D_MODEL)] * 3 + [slab_spec(steps, 128)] * 2 + [slab_spec(steps, D_MODEL)] * 4
                 + [state_spec(CONV_W - 1, CONV_CH), state_spec(POOL_HIST, D_MODEL)])
    return pl.pallas_call(
        functools.partial(_mixin_sample_body, steps=steps, bb=bb, pos0=pos0),
        grid=(nb // bb,), in_specs=in_specs, out_specs=out_specs, out_shape=out_shape,
        input_output_aliases=aliases,
        compiler_params=pltpu.CompilerParams(dimension_semantics=("arbitrary",), vmem_limit_bytes=VMEM_LIMIT),
        name="mixin_sample",
    )(*args)


def _delta_body(q_ref, k_ref, v_ref, g_ref, b_ref, o_ref, sout_ref, s_ref, *, tile, nseq):
    t = pl.program_id(1)

    @pl.when(t == 0)
    def _():
        s_ref[...] = jnp.zeros(s_ref.shape, F32)

    r = lax.broadcasted_iota(jnp.int32, (BLOCK, BLOCK), 0)
    c = lax.broadcasted_iota(jnp.int32, (BLOCK, BLOCK), 1)
    same = (r // CHUNK) == (c // CHUNK)
    low = same & (r >= c)
    strict = same & (r > c)
    eye = jnp.where(r == c, 1.0, 0.0).astype(F32)
    l_blk = jnp.where(low, 1.0, 0.0).astype(BF16)
    ones0 = jnp.where(c < CHUNK, 1.0, 0.0).astype(BF16)
    ones1 = jnp.where(c >= CHUNK, 1.0, 0.0).astype(BF16)
    first = lax.broadcasted_iota(jnp.int32, (BLOCK, 1), 0) < CHUNK
    zeros_half = jnp.zeros((CHUNK, HEAD_DIM), F32)
    cat = jnp.concatenate
    col = lambda a, hh: a[:, hh:hh + 1]
    units = [(sq, hh) for sq in range(nseq) for hh in range(N_HEADS)]
    n_units = range(len(units))
    ls = [slice(hh * HEAD_DIM, (hh + 1) * HEAD_DIM) for hh in range(N_HEADS)]

    def block(i, carry):
        rows = pl.ds(pl.multiple_of(i * BLOCK, BLOCK), BLOCK)
        gp = [g_ref[sq, rows, :] for sq in range(nseq)]
        bt = [b_ref[sq, rows, :] for sq in range(nseq)]
        g_cum = [_mm_exact_rhs(l_blk, x) for x in gp]
        tot0 = [_mm_exact_rhs(ones0, x) for x in gp]
        tot1 = [_mm_exact_rhs(ones1, x) for x in gp]
        g_t = [x.T for x in g_cum]
        e_g = [jnp.exp(x) for x in g_cum]
        e_tail = [jnp.exp(jnp.where(first, tot0[sq], tot1[sq]) - g_cum[sq]) for sq in range(nseq)]
        e_tot = ([jnp.exp(x) for x in tot0], [jnp.exp(x) for x in tot1])

        kh = [k_ref[sq, rows, ls[hh]] for sq, hh in units]
        qh = [q_ref[sq, rows, ls[hh]] for sq, hh in units]
        beta = [col(bt[sq], hh) for sq, hh in units]
        eg = [col(e_g[sq], hh) for sq, hh in units]
        decay = [jnp.exp(jnp.where(low, col(g_cum[sq], hh) - g_t[sq][hh:hh + 1, :], -jnp.inf)) for sq, hh in units]
        kq = [_mm(cat([kh[n], qh[n]], axis=0), kh[n], nt=True) for n in n_units]
        p = [-jnp.where(strict, kq[n][:BLOCK] * decay[n] * beta[n], 0.0) for n in n_units]
        qk = [kq[n][BLOCK:] * decay[n] for n in n_units]
        tinv = [eye + p[n] for n in n_units]
        p = [_mm(p[n], p[n]) for n in n_units]
        for _ in range(4):
            pp = [_mm(p[n], cat([p[n], tinv[n]], axis=1)) for n in n_units]
            p = [pp[n][:, :BLOCK] for n in n_units]
            tinv = [tinv[n] + pp[n][:, BLOCK:] for n in n_units]
        tinv = [tinv[n] + _mm(p[n], tinv[n]) for n in n_units]
        sol = [_mm(tinv[n], cat([v_ref[sq, rows, ls[hh]] * beta[n], kh[n] * (beta[n] * eg[n])], axis=1))
               for n, (sq, hh) in enumerate(units)]
        wv = [sol[n][:, :HEAD_DIM] for n in n_units]
        wk = [sol[n][:, HEAD_DIM:] for n in n_units]
        qd = [qh[n] * eg[n] for n in n_units]
        kt_t = [(kh[n] * col(e_tail[sq], hh)).T for n, (sq, hh) in enumerate(units)]
        s_cur = [s_ref[sq, hh] for sq, hh in units]
        o_parts = []
        for half in range(2):
            hs = slice(half * CHUNK, (half + 1) * CHUNK)
            res = [_mm(cat([wk[n][hs], qd[n][hs]], axis=0), s_cur[n]) for n in n_units]
            u_new = [wv[n][hs] - res[n][:CHUNK] for n in n_units]
            u_pad = [cat([u_new[n], zeros_half] if half == 0 else [zeros_half, u_new[n]], axis=0) for n in n_units]
            upd = [_mm(cat([qk[n][hs], kt_t[n]], axis=0), u_pad[n]) for n in n_units]
            o_parts.append([res[n][CHUNK:] + upd[n][:CHUNK] for n in n_units])
            s_cur = [s_cur[n] * col(e_tot[half][sq], hh) + upd[n][CHUNK:] for n, (sq, hh) in enumerate(units)]
        for n, (sq, hh) in enumerate(units):
            s_ref[sq, hh] = s_cur[n]
            o_ref[sq, rows, ls[hh]] = cat([o_parts[0][n], o_parts[1][n]], axis=0)
        return carry

    lax.fori_loop(0, tile // BLOCK, block, 0)
    sout_ref[...] = s_ref[...]


def _delta_prompt(q, k, v, g, beta, *, batch, seq, tile, nseq):
    as3d = lambda a: a.reshape(batch, seq, a.shape[-1])
    row_spec = lambda c: pl.BlockSpec((nseq, tile, c), lambda b, t: (b, t, 0))
    s_shape = (N_HEADS, HEAD_DIM, HEAD_DIM)
    o, s_new = pl.pallas_call(
        functools.partial(_delta_body, tile=tile, nseq=nseq),
        grid=(batch // nseq, seq // tile),
        in_specs=[row_spec(D_MODEL)] * 3 + [row_spec(128)] * 2,
        out_specs=[row_spec(D_MODEL), pl.BlockSpec((nseq,) + s_shape, lambda b, t: (b, 0, 0, 0))],
        out_shape=[jax.ShapeDtypeStruct((batch, seq, D_MODEL), F32), jax.ShapeDtypeStruct((batch,) + s_shape, F32)],
        scratch_shapes=[pltpu.VMEM((nseq,) + s_shape, F32)],
        compiler_params=pltpu.CompilerParams(dimension_semantics=("arbitrary", "arbitrary"),
                                             vmem_limit_bytes=VMEM_LIMIT),
        name="delta_prompt",
    )(as3d(q), as3d(k), as3d(v), as3d(g), as3d(beta))
    return o.reshape(batch * seq, D_MODEL), s_new


def _head_indicators():
    d = lax.broadcasted_iota(jnp.int32, (D_MODEL, 128), 0) // HEAD_DIM
    hcol = lax.broadcasted_iota(jnp.int32, (D_MODEL, 128), 1)
    e_sum = jnp.where(d == hcol, 1.0, 0.0).astype(BF16)
    hrow = lax.broadcasted_iota(jnp.int32, (128, D_MODEL), 0)
    d2 = lax.broadcasted_iota(jnp.int32, (128, D_MODEL), 1) // HEAD_DIM
    e_bc = jnp.where(hrow == d2, 1.0, 0.0).astype(BF16)
    return e_sum, e_bc


def _sample_prep_body(q_ref, k_ref, v_ref, g_ref, b_ref, wkqd_ref, wv_ref, kt_ref, qkd_ref, glx_ref,
                      *, steps, nb):
    e_sum, e_bc = _head_indicators()
    expand = lambda x: _mm_exact_lhs(x, e_bc)
    hsum = lambda y: _mm_exact_lhs(y, e_sum)
    sl = lambda ref, i: ref[i * nb:(i + 1) * nb, :]
    q = [sl(q_ref, i) for i in range(steps)]
    k = [sl(k_ref, i) for i in range(steps)]
    v = [sl(v_ref, i) for i in range(steps)]
    beta = [sl(b_ref, i) for i in range(steps)]
    g_cum = []
    for i in range(steps):
        gi = sl(g_ref, i)
        g_cum.append(gi if i == 0 else g_cum[-1] + gi)
    wv, wk = [], []
    for i in range(steps):
        acc_v = v[i] * expand(beta[i])
        acc_k = k[i] * expand(beta[i] * jnp.exp(g_cum[i]))
        for j in range(i):
            a_ij = expand(hsum(k[i] * k[j]) * jnp.exp(g_cum[i] - g_cum[j]) * beta[i])
            acc_v = acc_v - a_ij * wv[j]
            acc_k = acc_k - a_ij * wk[j]
        wv.append(acc_v)
        wk.append(acc_k)
    zeros = jnp.zeros((nb, D_MODEL), F32)
    for i in range(steps):
        wkqd_ref[:, i, :] = wk[i]
        wkqd_ref[:, steps + i, :] = q[i] * expand(jnp.exp(g_cum[i]))
        wv_ref[:, i, :] = wv[i]
        wv_ref[:, steps + i, :] = zeros
        kt_ref[:, i, :] = k[i] * expand(jnp.exp(g_cum[steps - 1] - g_cum[i]))
        kt_ref[:, steps + i, :] = zeros
        for j in range(steps):
            idx = i * steps + j
            if j <= i:
                qkd_ref[idx * nb:(idx + 1) * nb, :] = hsum(q[i] * k[j]) * jnp.exp(g_cum[i] - g_cum[j])
            else:
                qkd_ref[idx * nb:(idx + 1) * nb, :] = jnp.zeros((nb, 128), F32)
    glx_ref[...] = expand(jnp.exp(g_cum[steps - 1]))


def _sample_prep(q, k, v, g, beta, *, steps, nb):
    m = steps * nb
    out_shape = [jax.ShapeDtypeStruct((nb, 2 * steps, D_MODEL), F32)] * 3 + [
        jax.ShapeDtypeStruct((steps * steps * nb, 128), F32), jax.ShapeDtypeStruct((nb, D_MODEL), F32)]
    return pl.pallas_call(
        functools.partial(_sample_prep_body, steps=steps, nb=nb),
        grid=(1,),
        in_specs=[_full((m, D_MODEL))] * 3 + [_full((m, 128))] * 2,
        out_specs=[_full(s.shape) for s in out_shape],
        out_shape=out_shape,
        compiler_params=pltpu.CompilerParams(dimension_semantics=("arbitrary",), vmem_limit_bytes=VMEM_LIMIT),
        name="sample_prep",
    )(q, k, v, g, beta)


def _sample_state_body(*refs, bb, group):
    s_ref, wkqd_ref, wv_ref, kt_ref, glx_ref = refs[:5]
    r_ref, snew_ref = refs[-2:]
    slots = wkqd_ref.shape[1]
    zeros_pad = jnp.zeros((HEAD_DIM - slots, HEAD_DIM), F32)

    def per_group(gi, carry):
        units = [(gi * group + j, hh) for j in range(group) for hh in range(N_HEADS)]
        ls = [slice(hh * HEAD_DIM, (hh + 1) * HEAD_DIM) for hh in range(N_HEADS)]
        pad = lambda a, n: jnp.concatenate([a, zeros_pad[:n - slots]], axis=0)
        s0 = [s_ref[b, hh] for b, hh in units]
        res = [_mm(pad(wkqd_ref[b, :, ls[hh]], 2 * slots), s0[n])[:slots] for n, (b, hh) in enumerate(units)]
        kt_t = [pad(kt_ref[b, :, ls[hh]], HEAD_DIM).T for b, hh in units]
        upd = [_mm(kt_t[n], pad(wv_ref[b, :, ls[hh]] - res[n], HEAD_DIM)) for n, (b, hh) in enumerate(units)]
        for n, (b, hh) in enumerate(units):
            r_ref[b, :, ls[hh]] = res[n]
            snew_ref[b, hh] = s0[n] * glx_ref[b, :, ls[hh]] + upd[n]
        return carry

    lax.fori_loop(0, bb // group, per_group, 0)


def _sample_state(li, state, wkqd, wv, kt, glx, prev, *, bb):
    nb = state.shape[1]
    s_spec = pl.BlockSpec((None, bb, N_HEADS, HEAD_DIM, HEAD_DIM), lambda i: (li, i, 0, 0, 0))
    slots = wkqd.shape[1]
    slot_spec = pl.BlockSpec((bb, slots, D_MODEL), lambda i: (i, 0, 0))
    in_specs = [s_spec, slot_spec, slot_spec, slot_spec, pl.BlockSpec((bb, 1, D_MODEL), lambda i: (i, 0, 0))]
    args = [state, wkqd, wv, kt, glx.reshape(nb, 1, D_MODEL)]
    aliases = {}
    if prev is not None:
        in_specs.append(pl.BlockSpec(memory_space=pl.ANY))
        args.append(prev)
        aliases = {len(args) - 1: 1}
    return pl.pallas_call(
        functools.partial(_sample_state_body, bb=bb, group=STATE_GROUP),
        grid=(nb // bb,),
        in_specs=in_specs,
        out_specs=[slot_spec, s_spec],
        out_shape=[jax.ShapeDtypeStruct((nb, slots, D_MODEL), F32), jax.ShapeDtypeStruct(state.shape, F32)],
        input_output_aliases=aliases,
        compiler_params=pltpu.CompilerParams(dimension_semantics=("arbitrary",), vmem_limit_bytes=VMEM_LIMIT),
        name="sample_state",
    )(*args)


def _sample_out_body(r_ref, wv_ref, qkd_ref, o_ref, *, steps, nb):
    _, e_bc = _head_indicators()
    sl = lambda ref, i: ref[i * nb:(i + 1) * nb, :]
    u = [wv_ref[:, j, :] - r_ref[:, j, :] for j in range(steps)]
    for i in range(steps):
        acc = r_ref[:, steps + i, :]
        for j in range(i + 1):
            acc = acc + _mm_exact_lhs(sl(qkd_ref, i * steps + j), e_bc) * u[j]
        o_ref[i * nb:(i + 1) * nb, :] = acc


def _sample_out(r, wv, qkd, *, steps, nb):
    m = steps * nb
    return pl.pallas_call(
        functools.partial(_sample_out_body, steps=steps, nb=nb),
        grid=(1,),
        in_specs=[_full(r.shape), _full(wv.shape), _full(qkd.shape)],
        out_specs=_full((m, D_MODEL)),
        out_shape=jax.ShapeDtypeStruct((m, D_MODEL), F32),
        compiler_params=pltpu.CompilerParams(dimension_semantics=("arbitrary",), vmem_limit_bytes=VMEM_LIMIT),
        name="sample_out",
    )(r, wv, qkd)


def _merge_body(o_ref, zs_ref, ypg_ref, sga_ref, sgb_ref, x_ref, p_ref, gn_ref, wpa_ref, wpb_ref, wout_ref,
                npl_ref, wpg_ref, wpp_ref, fn_ref, y_ref, *, final, parts):
    step = o_ref.shape[0] // parts
    groups = [slice(i * step, (i + 1) * step) for i in range(parts)]
    gn = gn_ref[...]
    y_a = []
    for rs in groups:
        gated = []
        for hh in range(N_HEADS):
            ls = slice(hh * HEAD_DIM, (hh + 1) * HEAD_DIM)
            oh = o_ref[rs, ls]
            on = oh * lax.rsqrt(jnp.mean(oh * oh, axis=-1, keepdims=True) + EPS) * gn
            gated.append((on * zs_ref[rs, ls]).astype(BF16))
        y_a.append(_dot(jnp.concatenate(gated, axis=1), wpa_ref[...]))
    y_b = [_mm(ypg_ref[rs, :], wpb_ref[...]) for rs in groups]
    m = [sga_ref[rs, :] * y_a[i] + sgb_ref[rs, :] * y_b[i] for i, rs in enumerate(groups)]
    x1 = [x_ref[rs, :] + _mm(m[i], wout_ref[...]) for i, rs in enumerate(groups)]
    gate = [jax.nn.sigmoid(_mm(_rmsnorm(x1[i], npl_ref[...]), wpg_ref[...])) for i in range(parts)]
    pe = [_mm(p_ref[rs, :], wpp_ref[...]) for rs in groups]
    for i, rs in enumerate(groups):
        x2 = x1[i] + gate[i] * pe[i]
        if final:
            x2 = _rmsnorm(x2, fn_ref[...])
        y_ref[rs, :] = x2


def _merge(li, o, zs, ypg, sga, sgb, x, p, gn, wpa, wpb, wout, npl, wpg, wpp, fn, *, tile, final):
    m = x.shape[0]
    ple = p.shape[-1]
    row_spec = lambda c: pl.BlockSpec((tile, c), lambda i: (i, 0))
    sq = _layer(li, (D_MODEL, D_MODEL))
    return pl.pallas_call(
        functools.partial(_merge_body, final=final, parts=2),
        grid=(m // tile,),
        in_specs=[row_spec(D_MODEL)] * 6 + [pl.BlockSpec((None, tile, ple), lambda i: (li, i, 0)),
                                            _layer(li, (1, HEAD_DIM)), sq, sq, sq, _layer(li, (1, D_MODEL)),
                                            sq, _layer(li, (ple, D_MODEL)), _full((1, D_MODEL))],
        out_specs=row_spec(D_MODEL),
        out_shape=jax.ShapeDtypeStruct((m, D_MODEL), F32),
        compiler_params=pltpu.CompilerParams(dimension_semantics=("arbitrary",), vmem_limit_bytes=VMEM_LIMIT),
        name="merge",
    )(o, zs, ypg, sga, sgb, x, p, gn, wpa, wpb, wout, npl, wpg, wpp, fn)


def kernel(x_prompt, x_sample, p_prompt, p_sample, state_conv, state_delta, state_pool, norm_mix, w_in, conv_w,
           a_log, dt_bias, gdn_norm, w_proj_a, pool_w, pool_scale, w_proj_b, w_out, norm_ple, w_ple_gate,
           w_ple_proj, final_norm):
    depth = w_in.shape[0]
    batch, seq, _ = x_prompt.shape
    nb, steps, _ = x_sample.shape
    rowvec = lambda a: a.reshape(depth, 1, -1)
    pad128 = lambda a: jnp.pad(a, ((0, 0), (0, 128 - a.shape[1]))).reshape(depth, 1, 128)
    mix_w = (rowvec(norm_mix), _wprep(w_in, tk=WPREP_TK), conv_w, pad128(a_log), pad128(dt_bias),
             pool_w.astype(BF16), rowvec(pool_scale))
    merge_w = (rowvec(gdn_norm), w_proj_a.astype(BF16), w_proj_b.astype(BF16), w_out.astype(BF16),
               rowvec(norm_ple), w_ple_gate.astype(BF16), w_ple_proj.astype(BF16), final_norm.reshape(1, -1))

    xp = x_prompt.reshape(batch * seq, D_MODEL)
    pp = p_prompt.reshape(depth, batch * seq, -1)
    conv_p, delta_p, pool_p = [], [], []
    for li in range(depth):
        q, k, v, g, beta, zs, ypg, sga, sgb, cnew, pnew = _mixin_prompt(
            li, xp, *mix_w, batch=batch, seq=seq, tile=MIXIN_TILE)
        o, s_new = _delta_prompt(q, k, v, g, beta, batch=batch, seq=seq, tile=DELTA_TILE, nseq=DELTA_SEQS)
        xp = _merge(li, o, zs, ypg, sga, sgb, xp, pp, *merge_w, tile=MERGE_TILE, final=(li == depth - 1))
        conv_p.append(cnew)
        delta_p.append(s_new)
        pool_p.append(pnew)
    y_prompt = xp.reshape(batch, seq, D_MODEL)

    swap = lambda a: jnp.swapaxes(a, 0, 1)
    flat = lambda a: a.reshape(-1, a.shape[-1])
    xs = swap(x_sample)
    ps = jnp.swapaxes(p_sample, 1, 2).reshape(depth, steps * nb, -1)
    conv_t, pool_t = jnp.swapaxes(state_conv, 1, 2), jnp.swapaxes(state_pool, 1, 2)
    new_states, delta_s = None, None
    for li in range(depth):
        q, k, v, g, beta, zs, ypg, sga, sgb, *new_states = _mixin_sample(
            li, xs, *mix_w, conv_t, pool_t, new_states, bb=SAMPLE_SEQS, pos0=PAST_LEN)
        wkqd, wv, kt, qkd, glx = _sample_prep(flat(q), flat(k), flat(v), flat(g), flat(beta), steps=steps, nb=nb)
        r, delta_s = _sample_state(li, state_delta, wkqd, wv, kt, glx, delta_s, bb=STATE_SEQS)
        o = _sample_out(r, wv, qkd, steps=steps, nb=nb)
        xs = _merge(li, o, flat(zs), flat(ypg), flat(sga), flat(sgb), flat(xs), ps, *merge_w, tile=steps * nb,
                    final=(li == depth - 1)).reshape(steps, nb, D_MODEL)
    y_sample = swap(xs)
    conv_s, pool_s = (jnp.swapaxes(a, 1, 2) for a in new_states)

    return (y_prompt, y_sample, jnp.stack(conv_p), jnp.stack(delta_p), jnp.stack(pool_p),
            conv_s, delta_s, pool_s)
```
